```python
import jax, jax.numpy as jnp
from jax import lax
import numpy as np

D_MODEL = 2048
BATCH = 4
SEQ = 8192
DEPTH = 4

CHUNK = 64
N_MIXERS = 3

MLSTM_HEADS = 4
MLSTM_DQK = D_MODEL // (2 * MLSTM_HEADS)
MLSTM_DV = D_MODEL // MLSTM_HEADS
MLSTM_GATE_CAP = 15.0
MLSTM_IN = 2 * MLSTM_HEADS * MLSTM_DQK + 2 * MLSTM_HEADS * MLSTM_DV + 2 * MLSTM_HEADS

HGRN_EXPAND = 128
HGRN_HEADS = D_MODEL // HGRN_EXPAND
HGRN_DK = HGRN_EXPAND
HGRN_DV = D_MODEL // HGRN_HEADS

SSM_EXPAND = 2
SSM_D_INNER = SSM_EXPAND * D_MODEL
SSM_HEADDIM = 64
SSM_HEADS = SSM_D_INNER // SSM_HEADDIM
SSM_GROUPS = 8
SSM_HPG = SSM_HEADS // SSM_GROUPS
SSM_STATE = 128
SSM_CONV = 4
SSM_CONV_DIM = SSM_D_INNER + 2 * SSM_GROUPS * SSM_STATE
SSM_IN = SSM_D_INNER + SSM_CONV_DIM + SSM_HEADS

N_EXPERTS = 32
TOP_K = 4
D_EXPERT = D_MODEL // 4
SWIGLU_ALPHA = 1.702
SWIGLU_LIMIT = 7.0
MOE_BLOCK = 256

DEEPNORM_ALPHA = (2 * DEPTH) ** 0.25
DEEPNORM_BETA = (8 * DEPTH) ** -0.25
LN_EPS = 1e-5
RMS_EPS = 1e-6

N_MLSTM = (DEPTH + 2) // 3
N_HGRN = (DEPTH + 1) // 3
N_SSM = DEPTH // 3

kernel_name = "hybrid_mlstm_hgrn2_ssd_moe_trunk"

F32 = jnp.float32


def layer_norm(x, g, b):
    xf = x.astype(F32)
    mu = xf.mean(-1, keepdims=True)
    var = jnp.square(xf - mu).mean(-1, keepdims=True)
    return ((xf - mu) * lax.rsqrt(var + LN_EPS) * g.astype(F32) + b.astype(F32)).astype(x.dtype)


def rms_norm(x, g):
    xf = x.astype(F32)
    return xf * lax.rsqrt(jnp.mean(xf * xf, -1, keepdims=True) + RMS_EPS) * g.astype(F32)


def to_chunks(a):
    b, s = a.shape[:2]
    a = a.reshape((b, s // CHUNK, CHUNK) + a.shape[2:])
    return jnp.moveaxis(a, 1, 0)


def from_chunks(a):
    a = jnp.moveaxis(a, 0, 1)
    return a.reshape((a.shape[0], a.shape[1] * a.shape[2]) + a.shape[3:])


def causal_tri():
    return jnp.tril(jnp.ones((CHUNK, CHUNK), dtype=bool))


def mlstm_mixer(h, w_in, b_gates, norm_w, w_out):
    bsz, s, _ = h.shape
    H = MLSTM_HEADS
    nqk, nv = H * MLSTM_DQK, H * MLSTM_DV
    proj = h @ w_in
    q, k, v, o, gates = jnp.split(proj, [nqk, 2 * nqk, 2 * nqk + nv, 2 * nqk + 2 * nv], axis=-1)
    gates = gates.astype(F32) + b_gates.astype(F32)
    gates = MLSTM_GATE_CAP * jnp.tanh(gates / MLSTM_GATE_CAP)
    ig = gates[..., :H]
    logf = jax.nn.log_sigmoid(gates[..., H:])
    q = q.reshape(bsz, s, H, MLSTM_DQK).astype(F32) * (MLSTM_DQK ** -0.5)
    k = k.reshape(bsz, s, H, MLSTM_DQK).astype(F32)
    v = v.reshape(bsz, s, H, MLSTM_DV).astype(F32)
    tri = causal_tri()

    def step(carry, inp):
        C, n, m = carry
        qc, kc, vc, ic, fc = inp
        b = jnp.cumsum(fc, axis=1)
        log_d = b[:, :, None, :] - b[:, None, :, :] + ic[:, None, :, :]
        log_d = jnp.where(tri[None, :, :, None], log_d, -jnp.inf)
        m_inter = b + m[:, None, :]
        m_t = jnp.maximum(log_d.max(axis=2), m_inter)
        scores = jnp.einsum('bthd,bshd->btsh', qc, kc) * jnp.exp(log_d - m_t[:, :, None, :])
        inter = jnp.exp(m_inter - m_t)
        num = jnp.einsum('btsh,bshv->bthv', scores, vc) + inter[..., None] * jnp.einsum('bthd,bhvd->bthv', qc, C)
        den = scores.sum(axis=2) + inter * jnp.einsum('bthd,bhd->bth', qc, n)
        hout = num / jnp.maximum(jnp.abs(den), jnp.exp(-m_t))[..., None]
        b_last = b[:, -1, :]
        log_w = b_last[:, None, :] - b + ic
        m_new = jnp.maximum(b_last + m, log_w.max(axis=1))
        w = jnp.exp(log_w - m_new[:, None, :])
        decay = jnp.exp(b_last + m - m_new)
        C = decay[..., None, None] * C + jnp.einsum('bshv,bshd->bhvd', vc * w[..., None], kc)
        n = decay[..., None] * n + jnp.einsum('bsh,bshd->bhd', w, kc)
        return (C, n, m_new), hout

    carry0 = (jnp.zeros((bsz, H, MLSTM_DV, MLSTM_DQK), F32),
              jnp.zeros((bsz, H, MLSTM_DQK), F32),
              jnp.zeros((bsz, H), F32))
    _, hs = lax.scan(step, carry0, (to_chunks(q), to_chunks(k), to_chunks(v), to_chunks(ig), to_chunks(logf)))
    hs = from_chunks(hs)
    hs = rms_norm(hs, norm_w.reshape(H, MLSTM_DV)).reshape(bsz, s, nv)
    hs = hs * jax.nn.sigmoid(o.astype(F32))
    return hs.astype(h.dtype) @ w_out


def hgrn2_mixer(h, w_in, lower_bound, norm_w, w_out):
    bsz, s, _ = h.shape
    H = HGRN_HEADS
    proj = h @ w_in
    q, f, i, g = jnp.split(proj, 4, axis=-1)
    lb = lower_bound.astype(F32).reshape(H, HGRN_DK)
    f = f.astype(F32).reshape(bsz, s, H, HGRN_DK)
    q = jax.nn.silu(q.astype(F32)).reshape(bsz, s, H, HGRN_DK)
    logf = jnp.logaddexp(jnp.log(lb), jnp.log1p(-lb) + jax.nn.log_sigmoid(f))
    k = (1.0 - lb) * jax.nn.sigmoid(-f)
    i = i.astype(F32).reshape(bsz, s, H, HGRN_DV)
    tri = causal_tri()

    def step(S, inp):
        qc, kc, ic, fc = inp
        G = jnp.cumsum(fc, axis=1)
        rel = G[:, :, None] - G[:, None, :]
        rel = jnp.exp(jnp.where(tri[None, :, :, None, None], rel, -jnp.inf))
        attn = jnp.einsum('btshd,bshd->btsh', qc[:, :, None] * rel, kc)
        o = jnp.einsum('btsh,bshv->bthv', attn, ic) + jnp.einsum('bthd,bhdv->bthv', qc * jnp.exp(G), S)
        G_last = G[:, -1]
        S = jnp.exp(G_last)[..., None] * S + jnp.einsum('bshd,bshv->bhdv', kc * jnp.exp(G_last[:, None] - G), ic)
        return S, o

    S0 = jnp.zeros((bsz, H, HGRN_DK, HGRN_DV), F32)
    _, os_ = lax.scan(step, S0, (to_chunks(q), to_chunks(k), to_chunks(i), to_chunks(logf)))
    os_ = rms_norm(from_chunks(os_), norm_w).reshape(bsz, s, H * HGRN_DV)
    os_ = os_ * jax.nn.silu(g.astype(F32))
    return os_.astype(h.dtype) @ w_out


def ssd_mixer(h, w_in, conv_w, conv_b, dt_bias, a_log, d_skip, norm_w, w_out):
    bsz, s, _ = h.shape
    G, J, P, N = SSM_GROUPS, SSM_HPG, SSM_HEADDIM, SSM_STATE
    proj = h @ w_in
    z, xbc, dt = jnp.split(proj, [SSM_D_INNER, SSM_D_INNER + SSM_CONV_DIM], axis=-1)
    conv = lax.conv_general_dilated(xbc, conv_w.astype(xbc.dtype), window_strides=(1,),
                                    padding=[(SSM_CONV - 1, 0)],
                                    dimension_numbers=('NWC', 'WIO', 'NWC'),
                                    feature_group_count=SSM_CONV_DIM)
    xbc = jax.nn.silu(conv.astype(F32) + conv_b.astype(F32))
    xs, Bm, Cm = jnp.split(xbc, [SSM_D_INNER, SSM_D_INNER + G * N], axis=-1)
    xs = xs.reshape(bsz, s, G, J, P)
    Bm = Bm.reshape(bsz, s, G, N)
    Cm = Cm.reshape(bsz, s, G, N)
    dt = jax.nn.softplus(dt.astype(F32) + dt_bias.astype(F32)).reshape(bsz, s, G, J)
    A = -jnp.exp(a_log.astype(F32)).reshape(G, J)
    la = dt * A
    xdt = xs * dt[..., None]
    tri = causal_tri()

    def step(state, inp):
        xc, bc, cc, ac = inp
        cum = jnp.cumsum(ac, axis=1)
        seg = cum[:, :, None] - cum[:, None]
        dec = jnp.exp(jnp.where(tri[None, :, :, None, None], seg, -jnp.inf))
        cb = jnp.einsum('btgn,bsgn->btsg', cc, bc)
        y = jnp.einsum('btsgj,bsgjp->btgjp', cb[..., None] * dec, xc)
        y = y + jnp.einsum('btgn,bgjpn->btgjp', cc, state) * jnp.exp(cum)[..., None]
        last = cum[:, -1]
        w = jnp.exp(last[:, None] - cum)
        state = jnp.exp(last)[..., None, None] * state + jnp.einsum('bsgn,bsgjp->bgjpn', bc, xc * w[..., None])
        return state, y

    st0 = jnp.zeros((bsz, G, J, P, N), F32)
    _, ys = lax.scan(step, st0, (to_chunks(xdt), to_chunks(Bm), to_chunks(Cm), to_chunks(la)))
    y = from_chunks(ys) + d_skip.astype(F32).reshape(G, J)[..., None] * xs
    y = y.reshape(bsz, s, SSM_D_INNER) * jax.nn.silu(z.astype(F32))
    y = rms_norm(y.reshape(bsz, s, G, SSM_D_INNER // G), norm_w.reshape(G, SSM_D_INNER // G))
    y = y.reshape(bsz, s, SSM_D_INNER)
    return y.astype(h.dtype) @ w_out


def swiglu_clamped(hh):
    glu, lin = hh[..., ::2], hh[..., 1::2]
    glu = jnp.minimum(glu, SWIGLU_LIMIT)
    lin = jnp.clip(lin, -SWIGLU_LIMIT, SWIGLU_LIMIT)
    return glu * jax.nn.sigmoid(SWIGLU_ALPHA * glu) * (lin + 1.0)


def moe_ffn(h, w_router, b_router, w1, b1, w2, b2):
    bsz, s, d = h.shape
    t = bsz * s
    xf = h.reshape(t, d)
    logits = (xf @ w_router).astype(F32) + b_router.astype(F32)
    top_v, top_i = lax.top_k(logits, TOP_K)
    gates = jax.nn.softmax(top_v, axis=-1)
    n = t * TOP_K
    e_flat = top_i.reshape(n).astype(jnp.int32)
    order = jnp.argsort(e_flat)
    e_sorted = e_flat[order]
    tok_sorted = (order // TOP_K).astype(jnp.int32)
    gate_sorted = gates.reshape(n)[order]
    counts = jnp.bincount(e_flat, length=N_EXPERTS).astype(jnp.int32)
    padded = (counts + MOE_BLOCK - 1) // MOE_BLOCK * MOE_BLOCK
    starts = jnp.cumsum(counts) - counts
    pad_ends = jnp.cumsum(padded)
    pad_starts = pad_ends - padded
    dest = pad_starts[e_sorted] + jnp.arange(n, dtype=jnp.int32) - starts[e_sorted]
    n_blocks = -(-n // MOE_BLOCK) + N_EXPERTS
    n_pad = n_blocks * MOE_BLOCK
    slot_tok = jnp.full((n_pad,), t, jnp.int32).at[dest].set(tok_sorted)
    slot_gate = jnp.zeros((n_pad,), F32).at[dest].set(gate_sorted)
    block_e = jnp.minimum(jnp.searchsorted(pad_ends, jnp.arange(n_blocks, dtype=jnp.int32) * MOE_BLOCK,
                                           side='right'), N_EXPERTS - 1)
    x_pad = jnp.concatenate([xf, jnp.zeros((1, d), xf.dtype)], axis=0)

    def block_step(acc, inp):
        e, tok, g = inp
        hh = x_pad[tok] @ w1[e] + b1[e]
        yb = swiglu_clamped(hh) @ w2[e] + b2[e]
        return acc.at[tok].add(g[:, None] * yb.astype(F32)), None

    acc0 = jnp.zeros((t + 1, d), F32)
    acc, _ = lax.scan(block_step, acc0, (block_e, slot_tok.reshape(n_blocks, MOE_BLOCK),
                                         slot_gate.reshape(n_blocks, MOE_BLOCK)))
    return acc[:t].reshape(bsz, s, d).astype(h.dtype)


def setup_inputs(seed: int = 0) -> dict:
    key = jax.random.key(seed)
    ks = jax.random.split(key, 32)

    def nrm(i, shape, scale):
        return jax.random.normal(ks[i], shape, F32) * scale

    D, E, F = D_MODEL, N_EXPERTS, D_EXPERT
    gate_base = jnp.concatenate([jnp.full((MLSTM_HEADS,), -3.0, F32), jnp.full((MLSTM_HEADS,), 3.0, F32)])
    dt0 = jnp.exp(jax.random.uniform(ks[19], (N_SSM, SSM_HEADS), F32) * (np.log(0.1) - np.log(0.001)) + np.log(0.001))
    return {
        "x": nrm(0, (BATCH, SEQ, D), 1.0),
        "c": nrm(1, (BATCH, D), 1.0),
        "ada_w": nrm(2, (DEPTH, D, 6 * D), 0.1 * D ** -0.5),
        "ada_b": nrm(3, (DEPTH, 6 * D), 0.01),
        "ln_g": 1.0 + nrm(4, (DEPTH, 2, D), 0.05),
        "ln_b": nrm(5, (DEPTH, 2, D), 0.02),
        "mlstm_w_in": nrm(6, (N_MLSTM, D, MLSTM_IN), D ** -0.5),
        "mlstm_b_gates": gate_base + nrm(7, (N_MLSTM, 2 * MLSTM_HEADS), 0.3),
        "mlstm_norm_w": 1.0 + nrm(8, (N_MLSTM, MLSTM_HEADS * MLSTM_DV), 0.05),
        "mlstm_w_out": nrm(9, (N_MLSTM, MLSTM_HEADS * MLSTM_DV, D), (MLSTM_HEADS * MLSTM_DV) ** -0.5 * DEEPNORM_BETA),
        "hgrn_w_in": nrm(10, (N_HGRN, D, 4 * D), D ** -0.5),
        "hgrn_lb_param": nrm(11, (DEPTH, HGRN_HEADS * HGRN_DK), 0.5),
        "hgrn_norm_w": 1.0 + nrm(12, (N_HGRN, HGRN_DV), 0.05),
        "hgrn_w_out": nrm(13, (N_HGRN, HGRN_HEADS * HGRN_DV, D), (HGRN_HEADS * HGRN_DV) ** -0.5 * DEEPNORM_BETA),
        "ssm_w_in": nrm(14, (N_SSM, D, SSM_IN), D ** -0.5),
        "ssm_conv_w": nrm(15, (N_SSM, SSM_CONV, 1, SSM_CONV_DIM), SSM_CONV ** -0.5),
        "ssm_conv_b": nrm(16, (N_SSM, SSM_CONV_DIM), 0.02),
        "ssm_dt_bias": dt0 + jnp.log(-jnp.expm1(-dt0)),
        "ssm_a_log": jnp.log(jax.random.uniform(ks[17], (N_SSM, SSM_HEADS), F32, 1.0, 16.0)),
        "ssm_d": 1.0 + nrm(18, (N_SSM, SSM_HEADS), 0.1),
        "ssm_norm_w": 1.0 + nrm(20, (N_SSM, SSM_D_INNER), 0.05),
        "ssm_w_out": nrm(21, (N_SSM, SSM_D_INNER, D), SSM_D_INNER ** -0.5 * DEEPNORM_BETA),
        "moe_w_router": nrm(22, (DEPTH, D, E), D ** -0.5),
        "moe_b_router": nrm(23, (DEPTH, E), 0.01),
        "moe_w1": nrm(24, (DEPTH, E, D, 2 * F), D ** -0.5),
        "moe_b1": nrm(25, (DEPTH, E, 2 * F), 0.01),
        "moe_w2": nrm(26, (DEPTH, E, F, D), F ** -0.5 * DEEPNORM_BETA),
        "moe_b2": nrm(27, (DEPTH, E, D), 0.01),
    }


def reference(x, c, ada_w, ada_b, ln_g, ln_b,
              mlstm_w_in, mlstm_b_gates, mlstm_norm_w, mlstm_w_out,
              hgrn_w_in, hgrn_lb_param, hgrn_norm_w, hgrn_w_out,
              ssm_w_in, ssm_conv_w, ssm_conv_b, ssm_dt_bias, ssm_a_log, ssm_d, ssm_norm_w, ssm_w_out,
              moe_w_router, moe_b_router, moe_w1, moe_b1, moe_w2, moe_b2):
    lb_all = jnp.cumsum(jax.nn.softmax(hgrn_lb_param.astype(F32), axis=0), axis=0)
    lb_all = lb_all - lb_all[0:1]
    cond = jax.nn.silu(c)
    for i in range(DEPTH):
        mod = (cond @ ada_w[i] + ada_b[i])[:, None, :]
        sh1, sc1, g1, sh2, sc2, g2 = jnp.split(mod, 6, axis=-1)
        h = x * (1.0 + sc1) + sh1
        kind, j = i % N_MIXERS, i // N_MIXERS
        if kind == 0:
            y = mlstm_mixer(h, mlstm_w_in[j], mlstm_b_gates[j], mlstm_norm_w[j], mlstm_w_out[j])
        elif kind == 1:
            y = hgrn2_mixer(h, hgrn_w_in[j], lb_all[i], hgrn_norm_w[j], hgrn_w_out[j])
        else:
            y = ssd_mixer(h, ssm_w_in[j], ssm_conv_w[j], ssm_conv_b[j], ssm_dt_bias[j], ssm_a_log[j],
                          ssm_d[j], ssm_norm_w[j], ssm_w_out[j])
        x = layer_norm(DEEPNORM_ALPHA * x + (1.0 + g1) * y, ln_g[i, 0], ln_b[i, 0])
        h = x * (1.0 + sc2) + sh2
        y = moe_ffn(h, moe_w_router[i], moe_b_router[i], moe_w1[i], moe_b1[i], moe_w2[i], moe_b2[i])
        x = layer_norm(DEEPNORM_ALPHA * x + (1.0 + g2) * y, ln_g[i, 1], ln_b[i, 1])
    return x
```

```python
import functools

import jax
import jax.numpy as jnp
from jax import lax
from jax.experimental import pallas as pl
from jax.experimental.pallas import tpu as pltpu

F32 = jnp.float32
BF16 = jnp.bfloat16
I32 = jnp.int32

MLSTM_GATE_CAP = 15.0
HGRN_HEAD_DIM = 128
SSM_GROUPS = 8
SSM_HEADDIM = 64
TOP_K = 4
SWIGLU_ALPHA = 1.702
SWIGLU_LIMIT = 7.0
LN_EPS = 1e-5
RMS_EPS = 1e-6

LANES = 128
SUBLANES = 8
VMEM_LIMIT_BYTES = 56 * 1024 * 1024

MLSTM_CHUNK = 128
HGRN_CHUNK = 64
HGRN_SUB = 16
HGRN_ROWS_PER_STEP = 512
SSD_CHUNK = 128
MOE_BLOCK = 512
ROWCOPY_ROWS = 2048

_HI = lax.Precision.HIGHEST


def _cparams(sem):
    return pltpu.CompilerParams(dimension_semantics=sem, vmem_limit_bytes=VMEM_LIMIT_BYTES)


def _dot(a, b):
    return jnp.dot(a, b, preferred_element_type=F32)


def _dot_nt(a, b):
    return lax.dot_general(a, b, (((1,), (1,)), ((), ())), preferred_element_type=F32)


def _dot_tn(a, b):
    return lax.dot_general(a, b, (((0,), (0,)), ((), ())), preferred_element_type=F32)


def _dot_hi(a, b):
    return jnp.dot(a, b, preferred_element_type=F32, precision=_HI)


def _sigmoid(x):
    return 1.0 / (1.0 + jnp.exp(-x))


def _silu(x):
    return x * _sigmoid(x)


def _log_sigmoid(x):
    return jnp.minimum(x, 0.0) - jnp.log1p(jnp.exp(-jnp.abs(x)))


def _softplus(x):
    return jnp.maximum(x, 0.0) + jnp.log1p(jnp.exp(-jnp.abs(x)))


def _lower_tri(n):
    r = lax.broadcasted_iota(I32, (n, n), 0)
    c = lax.broadcasted_iota(I32, (n, n), 1)
    return r >= c


def _ada_kernel(c_ref, w_ref, b_ref, o_ref):
    cond = _silu(c_ref[...])
    o_ref[0] = _dot(cond.astype(BF16), w_ref[0].astype(BF16)) + b_ref[0]


def _ada_mods(c, ada_w, ada_b):
    depth, d, n6 = ada_w.shape
    bsz = c.shape[0]
    rows = -(-bsz // SUBLANES) * SUBLANES
    c_pad = jnp.zeros((rows, d), F32).at[:bsz].set(c)
    tn = 1536 if n6 % 1536 == 0 else n6
    out = pl.pallas_call(
        _ada_kernel,
        out_shape=jax.ShapeDtypeStruct((depth, rows, n6), F32),
        grid=(depth, n6 // tn),
        in_specs=[
            pl.BlockSpec((rows, d), lambda l, j: (0, 0)),
            pl.BlockSpec((1, d, tn), lambda l, j: (l, 0, j)),
            pl.BlockSpec((1, 1, tn), lambda l, j: (l, 0, j)),
        ],
        out_specs=pl.BlockSpec((1, rows, tn), lambda l, j: (l, 0, j)),
        compiler_params=_cparams(("parallel", "parallel")),
        name="ada_mods",
    )(c_pad, ada_w, ada_b.reshape(depth, 1, n6))
    return out[:, :bsz].reshape(depth, bsz, 6, d)


def _linear_mod_kernel(x_ref, mod_ref, w_ref, o_ref, h_ref, *, shift_row, scale_row):
    @pl.when(pl.program_id(1) == 0)
    def _():
        m = mod_ref[0]
        h = x_ref[...] * (1.0 + m[scale_row:scale_row + 1, :]) + m[shift_row:shift_row + 1, :]
        h_ref[...] = h.astype(BF16)

    o_ref[...] = _dot(h_ref[...], w_ref[...]).astype(o_ref.dtype)


def _linear_mod(x, mod, w, *, seq, shift_row, scale_row, out_dtype, tm, tn, name):
    t, d = x.shape
    n = w.shape[1]
    tm = min(tm, seq)
    tn = min(tn, n)
    assert t % tm == 0 and seq % tm == 0 and n % tn == 0
    return pl.pallas_call(
        functools.partial(_linear_mod_kernel, shift_row=shift_row, scale_row=scale_row),
        out_shape=jax.ShapeDtypeStruct((t, n), out_dtype),
        grid=(t // tm, n // tn),
        in_specs=[
            pl.BlockSpec((tm, d), lambda i, j: (i, 0)),
            pl.BlockSpec((1, 6, d), lambda i, j: ((i * tm) // seq, 0, 0)),
            pl.BlockSpec((d, tn), lambda i, j: (0, j)),
        ],
        out_specs=pl.BlockSpec((tm, tn), lambda i, j: (i, j)),
        scratch_shapes=[pltpu.VMEM((tm, d), BF16)],
        compiler_params=_cparams(("parallel", "arbitrary")),
        name=name,
    )(x, mod, w)


def _deepnorm_ln(x, y, gate, g, b, alpha):
    v = alpha * x + (1.0 + gate) * y
    mu = jnp.mean(v, axis=-1, keepdims=True)
    dv = v - mu
    var = jnp.mean(dv * dv, axis=-1, keepdims=True)
    return dv * lax.rsqrt(var + LN_EPS) * g + b


def _outproj_ln_kernel(y_ref, w_ref, x_ref, mod_ref, g_ref, b_ref, o_ref, acc_ref, *, gate_row, nk, alpha):
    k = pl.program_id(1)
    part = _dot(y_ref[...], w_ref[...])

    def finish(acc):
        gate = mod_ref[0][gate_row:gate_row + 1, :]
        o_ref[...] = _deepnorm_ln(x_ref[...], acc, gate, g_ref[...], b_ref[...], alpha)

    if nk == 1:
        finish(part)
    else:
        @pl.when(k == 0)
        def _():
            acc_ref[...] = part

        @pl.when(jnp.logical_and(k > 0, k < nk - 1))
        def _():
            acc_ref[...] += part

        @pl.when(k == nk - 1)
        def _():
            finish(acc_ref[...] + part)


def _outproj_ln(y, w, x, mod, ln_g, ln_b, *, seq, gate_row, alpha, tm, tk, name):
    t, kdim = y.shape
    d = w.shape[1]
    tm = min(tm, seq)
    tk = min(tk, kdim)
    assert t % tm == 0 and seq % tm == 0 and kdim % tk == 0
    nk = kdim // tk
    return pl.pallas_call(
        functools.partial(_outproj_ln_kernel, gate_row=gate_row, nk=nk, alpha=alpha),
        out_shape=jax.ShapeDtypeStruct((t, d), F32),
        grid=(t // tm, nk),
        in_specs=[
            pl.BlockSpec((tm, tk), lambda i, k: (i, k)),
            pl.BlockSpec((tk, d), lambda i, k: (k, 0)),
            pl.BlockSpec((tm, d), lambda i, k: (i, 0)),
            pl.BlockSpec((1, 6, d), lambda i, k: ((i * tm) // seq, 0, 0)),
            pl.BlockSpec((1, d), lambda i, k: (0, 0)),
            pl.BlockSpec((1, d), lambda i, k: (0, 0)),
        ],
        out_specs=pl.BlockSpec((tm, d), lambda i, k: (i, 0)),
        scratch_shapes=[pltpu.VMEM((tm, d), F32)],
        compiler_params=_cparams(("parallel", "arbitrary")),
        name=name,
    )(y, w, x, mod, ln_g.reshape(1, d), ln_b.reshape(1, d))


def _mlstm_kernel(q_ref, k_ref, v_ref, o_ref, aux_ref, bias_ref, nw_ref, y_ref,
                  c_ref, n_ref, m_ref, *, heads, dqk, dv, chunk):
    @pl.when(pl.program_id(1) == 0)
    def _():
        c_ref[...] = jnp.zeros_like(c_ref)
        n_ref[...] = jnp.zeros_like(n_ref)
        m_ref[...] = jnp.zeros_like(m_ref)

    scale = dqk ** -0.5
    g = MLSTM_GATE_CAP * jnp.tanh((aux_ref[...] + bias_ref[...]) / MLSTM_GATE_CAP)
    lane = lax.broadcasted_iota(I32, g.shape, 1)
    logsig = _log_sigmoid(g)
    is_f = jnp.logical_and(lane >= heads, lane < 2 * heads)
    tri = _lower_tri(chunk)
    bcum = _dot_hi(tri.astype(F32), jnp.where(is_f, logsig, 0.0))
    g_t = g.T
    bcum_t = bcum.T

    for h in range(heads):
        ig_col = g[:, h:h + 1]
        ig_row = g_t[h:h + 1, :]
        b_col = bcum[:, heads + h:heads + h + 1]
        b_row = bcum_t[heads + h:heads + h + 1, :]
        m_prev = m_ref[h][0:1, 0:1]

        log_d = jnp.where(tri, b_col - b_row + ig_row, -jnp.inf)
        m_inter = b_col + m_prev
        m_t = jnp.maximum(jnp.max(log_d, axis=1, keepdims=True), m_inter)
        dmat = jnp.exp(log_d - m_t)

        q = q_ref[:, h * dqk:(h + 1) * dqk]
        k = k_ref[:, h * dqk:(h + 1) * dqk]
        v = v_ref[:, h * dv:(h + 1) * dv]
        scores = _dot_nt(q, k) * scale * dmat
        inter = jnp.exp(m_inter - m_t)
        c_st = c_ref[h]
        n_st = n_ref[h][0:1, :]
        num = _dot(scores.astype(BF16), v) + inter * (_dot_nt(q, c_st.astype(BF16)) * scale)
        qn = jnp.sum(q.astype(F32) * n_st, axis=1, keepdims=True) * scale
        den = jnp.sum(scores, axis=1, keepdims=True) + inter * qn
        hout = num * (1.0 / jnp.maximum(jnp.abs(den), jnp.exp(-m_t)))

        b_last = b_col[chunk - 1:chunk, :]
        log_w = b_last - b_col + ig_col
        m_new = jnp.maximum(b_last + m_prev, jnp.max(log_w, axis=0, keepdims=True))
        w = jnp.exp(log_w - m_new)
        decay = jnp.exp(b_last + m_prev - m_new)
        vw = (v.astype(F32) * w).astype(BF16)
        c_ref[h] = decay * c_st + _dot_tn(vw, k)
        n_new = decay * n_st + jnp.sum(k.astype(F32) * w, axis=0, keepdims=True)
        n_ref[h] = jnp.broadcast_to(n_new, n_ref.shape[1:])
        m_ref[h] = jnp.broadcast_to(m_new, m_ref.shape[1:])

        ms = jnp.mean(hout * hout, axis=1, keepdims=True)
        hn = hout * lax.rsqrt(ms + RMS_EPS) * nw_ref[:, h * dv:(h + 1) * dv]
        og = o_ref[:, h * dv:(h + 1) * dv].astype(F32)
        y_ref[:, h * dv:(h + 1) * dv] = (hn * _sigmoid(og)).astype(y_ref.dtype)


def _mlstm_recurrence(main, aux, bias, norm_w, *, bsz, seq, heads, dqk, dv):
    t = main.shape[0]
    nqk, nv = heads * dqk, heads * dv
    chunk = min(MLSTM_CHUNK, seq)
    nc = seq // chunk
    assert nv == 2 * nqk and seq % chunk == 0
    row = lambda b, c: b * nc + c
    return pl.pallas_call(
        functools.partial(_mlstm_kernel, heads=heads, dqk=dqk, dv=dv, chunk=chunk),
        out_shape=jax.ShapeDtypeStruct((t, nv), BF16),
        grid=(bsz, nc),
        in_specs=[
            pl.BlockSpec((chunk, nqk), lambda b, c: (row(b, c), 0)),
            pl.BlockSpec((chunk, nqk), lambda b, c: (row(b, c), 1)),
            pl.BlockSpec((chunk, nv), lambda b, c: (row(b, c), 1)),
            pl.BlockSpec((chunk, nv), lambda b, c: (row(b, c), 2)),
            pl.BlockSpec((chunk, LANES), lambda b, c: (row(b, c), 0)),
            pl.BlockSpec((1, LANES), lambda b, c: (0, 0)),
            pl.BlockSpec((1, nv), lambda b, c: (0, 0)),
        ],
        out_specs=pl.BlockSpec((chunk, nv), lambda b, c: (row(b, c), 0)),
        scratch_shapes=[
            pltpu.VMEM((heads, dv, dqk), F32),
            pltpu.VMEM((heads, SUBLANES, dqk), F32),
            pltpu.VMEM((heads, SUBLANES, LANES), F32),
        ],
        compiler_params=_cparams(("parallel", "arbitrary")),
        name="mlstm_recurrence",
    )(main, main, main, main, aux, bias, norm_w.reshape(1, nv))


def _mlstm_mixer(x, mod, w_in, b_gates, norm_w, w_out, ln_g, ln_b, *, bsz, seq, alpha):
    d = x.shape[1]
    heads = b_gates.shape[0] // 2
    nv = w_out.shape[0]
    nqk = (w_in.shape[1] - 2 * nv - 2 * heads) // 2
    n_main = 2 * nqk + 2 * nv
    w_main = w_in[:, :n_main].astype(BF16)
    w_aux = jnp.zeros((d, LANES), BF16).at[:, :2 * heads].set(w_in[:, n_main:].astype(BF16))
    bias = jnp.zeros((1, LANES), F32).at[0, :2 * heads].set(b_gates)
    main = _linear_mod(x, mod, w_main, seq=seq, shift_row=0, scale_row=1, out_dtype=BF16,
                       tm=1024, tn=512, name="mlstm_in_main")
    aux = _linear_mod(x, mod, w_aux, seq=seq, shift_row=0, scale_row=1, out_dtype=F32,
                      tm=1024, tn=LANES, name="mlstm_in_gates")
    y = _mlstm_recurrence(main, aux, bias, norm_w, bsz=bsz, seq=seq, heads=heads,
                          dqk=nqk // heads, dv=nv // heads)
    return _outproj_ln(y, w_out.astype(BF16), x, mod, ln_g, ln_b, seq=seq, gate_row=2, alpha=alpha,
                       tm=512, tk=2048, name="mlstm_out_ln")


def _hgrn_kernel(main_ref, lbp_ref, nw_ref, y_ref, st_ref, *, layer, chunk, sub, nchunks):
    dk = HGRN_HEAD_DIM

    @pl.when(pl.program_id(2) == 0)
    def _():
        st_ref[...] = jnp.zeros_like(st_ref)

    p = lbp_ref[...]
    e = jnp.exp(p - jnp.max(p, axis=0, keepdims=True))
    sm = e / jnp.sum(e, axis=0, keepdims=True)
    lb = jnp.zeros((1, dk), F32)
    for r in range(1, layer + 1):
        lb = lb + sm[r:r + 1, :]
    log_lb = jnp.log(lb)
    log_1m = jnp.log1p(-lb)
    one_m = 1.0 - lb
    tri = _lower_tri(chunk).astype(F32)
    nsub = chunk // sub
    sub_row = lax.broadcasted_iota(I32, (sub, dk), 0)
    nw = nw_ref[...]

    def body(ci, carry):
        r0 = pl.multiple_of(ci * chunk, chunk)
        blk = main_ref[pl.ds(r0, chunk), :]
        qraw = blk[:, 0:dk].astype(F32)
        fraw = blk[:, dk:2 * dk].astype(F32)
        iv_b = blk[:, 2 * dk:3 * dk]
        iv = iv_b.astype(F32)
        graw = blk[:, 3 * dk:4 * dk].astype(F32)

        q = _silu(qraw)
        b2 = log_1m + _log_sigmoid(fraw)
        logf = jnp.maximum(log_lb, b2) + jnp.log1p(jnp.exp(-jnp.abs(log_lb - b2)))
        k = one_m * _sigmoid(-fraw)
        gcum = _dot_hi(tri, logf)

        st = st_ref[...]
        o_inter = _dot_nt((q * jnp.exp(gcum)).astype(BF16), st.astype(BF16))

        outs = []
        for bi in range(nsub):
            lo = bi * sub
            g_i = gcum[lo:lo + sub]
            q_i = q[lo:lo + sub]
            k_i = k[lo:lo + sub]
            i_i = iv[lo:lo + sub]
            acc = o_inter[lo:lo + sub]
            if bi > 0:
                ref_row = gcum[lo - 1:lo]
                qt = (q_i * jnp.exp(g_i - ref_row)).astype(BF16)
                kt = (k[0:lo] * jnp.exp(ref_row - gcum[0:lo])).astype(BF16)
                a = _dot_nt(qt, kt)
                acc = acc + _dot(a.astype(BF16), iv_b[0:lo])
            for s in range(sub):
                arg = jnp.where(sub_row >= s, g_i - g_i[s:s + 1], -jnp.inf)
                pw = q_i * jnp.exp(arg) * k_i[s:s + 1]
                acc = acc + jnp.sum(pw, axis=1, keepdims=True) * i_i[s:s + 1]
            outs.append(acc)
        o = jnp.concatenate(outs, axis=0)

        g_last = gcum[chunk - 1:chunk]
        kdec = (k * jnp.exp(g_last - gcum)).astype(BF16)
        st_ref[...] = st * jnp.exp(g_last) + _dot_tn(iv_b, kdec)

        ms = jnp.mean(o * o, axis=1, keepdims=True)
        y = o * lax.rsqrt(ms + RMS_EPS) * nw * _silu(graw)
        y_ref[pl.ds(r0, chunk), :] = y.astype(y_ref.dtype)
        return carry

    lax.fori_loop(0, nchunks, body, 0)


def _hgrn_recurrence(main, lb_param, norm_w, *, bsz, seq, heads, layer):
    t = main.shape[0]
    dk = HGRN_HEAD_DIM
    depth = lb_param.shape[0]
    rows = min(HGRN_ROWS_PER_STEP, seq)
    chunk = min(HGRN_CHUNK, rows)
    assert seq % rows == 0 and rows % chunk == 0 and chunk % HGRN_SUB == 0
    nr = seq // rows
    return pl.pallas_call(
        functools.partial(_hgrn_kernel, layer=layer, chunk=chunk, sub=HGRN_SUB, nchunks=rows // chunk),
        out_shape=jax.ShapeDtypeStruct((t, heads * dk), BF16),
        grid=(bsz, heads, nr),
        in_specs=[
            pl.BlockSpec((rows, 4 * dk), lambda b, h, c: (b * nr + c, h)),
            pl.BlockSpec((depth, dk), lambda b, h, c: (0, h)),
            pl.BlockSpec((1, dk), lambda b, h, c: (0, 0)),
        ],
        out_specs=pl.BlockSpec((rows, dk), lambda b, h, c: (b * nr + c, h)),
        scratch_shapes=[pltpu.VMEM((dk, dk), F32)],
        compiler_params=_cparams(("parallel", "parallel", "arbitrary")),
        name="hgrn_recurrence",
    )(main, lb_param, norm_w.reshape(1, dk))


def _hgrn_mixer(x, mod, w_in, lb_param, norm_w, w_out, ln_g, ln_b, *, bsz, seq, alpha, layer):
    d = x.shape[1]
    dk = HGRN_HEAD_DIM
    heads = d // dk
    w_perm = w_in.reshape(d, 4, heads, dk).transpose(0, 2, 1, 3).reshape(d, 4 * d).astype(BF16)
    main = _linear_mod(x, mod, w_perm, seq=seq, shift_row=0, scale_row=1, out_dtype=BF16,
                       tm=1024, tn=512, name="hgrn_in")
    y = _hgrn_recurrence(main, lb_param, norm_w, bsz=bsz, seq=seq, heads=heads, layer=layer)
    return _outproj_ln(y, w_out.astype(BF16), x, mod, ln_g, ln_b, seq=seq, gate_row=2, alpha=alpha,
                       tm=512, tk=2048, name="hgrn_out_ln")


def _conv_kernel(x_ref, halo_ref, w_ref, b_ref, o_ref, *, taps, steps_per_seq):
    first = (pl.program_id(0) % steps_per_seq) == 0
    x = x_ref[...].astype(F32)
    halo = jnp.where(first, 0.0, halo_ref[...].astype(F32))
    xc = jnp.concatenate([halo, x], axis=0)
    hr, tm = halo.shape[0], x.shape[0]
    acc = jnp.broadcast_to(b_ref[...], x.shape)
    for kk in range(taps):
        shift = taps - 1 - kk
        xs = xc if shift == 0 else pltpu.roll(xc, shift, 0)
        acc = acc + w_ref[kk:kk + 1, :] * xs[hr:hr + tm]
    o_ref[...] = _silu(acc).astype(o_ref.dtype)


def _ssd_conv(main, conv_w, conv_b, *, seq, col0):
    t = main.shape[0]
    taps, cdim = conv_w.shape
    tm = min(512, seq)
    tc = 512
    halo = 2 * SUBLANES
    assert seq % tm == 0 and cdim % tc == 0 and col0 % tc == 0 and taps - 1 <= halo and tm % halo == 0
    cb0 = col0 // tc
    hb = tm // halo
    return pl.pallas_call(
        functools.partial(_conv_kernel, taps=taps, steps_per_seq=seq // tm),
        out_shape=jax.ShapeDtypeStruct((t, cdim), BF16),
        grid=(t // tm, cdim // tc),
        in_specs=[
            pl.BlockSpec((tm, tc), lambda i, j: (i, cb0 + j)),
            pl.BlockSpec((halo, tc), lambda i, j: (jnp.maximum(i * hb - 1, 0), cb0 + j)),
            pl.BlockSpec((taps, tc), lambda i, j: (0, j)),
            pl.BlockSpec((1, tc), lambda i, j: (0, j)),
        ],
        out_specs=pl.BlockSpec((tm, tc), lambda i, j: (i, j)),
        compiler_params=_cparams(("parallel", "parallel")),
        name="ssd_conv",
    )(main, main, conv_w, conv_b.reshape(1, cdim))


def _ssd_kernel(xs_ref, b_ref, c_ref, z_ref, aux_ref, dtb_ref, alog_ref, dsk_ref, nw_ref, y_ref,
                state_ref, dt_ref, cum_ref, dt_t_ref, cum_t_ref, *, chunk, hpg, hd):
    cidx = pl.program_id(1)
    g = pl.program_id(2)

    @pl.when(jnp.logical_and(cidx == 0, g == 0))
    def _():
        state_ref[...] = jnp.zeros_like(state_ref)

    tri = _lower_tri(chunk)

    @pl.when(g == 0)
    def _():
        dt = _softplus(aux_ref[...] + dtb_ref[...])
        la = dt * (-jnp.exp(alog_ref[...]))
        cum = _dot_hi(tri.astype(F32), la)
        dt_ref[...] = dt
        cum_ref[...] = cum
        dt_t_ref[...] = dt.T
        cum_t_ref[...] = cum.T

    dt_nat = dt_ref[...]
    cum_nat = cum_ref[...]
    g8 = pl.multiple_of(g * hpg, hpg)
    cum_t = cum_t_ref[pl.ds(g8, hpg), :]
    dt_t = dt_t_ref[pl.ds(g8, hpg), :]
    lane = lax.broadcasted_iota(I32, cum_nat.shape, 1)

    xs = xs_ref[...].astype(F32)
    x_t = xs.T
    bm = b_ref[...]
    cm = c_ref[...]
    cb_t = _dot_nt(bm, cm)
    st = state_ref[g]
    inter_t = _dot_nt(st.astype(BF16), cm)
    last = cum_t[:, chunk - 1:chunk]
    w_t = jnp.exp(last - cum_t) * dt_t
    ecum_t = jnp.exp(cum_t)
    s_le_t = jnp.logical_not(tri) | (lax.broadcasted_iota(I32, (chunk, chunk), 0)
                                     == lax.broadcasted_iota(I32, (chunk, chunk), 1))

    y_parts = []
    xw_parts = []
    st_parts = []
    for j in range(hpg):
        col = g8 + j
        sel = lane == col
        cum_col = jnp.sum(jnp.where(sel, cum_nat, 0.0), axis=1, keepdims=True)
        dt_col = jnp.sum(jnp.where(sel, dt_nat, 0.0), axis=1, keepdims=True)
        cum_row = cum_t[j:j + 1, :]
        dec_t = jnp.exp(jnp.where(s_le_t, cum_row - cum_col, -jnp.inf))
        m_t = (cb_t * dec_t * dt_col).astype(BF16)
        x_j = x_t[j * hd:(j + 1) * hd, :]
        y_j = _dot(x_j.astype(BF16), m_t) + inter_t[j * hd:(j + 1) * hd, :] * ecum_t[j:j + 1, :]
        y_parts.append(y_j)
        xw_parts.append((x_j * w_t[j:j + 1, :]).astype(BF16))
        st_parts.append(st[j * hd:(j + 1) * hd, :] * jnp.exp(last[j:j + 1, :]))
    y_t = jnp.concatenate(y_parts, axis=0)
    xw_t = jnp.concatenate(xw_parts, axis=0)
    state_ref[g] = jnp.concatenate(st_parts, axis=0) + _dot(xw_t, bm)

    y = y_t.T + dsk_ref[...] * xs
    y = y * _silu(z_ref[...].astype(F32))
    ms = jnp.mean(y * y, axis=1, keepdims=True)
    y_ref[...] = (y * lax.rsqrt(ms + RMS_EPS) * nw_ref[...]).astype(y_ref.dtype)


def _ssd_recurrence(main, conv, aux, dt_bias, a_log, d_skip, norm_w, *, bsz, seq, d_inner, nstate):
    t = main.shape[0]
    groups = SSM_GROUPS
    gw = d_inner // groups
    hpg = gw // SSM_HEADDIM
    nheads = groups * hpg
    chunk = min(SSD_CHUNK, seq)
    assert seq % chunk == 0 and nstate == LANES and chunk == LANES and nheads <= LANES
    nc = seq // chunk
    row = lambda b, c, g: b * nc + c
    pad = lambda v: jnp.zeros((1, LANES), F32).at[0, :nheads].set(v)
    dsk = jnp.repeat(d_skip, SSM_HEADDIM).reshape(1, d_inner)
    b0 = d_inner // nstate
    return pl.pallas_call(
        functools.partial(_ssd_kernel, chunk=chunk, hpg=hpg, hd=SSM_HEADDIM),
        out_shape=jax.ShapeDtypeStruct((t, d_inner), BF16),
        grid=(bsz, nc, groups),
        in_specs=[
            pl.BlockSpec((chunk, gw), lambda b, c, g: (row(b, c, g), g)),
            pl.BlockSpec((chunk, nstate), lambda b, c, g: (row(b, c, g), b0 + g)),
            pl.BlockSpec((chunk, nstate), lambda b, c, g: (row(b, c, g), b0 + groups + g)),
            pl.BlockSpec((chunk, gw), lambda b, c, g: (row(b, c, g), g)),
            pl.BlockSpec((chunk, LANES), lambda b, c, g: (row(b, c, g), 0)),
            pl.BlockSpec((1, LANES), lambda b, c, g: (0, 0)),
            pl.BlockSpec((1, LANES), lambda b, c, g: (0, 0)),
            pl.BlockSpec((1, gw), lambda b, c, g: (0, g)),
            pl.BlockSpec((1, gw), lambda b, c, g: (0, g)),
        ],
        out_specs=pl.BlockSpec((chunk, gw), lambda b, c, g: (row(b, c, g), g)),
        scratch_shapes=[
            pltpu.VMEM((groups, gw, nstate), F32),
            pltpu.VMEM((chunk, LANES), F32),
            pltpu.VMEM((chunk, LANES), F32),
            pltpu.VMEM((LANES, chunk), F32),
            pltpu.VMEM((LANES, chunk), F32),
        ],
        compiler_params=_cparams(("parallel", "arbitrary", "arbitrary")),
        name="ssd_recurrence",
    )(conv, conv, conv, main, aux, pad(dt_bias), pad(a_log), dsk, norm_w.reshape(1, d_inner))


def _ssd_mixer(x, mod, w_in, conv_w, conv_b, dt_bias, a_log, d_skip, norm_w, w_out, ln_g, ln_b,
               *, bsz, seq, alpha):
    d = x.shape[1]
    d_inner = w_out.shape[0]
    cdim = conv_w.shape[-1]
    nheads = dt_bias.shape[0]
    nstate = (cdim - d_inner) // (2 * SSM_GROUPS)
    n_main = d_inner + cdim
    w_main = w_in[:, :n_main].astype(BF16)
    w_aux = jnp.zeros((d, LANES), BF16).at[:, :nheads].set(w_in[:, n_main:].astype(BF16))
    main = _linear_mod(x, mod, w_main, seq=seq, shift_row=0, scale_row=1, out_dtype=BF16,
                       tm=1024, tn=512, name="ssd_in_main")
    aux = _linear_mod(x, mod, w_aux, seq=seq, shift_row=0, scale_row=1, out_dtype=F32,
                      tm=1024, tn=LANES, name="ssd_in_dt")
    conv = _ssd_conv(main, conv_w.reshape(conv_w.shape[0], cdim), conv_b, seq=seq, col0=d_inner)
    y = _ssd_recurrence(main, conv, aux, dt_bias, a_log, d_skip, norm_w, bsz=bsz, seq=seq,
                        d_inner=d_inner, nstate=nstate)
    return _outproj_ln(y, w_out.astype(BF16), x, mod, ln_g, ln_b, seq=seq, gate_row=2, alpha=alpha,
                       tm=512, tk=2048, name="ssd_out_ln")


def _router_kernel(x_ref, mod_ref, wr_ref, br_ref, h_ref, ti_ref, tg_ref, *, n_real):
    i = pl.program_id(0)

    @pl.when(i < n_real)
    def _():
        m = mod_ref[0]
        h = x_ref[...] * (1.0 + m[4:5, :]) + m[3:4, :]
        h_ref[...] = h
        logits = _dot_hi(h, wr_ref[...]) + br_ref[...]
        lane = lax.broadcasted_iota(I32, logits.shape, 1)
        lane_f = lane.astype(F32)
        vals = []
        ti = jnp.zeros(logits.shape, I32)
        for kk in range(TOP_K):
            mx = jnp.max(logits, axis=1, keepdims=True)
            idx = jnp.min(jnp.where(logits == mx, lane_f, float(LANES)), axis=1, keepdims=True).astype(I32)
            vals.append(mx)
            ti = jnp.where(lane == kk, idx, ti)
            logits = jnp.where(lane == idx, -jnp.inf, logits)
        es = [jnp.exp(v - vals[0]) for v in vals]
        tot = es[0]
        for e in es[1:]:
            tot = tot + e
        tg = jnp.zeros(ti.shape, F32)
        for kk in range(TOP_K):
            tg = jnp.where(lane == kk, es[kk] / tot, tg)
        ti_ref[...] = ti
        tg_ref[...] = tg

    @pl.when(i >= n_real)
    def _():
        h_ref[...] = jnp.zeros_like(h_ref)
        ti_ref[...] = jnp.zeros_like(ti_ref)
        tg_ref[...] = jnp.zeros_like(tg_ref)


def _router(x, mod, w_router, b_router, *, seq, tm):
    t, d = x.shape
    ne = w_router.shape[1]
    tm = min(tm, seq)
    n_real = t // tm
    wr = jnp.zeros((d, LANES), F32).at[:, :ne].set(w_router)
    br = jnp.full((1, LANES), -1e30, F32).at[0, :ne].set(b_router)
    clamp = lambda i: jnp.minimum(i, n_real - 1)
    return pl.pallas_call(
        functools.partial(_router_kernel, n_real=n_real),
        out_shape=(
            jax.ShapeDtypeStruct((t + tm, d), F32),
            jax.ShapeDtypeStruct((t + tm, LANES), I32),
            jax.ShapeDtypeStruct((t + tm, LANES), F32),
        ),
        grid=(n_real + 1,),
        in_specs=[
            pl.BlockSpec((tm, d), lambda i: (clamp(i), 0)),
            pl.BlockSpec((1, 6, d), lambda i: ((clamp(i) * tm) // seq, 0, 0)),
            pl.BlockSpec((d, LANES), lambda i: (0, 0)),
            pl.BlockSpec((1, LANES), lambda i: (0, 0)),
        ],
        out_specs=(
            pl.BlockSpec((tm, d), lambda i: (i, 0)),
            pl.BlockSpec((tm, LANES), lambda i: (i, 0)),
            pl.BlockSpec((tm, LANES), lambda i: (i, 0)),
        ),
        compiler_params=_cparams(("parallel",)),
        name="moe_router",
    )(x, mod, wr, br)


def _rowcopy_kernel(sidx_ref, didx_ref, src_ref, dst_ref, cnt_ref, sem, *, rows, nsteps):
    i = pl.program_id(0)
    slot = i % 2

    def issue(r, cnt):
        s = sidx_ref[0, r]
        d = didx_ref[0, r]

        @pl.when(d >= 0)
        def _():
            pltpu.make_async_copy(src_ref.at[s], dst_ref.at[d], sem.at[slot]).start()

        return cnt + (d >= 0).astype(I32)

    cnt = lax.fori_loop(0, rows, issue, jnp.int32(0))
    cnt_ref[slot] = cnt

    def drain(sl, count):
        def wait_one(r, c):
            pltpu.make_async_copy(src_ref.at[0], dst_ref.at[0], sem.at[sl]).wait()
            return c

        lax.fori_loop(0, count, wait_one, 0)

    @pl.when(i > 0)
    def _():
        drain(1 - slot, cnt_ref[1 - slot])

    @pl.when(i == nsteps - 1)
    def _():
        drain(slot, cnt)


def _rowcopy(src3, sidx, didx, n_dst):
    m = sidx.shape[0]
    rows = min(ROWCOPY_ROWS, m)
    assert m % rows == 0
    nsteps = m // rows
    idx_spec = pl.BlockSpec((None, 1, rows), lambda i: (i, 0, 0), memory_space=pltpu.SMEM)
    return pl.pallas_call(
        functools.partial(_rowcopy_kernel, rows=rows, nsteps=nsteps),
        out_shape=jax.ShapeDtypeStruct((n_dst,) + src3.shape[1:], src3.dtype),
        grid=(nsteps,),
        in_specs=[idx_spec, idx_spec, pl.BlockSpec(memory_space=pl.ANY)],
        out_specs=pl.BlockSpec(memory_space=pl.ANY),
        scratch_shapes=[pltpu.SMEM((2,), I32), pltpu.SemaphoreType.DMA((2,))],
        compiler_params=pltpu.CompilerParams(dimension_semantics=("arbitrary",)),
        name="moe_rowcopy",
    )(sidx.reshape(nsteps, 1, rows), didx.reshape(nsteps, 1, rows), src3)


def _expert_kernel(be_ref, bv_ref, x_ref, w1g_ref, w1l_ref, b1g_ref, b1l_ref, w2_ref, b2_ref, y_ref):
    i = pl.program_id(0)

    @pl.when(bv_ref[i] == 1)
    def _():
        x = x_ref[...].astype(BF16)
        glu = jnp.minimum(_dot(x, w1g_ref[0]) + b1g_ref[0], SWIGLU_LIMIT)
        lin = jnp.clip(_dot(x, w1l_ref[0]) + b1l_ref[0], -SWIGLU_LIMIT, SWIGLU_LIMIT)
        act = glu * _sigmoid(SWIGLU_ALPHA * glu) * (lin + 1.0)
        y_ref[...] = _dot(act.astype(BF16), w2_ref[0]) + b2_ref[0]

    @pl.when(bv_ref[i] != 1)
    def _():
        y_ref[...] = jnp.zeros_like(y_ref)


def _experts(xs, block_e, block_valid, w1g, w1l, b1g, b1l, w2, b2, *, blk):
    n_pad, d = xs.shape
    ne, _, f = w1g.shape
    nb = n_pad // blk
    wmap = lambda i, be, bv: (be[i], 0, 0)
    grid_spec = pltpu.PrefetchScalarGridSpec(
        num_scalar_prefetch=2,
        grid=(nb,),
        in_specs=[
            pl.BlockSpec((blk, d), lambda i, be, bv: (i, 0)),
            pl.BlockSpec((1, d, f), wmap),
            pl.BlockSpec((1, d, f), wmap),
            pl.BlockSpec((1, 1, f), wmap),
            pl.BlockSpec((1, 1, f), wmap),
            pl.BlockSpec((1, f, d), wmap),
            pl.BlockSpec((1, 1, d), wmap),
        ],
        out_specs=pl.BlockSpec((blk, d), lambda i, be, bv: (i, 0)),
    )
    return pl.pallas_call(
        _expert_kernel,
        out_shape=jax.ShapeDtypeStruct((n_pad, d), F32),
        grid_spec=grid_spec,
        compiler_params=_cparams(("arbitrary",)),
        name="moe_experts",
    )(block_e, block_valid, xs, w1g, w1l, b1g, b1l, w2, b2)


def _combine_ln_kernel(y_ref, tg_ref, x_ref, mod_ref, g_ref, b_ref, o_ref, *, alpha):
    d = x_ref.shape[1]
    tg = tg_ref[...]
    acc = tg[:, 0:1] * y_ref[:, 0:d]
    for kk in range(1, TOP_K):
        acc = acc + tg[:, kk:kk + 1] * y_ref[:, kk * d:(kk + 1) * d]
    gate = mod_ref[0][5:6, :]
    o_ref[...] = _deepnorm_ln(x_ref[...], acc, gate, g_ref[...], b_ref[...], alpha)


def _combine_ln(y2, tg, x, mod, ln_g, ln_b, *, seq, alpha, tm):
    t, d = x.shape
    tm = min(tm, seq)
    return pl.pallas_call(
        functools.partial(_combine_ln_kernel, alpha=alpha),
        out_shape=jax.ShapeDtypeStruct((t, d), F32),
        grid=(t // tm,),
        in_specs=[
            pl.BlockSpec((tm, TOP_K * d), lambda i: (i, 0)),
            pl.BlockSpec((tm, LANES), lambda i: (i, 0)),
            pl.BlockSpec((tm, d), lambda i: (i, 0)),
            pl.BlockSpec((1, 6, d), lambda i: ((i * tm) // seq, 0, 0)),
            pl.BlockSpec((1, d), lambda i: (0, 0)),
            pl.BlockSpec((1, d), lambda i: (0, 0)),
        ],
        out_specs=pl.BlockSpec((tm, d), lambda i: (i, 0)),
        compiler_params=_cparams(("parallel",)),
        name="moe_combine_ln",
    )(y2, tg, x, mod, ln_g.reshape(1, d), ln_b.reshape(1, d))


def _dispatch_plan(top_i, *, t, ne, blk):
    n = t * TOP_K
    n_blocks = n // blk + ne
    n_pad = n_blocks * blk
    e_flat = top_i[:t, :TOP_K].reshape(n)
    order = jnp.argsort(e_flat, stable=True).astype(I32)
    counts = jnp.sum((e_flat[:, None] == jnp.arange(ne, dtype=I32)[None, :]).astype(I32), axis=0)
    starts = jnp.cumsum(counts) - counts
    padded = (counts + blk - 1) // blk * blk
    pad_ends = jnp.cumsum(padded)
    pad_starts = pad_ends - padded
    blk_start = jnp.arange(n_blocks, dtype=I32) * blk
    block_e = jnp.minimum(jnp.searchsorted(pad_ends, blk_start, side="right"), ne - 1).astype(I32)
    block_valid = (blk_start < pad_ends[-1]).astype(I32)
    e_slot = jnp.repeat(block_e, blk)
    v_slot = jnp.repeat(block_valid, blk)
    slot = jnp.arange(n_pad, dtype=I32)
    r = slot - pad_starts[e_slot]
    valid = jnp.logical_and(r < counts[e_slot], v_slot == 1)
    p = jnp.clip(starts[e_slot] + r, 0, n - 1)
    flat = order[p]
    src_tok = jnp.where(valid, flat // TOP_K, t).astype(I32)
    dst_flat = jnp.where(valid, flat, -1).astype(I32)
    return block_e, block_valid, slot, src_tok, dst_flat


def _moe_ffn(x, mod, w_router, b_router, w1, b1, w2, b2, ln_g, ln_b, *, seq, alpha):
    t, d = x.shape
    ne = w_router.shape[1]
    f = w2.shape[1]
    tm_r = min(512, seq)
    h_pad, top_i, top_g = _router(x, mod, w_router, b_router, seq=seq, tm=tm_r)
    block_e, block_valid, slot, src_tok, dst_flat = _dispatch_plan(top_i, t=t, ne=ne, blk=MOE_BLOCK)
    n_pad = slot.shape[0]
    tiles = d // LANES
    xs3 = _rowcopy(h_pad.reshape(t + tm_r, tiles, LANES), src_tok, slot, n_pad)
    w1g = w1[:, :, 0::2].astype(BF16)
    w1l = w1[:, :, 1::2].astype(BF16)
    b1g = b1[:, 0::2].reshape(ne, 1, f)
    b1l = b1[:, 1::2].reshape(ne, 1, f)
    ys = _experts(xs3.reshape(n_pad, d), block_e, block_valid, w1g, w1l, b1g, b1l,
                  w2.astype(BF16), b2.reshape(ne, 1, d), blk=MOE_BLOCK)
    tm_c = min(256, seq)
    y3 = _rowcopy(ys.reshape(n_pad, tiles, LANES), slot, dst_flat, t * TOP_K)
    return _combine_ln(y3.reshape(t, TOP_K * d), top_g, x, mod, ln_g, ln_b, seq=seq, alpha=alpha, tm=tm_c)


def kernel(x, c, ada_w, ada_b, ln_g, ln_b, mlstm_w_in, mlstm_b_gates, mlstm_norm_w, mlstm_w_out, hgrn_w_in, hgrn_lb_param, hgrn_norm_w, hgrn_w_out, ssm_w_in, ssm_conv_w, ssm_conv_b, ssm_dt_bias, ssm_a_log, ssm_d, ssm_norm_w, ssm_w_out, moe_w_router, moe_b_router, moe_w1, moe_b1, moe_w2, moe_b2):
    bsz, seq, d = x.shape
    depth = ada_w.shape[0]
    alpha = (2 * depth) ** 0.25
    mods = _ada_mods(c, ada_w, ada_b)
    xf = x.reshape(bsz * seq, d)
    for i in range(depth):
        mod = mods[i]
        kind, j = i % 3, i // 3
        if kind == 0:
            xf = _mlstm_mixer(xf, mod, mlstm_w_in[j], mlstm_b_gates[j], mlstm_norm_w[j], mlstm_w_out[j],
                              ln_g[i, 0], ln_b[i, 0], bsz=bsz, seq=seq, alpha=alpha)
        elif kind == 1:
            xf = _hgrn_mixer(xf, mod, hgrn_w_in[j], hgrn_lb_param, hgrn_norm_w[j], hgrn_w_out[j],
                             ln_g[i, 0], ln_b[i, 0], bsz=bsz, seq=seq, alpha=alpha, layer=i)
        else:
            xf = _ssd_mixer(xf, mod, ssm_w_in[j], ssm_conv_w[j], ssm_conv_b[j], ssm_dt_bias[j], ssm_a_log[j],
                            ssm_d[j], ssm_norm_w[j], ssm_w_out[j], ln_g[i, 0], ln_b[i, 0],
                            bsz=bsz, seq=seq, alpha=alpha)
        xf = _moe_ffn(xf, mod, moe_w_router[i], moe_b_router[i], moe_w1[i], moe_b1[i], moe_w2[i], moe_b2[i],
                      ln_g[i, 1], ln_b[i, 1], seq=seq, alpha=alpha)
    return xf.reshape(bsz, seq, d)
```

```python
import functools

import jax
import jax.numpy as jnp
from jax import lax
from jax.experimental import pallas as pl
from jax.experimental.pallas import tpu as pltpu

F32 = jnp.float32
BF16 = jnp.bfloat16
I32 = jnp.int32
U32 = jnp.uint32
HI_HALF_MASK = 0xFFFF0000

MLSTM_GATE_CAP = 15.0
HGRN_HEAD_DIM = 128
SSM_GROUPS = 8
SSM_HEADDIM = 64
TOP_K = 4
SWIGLU_ALPHA = 1.702
SWIGLU_LIMIT = 7.0
LN_EPS = 1e-5
RMS_EPS = 1e-6

LANES = 128
SUBLANES = 8
VMEM_LIMIT_BYTES = 56 * 1024 * 1024

MLSTM_CHUNK = 128
HGRN_CHUNK = 64
HGRN_SUB = 16
HGRN_ROWS_PER_STEP = 512
SSD_CHUNK = 128
MOE_BLOCK = 512

_HI = lax.Precision.HIGHEST


def _cparams(sem):
    return pltpu.CompilerParams(dimension_semantics=sem, vmem_limit_bytes=VMEM_LIMIT_BYTES)


def _dot(a, b):
    return jnp.dot(a, b, preferred_element_type=F32)


def _dot_nt(a, b):
    return lax.dot_general(a, b, (((1,), (1,)), ((), ())), preferred_element_type=F32)


def _dot_tn(a, b):
    return lax.dot_general(a, b, (((0,), (0,)), ((), ())), preferred_element_type=F32)


def _dot_hi(a, b):
    return jnp.dot(a, b, preferred_element_type=F32, precision=_HI)


def _sigmoid(x):
    return 1.0 / (1.0 + jnp.exp(-x))


def _silu(x):
    return x * _sigmoid(x)


def _log_sigmoid(x):
    return jnp.minimum(x, 0.0) - jnp.log1p(jnp.exp(-jnp.abs(x)))


def _softplus(x):
    return jnp.maximum(x, 0.0) + jnp.log1p(jnp.exp(-jnp.abs(x)))


def _lower_tri(n):
    r = lax.broadcasted_iota(I32, (n, n), 0)
    c = lax.broadcasted_iota(I32, (n, n), 1)
    return r >= c


def _ada_kernel(c_ref, w_ref, b_ref, o_ref):
    cond = _silu(c_ref[...])
    o_ref[0] = _dot(cond.astype(BF16), w_ref[0].astype(BF16)) + b_ref[0]


def _ada_mods(c, ada_w, ada_b):
    depth, d, n6 = ada_w.shape
    bsz = c.shape[0]
    rows = -(-bsz // SUBLANES) * SUBLANES
    c_pad = jnp.zeros((rows, d), F32).at[:bsz].set(c)
    tn = 1536 if n6 % 1536 == 0 else n6
    out = pl.pallas_call(
        _ada_kernel,
        out_shape=jax.ShapeDtypeStruct((depth, rows, n6), F32),
        grid=(depth, n6 // tn),
        in_specs=[
            pl.BlockSpec((rows, d), lambda l, j: (0, 0)),
            pl.BlockSpec((1, d, tn), lambda l, j: (l, 0, j)),
            pl.BlockSpec((1, 1, tn), lambda l, j: (l, 0, j)),
        ],
        out_specs=pl.BlockSpec((1, rows, tn), lambda l, j: (l, 0, j)),
        compiler_params=_cparams(("parallel", "parallel")),
        name="ada_mods",
    )(c_pad, ada_w, ada_b.reshape(depth, 1, n6))
    return out[:, :bsz].reshape(depth, bsz, 6, d)


def _linear_mod_kernel(x_ref, mod_ref, w_ref, o_ref, h_ref, *, shift_row, scale_row):
    @pl.when(pl.program_id(1) == 0)
    def _():
        m = mod_ref[0]
        h = x_ref[...] * (1.0 + m[scale_row:scale_row + 1, :]) + m[shift_row:shift_row + 1, :]
        h_ref[...] = h.astype(BF16)

    o_ref[...] = _dot(h_ref[...], w_ref[...]).astype(o_ref.dtype)


def _linear_mod(x, mod, w, *, seq, shift_row, scale_row, out_dtype, tm, tn, name):
    t, d = x.shape
    n = w.shape[1]
    tm = min(tm, seq)
    tn = min(tn, n)
    assert t % tm == 0 and seq % tm == 0 and n % tn == 0
    return pl.pallas_call(
        functools.partial(_linear_mod_kernel, shift_row=shift_row, scale_row=scale_row),
        out_shape=jax.ShapeDtypeStruct((t, n), out_dtype),
        grid=(t // tm, n // tn),
        in_specs=[
            pl.BlockSpec((tm, d), lambda i, j: (i, 0)),
            pl.BlockSpec((1, 6, d), lambda i, j: ((i * tm) // seq, 0, 0)),
            pl.BlockSpec((d, tn), lambda i, j: (0, j)),
        ],
        out_specs=pl.BlockSpec((tm, tn), lambda i, j: (i, j)),
        scratch_shapes=[pltpu.VMEM((tm, d), BF16)],
        compiler_params=_cparams(("parallel", "arbitrary")),
        name=name,
    )(x, mod, w)


def _deepnorm_ln(x, y, gate, g, b, alpha):
    v = alpha * x + (1.0 + gate) * y
    mu = jnp.mean(v, axis=-1, keepdims=True)
    dv = v - mu
    var = jnp.mean(dv * dv, axis=-1, keepdims=True)
    return dv * lax.rsqrt(var + LN_EPS) * g + b


def _outproj_ln_kernel(y_ref, w_ref, x_ref, mod_ref, g_ref, b_ref, o_ref, acc_ref, *, gate_row, nk, alpha):
    k = pl.program_id(1)
    part = _dot(y_ref[...], w_ref[...])

    def finish(acc):
        gate = mod_ref[0][gate_row:gate_row + 1, :]
        o_ref[...] = _deepnorm_ln(x_ref[...], acc, gate, g_ref[...], b_ref[...], alpha)

    if nk == 1:
        finish(part)
    else:
        @pl.when(k == 0)
        def _():
            acc_ref[...] = part

        @pl.when(jnp.logical_and(k > 0, k < nk - 1))
        def _():
            acc_ref[...] += part

        @pl.when(k == nk - 1)
        def _():
            finish(acc_ref[...] + part)


def _outproj_ln(y, w, x, mod, ln_g, ln_b, *, seq, gate_row, alpha, tm, tk, name):
    t, kdim = y.shape
    d = w.shape[1]
    tm = min(tm, seq)
    tk = min(tk, kdim)
    assert t % tm == 0 and seq % tm == 0 and kdim % tk == 0
    nk = kdim // tk
    return pl.pallas_call(
        functools.partial(_outproj_ln_kernel, gate_row=gate_row, nk=nk, alpha=alpha),
        out_shape=jax.ShapeDtypeStruct((t, d), F32),
        grid=(t // tm, nk),
        in_specs=[
            pl.BlockSpec((tm, tk), lambda i, k: (i, k)),
            pl.BlockSpec((tk, d), lambda i, k: (k, 0)),
            pl.BlockSpec((tm, d), lambda i, k: (i, 0)),
            pl.BlockSpec((1, 6, d), lambda i, k: ((i * tm) // seq, 0, 0)),
            pl.BlockSpec((1, d), lambda i, k: (0, 0)),
            pl.BlockSpec((1, d), lambda i, k: (0, 0)),
        ],
        out_specs=pl.BlockSpec((tm, d), lambda i, k: (i, 0)),
        scratch_shapes=[pltpu.VMEM((tm, d), F32)],
        compiler_params=_cparams(("parallel", "arbitrary")),
        name=name,
    )(y, w, x, mod, ln_g.reshape(1, d), ln_b.reshape(1, d))


def _mlstm_kernel(q_ref, k_ref, v_ref, o_ref, aux_ref, bias_ref, nw_ref, y_ref,
                  c_ref, n_ref, m_ref, *, heads, dqk, dv, chunk):
    @pl.when(pl.program_id(1) == 0)
    def _():
        c_ref[...] = jnp.zeros_like(c_ref)
        n_ref[...] = jnp.zeros_like(n_ref)
        m_ref[...] = jnp.zeros_like(m_ref)

    scale = dqk ** -0.5
    g = MLSTM_GATE_CAP * jnp.tanh((aux_ref[...] + bias_ref[...]) / MLSTM_GATE_CAP)
    lane = lax.broadcasted_iota(I32, g.shape, 1)
    logsig = _log_sigmoid(g)
    is_f = jnp.logical_and(lane >= heads, lane < 2 * heads)
    tri = _lower_tri(chunk)
    bcum = _dot_hi(tri.astype(F32), jnp.where(is_f, logsig, 0.0))
    g_t = g.T
    bcum_t = bcum.T

    for h in range(heads):
        ig_col = g[:, h:h + 1]
        ig_row = g_t[h:h + 1, :]
        b_col = bcum[:, heads + h:heads + h + 1]
        b_row = bcum_t[heads + h:heads + h + 1, :]
        m_prev = m_ref[h][0:1, 0:1]

        log_d = jnp.where(tri, b_col - b_row + ig_row, -jnp.inf)
        m_inter = b_col + m_prev
        m_t = jnp.maximum(jnp.max(log_d, axis=1, keepdims=True), m_inter)
        dmat = jnp.exp(log_d - m_t)

        q = q_ref[:, h * dqk:(h + 1) * dqk]
        k = k_ref[:, h * dqk:(h + 1) * dqk]
        v = v_ref[:, h * dv:(h + 1) * dv]
        scores = _dot_nt(q, k) * scale * dmat
        inter = jnp.exp(m_inter - m_t)
        c_st = c_ref[h]
        n_st = n_ref[h][0:1, :]
        num = _dot(scores.astype(BF16), v) + inter * (_dot_nt(q, c_st.astype(BF16)) * scale)
        qn = jnp.sum(q.astype(F32) * n_st, axis=1, keepdims=True) * scale
        den = jnp.sum(scores, axis=1, keepdims=True) + inter * qn
        hout = num * (1.0 / jnp.maximum(jnp.abs(den), jnp.exp(-m_t)))

        b_last = b_col[chunk - 1:chunk, :]
        log_w = b_last - b_col + ig_col
        m_new = jnp.maximum(b_last + m_prev, jnp.max(log_w, axis=0, keepdims=True))
        w = jnp.exp(log_w - m_new)
        decay = jnp.exp(b_last + m_prev - m_new)
        vw = (v.astype(F32) * w).astype(BF16)
        c_ref[h] = decay * c_st + _dot_tn(vw, k)
        n_new = decay * n_st + jnp.sum(k.astype(F32) * w, axis=0, keepdims=True)
        n_ref[h] = jnp.broadcast_to(n_new, n_ref.shape[1:])
        m_ref[h] = jnp.broadcast_to(m_new, m_ref.shape[1:])

        ms = jnp.mean(hout * hout, axis=1, keepdims=True)
        hn = hout * lax.rsqrt(ms + RMS_EPS) * nw_ref[:, h * dv:(h + 1) * dv]
        og = o_ref[:, h * dv:(h + 1) * dv].astype(F32)
        y_ref[:, h * dv:(h + 1) * dv] = (hn * _sigmoid(og)).astype(y_ref.dtype)


def _mlstm_recurrence(main, aux, bias, norm_w, *, bsz, seq, heads, dqk, dv):
    t = main.shape[0]
    nqk, nv = heads * dqk, heads * dv
    chunk = min(MLSTM_CHUNK, seq)
    nc = seq // chunk
    assert nv == 2 * nqk and seq % chunk == 0
    row = lambda b, c: b * nc + c
    return pl.pallas_call(
        functools.partial(_mlstm_kernel, heads=heads, dqk=dqk, dv=dv, chunk=chunk),
        out_shape=jax.ShapeDtypeStruct((t, nv), BF16),
        grid=(bsz, nc),
        in_specs=[
            pl.BlockSpec((chunk, nqk), lambda b, c: (row(b, c), 0)),
            pl.BlockSpec((chunk, nqk), lambda b, c: (row(b, c), 1)),
            pl.BlockSpec((chunk, nv), lambda b, c: (row(b, c), 1)),
            pl.BlockSpec((chunk, nv), lambda b, c: (row(b, c), 2)),
            pl.BlockSpec((chunk, LANES), lambda b, c: (row(b, c), 0)),
            pl.BlockSpec((1, LANES), lambda b, c: (0, 0)),
            pl.BlockSpec((1, nv), lambda b, c: (0, 0)),
        ],
        out_specs=pl.BlockSpec((chunk, nv), lambda b, c: (row(b, c), 0)),
        scratch_shapes=[
            pltpu.VMEM((heads, dv, dqk), F32),
            pltpu.VMEM((heads, SUBLANES, dqk), F32),
            pltpu.VMEM((heads, SUBLANES, LANES), F32),
        ],
        compiler_params=_cparams(("parallel", "arbitrary")),
        name="mlstm_recurrence",
    )(main, main, main, main, aux, bias, norm_w.reshape(1, nv))


def _mlstm_mixer(x, mod, w_in, b_gates, norm_w, w_out, ln_g, ln_b, *, bsz, seq, alpha):
    d = x.shape[1]
    heads = b_gates.shape[0] // 2
    nv = w_out.shape[0]
    nqk = (w_in.shape[1] - 2 * nv - 2 * heads) // 2
    n_main = 2 * nqk + 2 * nv
    w_main = w_in[:, :n_main].astype(BF16)
    w_aux = jnp.zeros((d, LANES), BF16).at[:, :2 * heads].set(w_in[:, n_main:].astype(BF16))
    bias = jnp.zeros((1, LANES), F32).at[0, :2 * heads].set(b_gates)
    main = _linear_mod(x, mod, w_main, seq=seq, shift_row=0, scale_row=1, out_dtype=BF16,
                       tm=1024, tn=512, name="mlstm_in_main")
    aux = _linear_mod(x, mod, w_aux, seq=seq, shift_row=0, scale_row=1, out_dtype=F32,
                      tm=1024, tn=LANES, name="mlstm_in_gates")
    y = _mlstm_recurrence(main, aux, bias, norm_w, bsz=bsz, seq=seq, heads=heads,
                          dqk=nqk // heads, dv=nv // heads)
    return _outproj_ln(y, w_out.astype(BF16), x, mod, ln_g, ln_b, seq=seq, gate_row=2, alpha=alpha,
                       tm=512, tk=2048, name="mlstm_out_ln")


def _hgrn_kernel(main_ref, lbp_ref, nw_ref, y_ref, st_ref, *, layer, chunk, sub, nchunks):
    dk = HGRN_HEAD_DIM

    @pl.when(pl.program_id(2) == 0)
    def _():
        st_ref[...] = jnp.zeros_like(st_ref)

    p = lbp_ref[...]
    e = jnp.exp(p - jnp.max(p, axis=0, keepdims=True))
    sm = e / jnp.sum(e, axis=0, keepdims=True)
    lb = jnp.zeros((1, dk), F32)
    for r in range(1, layer + 1):
        lb = lb + sm[r:r + 1, :]
    log_lb = jnp.log(lb)
    log_1m = jnp.log1p(-lb)
    one_m = 1.0 - lb
    tri = _lower_tri(chunk).astype(F32)
    nsub = chunk // sub
    sub_row = lax.broadcasted_iota(I32, (sub, dk), 0)
    ones_sq = jnp.ones((dk, dk), BF16)
    nw = nw_ref[...]

    def body(ci, carry):
        r0 = pl.multiple_of(ci * chunk, chunk)
        blk = main_ref[pl.ds(r0, chunk), :]
        qraw = blk[:, 0:dk].astype(F32)
        fraw = blk[:, dk:2 * dk].astype(F32)
        iv_b = blk[:, 2 * dk:3 * dk]
        iv = iv_b.astype(F32)
        graw = blk[:, 3 * dk:4 * dk].astype(F32)

        q = _silu(qraw)
        b2 = log_1m + _log_sigmoid(fraw)
        logf = jnp.maximum(log_lb, b2) + jnp.log1p(jnp.exp(-jnp.abs(log_lb - b2)))
        k = one_m * _sigmoid(-fraw)
        gcum = _dot_hi(tri, logf)

        st = st_ref[...]
        o_inter = _dot_nt((q * jnp.exp(gcum)).astype(BF16), st.astype(BF16))

        outs = []
        for bi in range(nsub):
            lo = bi * sub
            g_i = gcum[lo:lo + sub]
            q_i = q[lo:lo + sub]
            k_i = k[lo:lo + sub]
            i_i = iv[lo:lo + sub]
            acc = o_inter[lo:lo + sub]
            if bi > 0:
                ref_row = gcum[lo - 1:lo]
                qt = (q_i * jnp.exp(g_i - ref_row)).astype(BF16)
                kt = (k[0:lo] * jnp.exp(ref_row - gcum[0:lo])).astype(BF16)
                a = _dot_nt(qt, kt)
                acc = acc + _dot(a.astype(BF16), iv_b[0:lo])
            pws = []
            for s in range(sub):
                arg = jnp.where(sub_row >= s, g_i - g_i[s:s + 1], -jnp.inf)
                pws.append((q_i * jnp.exp(arg) * k_i[s:s + 1]).astype(BF16))
            rs = _dot(jnp.concatenate(pws, axis=0), ones_sq)
            for s in range(sub):
                acc = acc + rs[s * sub:(s + 1) * sub] * i_i[s:s + 1]
            outs.append(acc)
        o = jnp.concatenate(outs, axis=0)

        g_last = gcum[chunk - 1:chunk]
        kdec = (k * jnp.exp(g_last - gcum)).astype(BF16)
        st_ref[...] = st * jnp.exp(g_last) + _dot_tn(iv_b, kdec)

        ms = jnp.mean(o * o, axis=1, keepdims=True)
        y = o * lax.rsqrt(ms + RMS_EPS) * nw * _silu(graw)
        y_ref[pl.ds(r0, chunk), :] = y.astype(y_ref.dtype)
        return carry

    lax.fori_loop(0, nchunks, body, 0)


def _hgrn_recurrence(main, lb_param, norm_w, *, bsz, seq, heads, layer):
    t = main.shape[0]
    dk = HGRN_HEAD_DIM
    depth = lb_param.shape[0]
    rows = min(HGRN_ROWS_PER_STEP, seq)
    chunk = min(HGRN_CHUNK, rows)
    assert seq % rows == 0 and rows % chunk == 0 and chunk % HGRN_SUB == 0
    nr = seq // rows
    return pl.pallas_call(
        functools.partial(_hgrn_kernel, layer=layer, chunk=chunk, sub=HGRN_SUB, nchunks=rows // chunk),
        out_shape=jax.ShapeDtypeStruct((t, heads * dk), BF16),
        grid=(bsz, heads, nr),
        in_specs=[
            pl.BlockSpec((rows, 4 * dk), lambda b, h, c: (b * nr + c, h)),
            pl.BlockSpec((depth, dk), lambda b, h, c: (0, h)),
            pl.BlockSpec((1, dk), lambda b, h, c: (0, 0)),
        ],
        out_specs=pl.BlockSpec((rows, dk), lambda b, h, c: (b * nr + c, h)),
        scratch_shapes=[pltpu.VMEM((dk, dk), F32)],
        compiler_params=_cparams(("parallel", "parallel", "arbitrary")),
        name="hgrn_recurrence",
    )(main, lb_param, norm_w.reshape(1, dk))


def _hgrn_mixer(x, mod, w_in, lb_param, norm_w, w_out, ln_g, ln_b, *, bsz, seq, alpha, layer):
    d = x.shape[1]
    dk = HGRN_HEAD_DIM
    heads = d // dk
    w_perm = w_in.reshape(d, 4, heads, dk).transpose(0, 2, 1, 3).reshape(d, 4 * d).astype(BF16)
    main = _linear_mod(x, mod, w_perm, seq=seq, shift_row=0, scale_row=1, out_dtype=BF16,
                       tm=1024, tn=512, name="hgrn_in")
    y = _hgrn_recurrence(main, lb_param, norm_w, bsz=bsz, seq=seq, heads=heads, layer=layer)
    return _outproj_ln(y, w_out.astype(BF16), x, mod, ln_g, ln_b, seq=seq, gate_row=2, alpha=alpha,
                       tm=512, tk=2048, name="hgrn_out_ln")


def _conv_kernel(x_ref, halo_ref, w_ref, b_ref, o_ref, *, taps, steps_per_seq):
    first = (pl.program_id(0) % steps_per_seq) == 0
    x = x_ref[...].astype(F32)
    halo = jnp.where(first, 0.0, halo_ref[...].astype(F32))
    xc = jnp.concatenate([halo, x], axis=0)
    hr, tm = halo.shape[0], x.shape[0]
    acc = jnp.broadcast_to(b_ref[...], x.shape)
    for kk in range(taps):
        shift = taps - 1 - kk
        xs = xc if shift == 0 else pltpu.roll(xc, shift, 0)
        acc = acc + w_ref[kk:kk + 1, :] * xs[hr:hr + tm]
    o_ref[...] = _silu(acc).astype(o_ref.dtype)


def _ssd_conv(main, conv_w, conv_b, *, seq, col0):
    t = main.shape[0]
    taps, cdim = conv_w.shape
    tm = min(512, seq)
    tc = 512
    halo = 2 * SUBLANES
    assert seq % tm == 0 and cdim % tc == 0 and col0 % tc == 0 and taps - 1 <= halo and tm % halo == 0
    cb0 = col0 // tc
    hb = tm // halo
    return pl.pallas_call(
        functools.partial(_conv_kernel, taps=taps, steps_per_seq=seq // tm),
        out_shape=jax.ShapeDtypeStruct((t, cdim), BF16),
        grid=(t // tm, cdim // tc),
        in_specs=[
            pl.BlockSpec((tm, tc), lambda i, j: (i, cb0 + j)),
            pl.BlockSpec((halo, tc), lambda i, j: (jnp.maximum(i * hb - 1, 0), cb0 + j)),
            pl.BlockSpec((taps, tc), lambda i, j: (0, j)),
            pl.BlockSpec((1, tc), lambda i, j: (0, j)),
        ],
        out_specs=pl.BlockSpec((tm, tc), lambda i, j: (i, j)),
        compiler_params=_cparams(("parallel", "parallel")),
        name="ssd_conv",
    )(main, main, conv_w, conv_b.reshape(1, cdim))


def _ssd_kernel(xs_ref, b_ref, c_ref, z_ref, aux_ref, dtb_ref, alog_ref, dsk_ref, nw_ref, y_ref,
                state_ref, dt_ref, cum_ref, dt_t_ref, cum_t_ref, *, chunk, hpg, hd):
    cidx = pl.program_id(1)
    g = pl.program_id(2)

    @pl.when(jnp.logical_and(cidx == 0, g == 0))
    def _():
        state_ref[...] = jnp.zeros_like(state_ref)

    tri = _lower_tri(chunk)

    @pl.when(g == 0)
    def _():
        dt = _softplus(aux_ref[...] + dtb_ref[...])
        la = dt * (-jnp.exp(alog_ref[...]))
        cum = _dot_hi(tri.astype(F32), la)
        dt_ref[...] = dt
        cum_ref[...] = cum
        dt_t_ref[...] = dt.T
        cum_t_ref[...] = cum.T

    dt_nat = dt_ref[...]
    cum_nat = cum_ref[...]
    g8 = pl.multiple_of(g * hpg, hpg)
    cum_t = cum_t_ref[pl.ds(g8, hpg), :]
    dt_t = dt_t_ref[pl.ds(g8, hpg), :]
    lane = lax.broadcasted_iota(I32, cum_nat.shape, 1)

    xs = xs_ref[...].astype(F32)
    x_t = xs.T
    bm = b_ref[...]
    cm = c_ref[...]
    cb_t = _dot_nt(bm, cm)
    st = state_ref[g]
    inter_t = _dot_nt(st.astype(BF16), cm)
    last = cum_t[:, chunk - 1:chunk]
    w_t = jnp.exp(last - cum_t) * dt_t
    ecum_t = jnp.exp(cum_t)
    s_le_t = jnp.logical_not(tri) | (lax.broadcasted_iota(I32, (chunk, chunk), 0)
                                     == lax.broadcasted_iota(I32, (chunk, chunk), 1))

    y_parts = []
    xw_parts = []
    st_parts = []
    for j in range(hpg):
        col = g8 + j
        sel = lane == col
        cum_col = jnp.sum(jnp.where(sel, cum_nat, 0.0), axis=1, keepdims=True)
        dt_col = jnp.sum(jnp.where(sel, dt_nat, 0.0), axis=1, keepdims=True)
        cum_row = cum_t[j:j + 1, :]
        dec_t = jnp.exp(jnp.where(s_le_t, cum_row - cum_col, -jnp.inf))
        m_t = (cb_t * dec_t * dt_col).astype(BF16)
        x_j = x_t[j * hd:(j + 1) * hd, :]
        y_j = _dot(x_j.astype(BF16), m_t) + inter_t[j * hd:(j + 1) * hd, :] * ecum_t[j:j + 1, :]
        y_parts.append(y_j)
        xw_parts.append((x_j * w_t[j:j + 1, :]).astype(BF16))
        st_parts.append(st[j * hd:(j + 1) * hd, :] * jnp.exp(last[j:j + 1, :]))
    y_t = jnp.concatenate(y_parts, axis=0)
    xw_t = jnp.concatenate(xw_parts, axis=0)
    state_ref[g] = jnp.concatenate(st_parts, axis=0) + _dot(xw_t, bm)

    y = y_t.T + dsk_ref[...] * xs
    y = y * _silu(z_ref[...].astype(F32))
    ms = jnp.mean(y * y, axis=1, keepdims=True)
    y_ref[...] = (y * lax.rsqrt(ms + RMS_EPS) * nw_ref[...]).astype(y_ref.dtype)


def _ssd_recurrence(main, conv, aux, dt_bias, a_log, d_skip, norm_w, *, bsz, seq, d_inner, nstate):
    t = main.shape[0]
    groups = SSM_GROUPS
    gw = d_inner // groups
    hpg = gw // SSM_HEADDIM
    nheads = groups * hpg
    chunk = min(SSD_CHUNK, seq)
    assert seq % chunk == 0 and nstate == LANES and chunk == LANES and nheads <= LANES
    nc = seq // chunk
    row = lambda b, c, g: b * nc + c
    pad = lambda v: jnp.zeros((1, LANES), F32).at[0, :nheads].set(v)
    dsk = jnp.repeat(d_skip, SSM_HEADDIM).reshape(1, d_inner)
    b0 = d_inner // nstate
    return pl.pallas_call(
        functools.partial(_ssd_kernel, chunk=chunk, hpg=hpg, hd=SSM_HEADDIM),
        out_shape=jax.ShapeDtypeStruct((t, d_inner), BF16),
        grid=(bsz, nc, groups),
        in_specs=[
            pl.BlockSpec((chunk, gw), lambda b, c, g: (row(b, c, g), g)),
            pl.BlockSpec((chunk, nstate), lambda b, c, g: (row(b, c, g), b0 + g)),
            pl.BlockSpec((chunk, nstate), lambda b, c, g: (row(b, c, g), b0 + groups + g)),
            pl.BlockSpec((chunk, gw), lambda b, c, g: (row(b, c, g), g)),
            pl.BlockSpec((chunk, LANES), lambda b, c, g: (row(b, c, g), 0)),
            pl.BlockSpec((1, LANES), lambda b, c, g: (0, 0)),
            pl.BlockSpec((1, LANES), lambda b, c, g: (0, 0)),
            pl.BlockSpec((1, gw), lambda b, c, g: (0, g)),
            pl.BlockSpec((1, gw), lambda b, c, g: (0, g)),
        ],
        out_specs=pl.BlockSpec((chunk, gw), lambda b, c, g: (row(b, c, g), g)),
        scratch_shapes=[
            pltpu.VMEM((groups, gw, nstate), F32),
            pltpu.VMEM((chunk, LANES), F32),
            pltpu.VMEM((chunk, LANES), F32),
            pltpu.VMEM((LANES, chunk), F32),
            pltpu.VMEM((LANES, chunk), F32),
        ],
        compiler_params=_cparams(("parallel", "arbitrary", "arbitrary")),
        name="ssd_recurrence",
    )(conv, conv, conv, main, aux, pad(dt_bias), pad(a_log), dsk, norm_w.reshape(1, d_inner))


def _ssd_mixer(x, mod, w_in, conv_w, conv_b, dt_bias, a_log, d_skip, norm_w, w_out, ln_g, ln_b,
               *, bsz, seq, alpha):
    d = x.shape[1]
    d_inner = w_out.shape[0]
    cdim = conv_w.shape[-1]
    nheads = dt_bias.shape[0]
    nstate = (cdim - d_inner) // (2 * SSM_GROUPS)
    n_main = d_inner + cdim
    w_main = w_in[:, :n_main].astype(BF16)
    w_aux = jnp.zeros((d, LANES), BF16).at[:, :nheads].set(w_in[:, n_main:].astype(BF16))
    main = _linear_mod(x, mod, w_main, seq=seq, shift_row=0, scale_row=1, out_dtype=BF16,
                       tm=1024, tn=512, name="ssd_in_main")
    aux = _linear_mod(x, mod, w_aux, seq=seq, shift_row=0, scale_row=1, out_dtype=F32,
                      tm=1024, tn=LANES, name="ssd_in_dt")
    conv = _ssd_conv(main, conv_w.reshape(conv_w.shape[0], cdim), conv_b, seq=seq, col0=d_inner)
    y = _ssd_recurrence(main, conv, aux, dt_bias, a_log, d_skip, norm_w, bsz=bsz, seq=seq,
                        d_inner=d_inner, nstate=nstate)
    return _outproj_ln(y, w_out.astype(BF16), x, mod, ln_g, ln_b, seq=seq, gate_row=2, alpha=alpha,
                       tm=512, tk=2048, name="ssd_out_ln")


def _pack_pairs(lo, hi):
    lo_b = lax.bitcast_convert_type(lo.astype(BF16).astype(F32), U32) >> 16
    hi_b = lax.bitcast_convert_type(hi.astype(BF16).astype(F32), U32) & jnp.uint32(HI_HALF_MASK)
    return lo_b | hi_b


def _unpack_pairs(w):
    lo = lax.bitcast_convert_type(w << 16, F32)
    hi = lax.bitcast_convert_type(w & jnp.uint32(HI_HALF_MASK), F32)
    return lo, hi


def _store_packed_rows(dst_ref, v):
    rows, d = v.shape
    half = d // 2
    pr = half // LANES
    w = _pack_pairs(v[:, :half], v[:, half:])
    for c in range(pr):
        dst_ref[pl.ds(c, rows, stride=pr), :] = w[:, c * LANES:(c + 1) * LANES]


def _load_packed_rows(src_ref, rows, pr):
    return [_unpack_pairs(src_ref[pl.ds(c, rows, stride=pr), :]) for c in range(pr)]


def _router_kernel(x_ref, mod_ref, wr_ref, br_ref, h_ref, ti_ref, tg_ref, *, n_real):
    i = pl.program_id(0)

    @pl.when(i < n_real)
    def _():
        m = mod_ref[0]
        h = x_ref[...] * (1.0 + m[4:5, :]) + m[3:4, :]
        _store_packed_rows(h_ref, h)
        logits = _dot_hi(h, wr_ref[...]) + br_ref[...]
        lane = lax.broadcasted_iota(I32, logits.shape, 1)
        lane_f = lane.astype(F32)
        vals = []
        ti = jnp.zeros(logits.shape, I32)
        for kk in range(TOP_K):
            mx = jnp.max(logits, axis=1, keepdims=True)
            idx = jnp.min(jnp.where(logits == mx, lane_f, float(LANES)), axis=1, keepdims=True).astype(I32)
            vals.append(mx)
            ti = jnp.where(lane == kk, idx, ti)
            logits = jnp.where(lane == idx, -jnp.inf, logits)
        es = [jnp.exp(v - vals[0]) for v in vals]
        tot = es[0]
        for e in es[1:]:
            tot = tot + e
        tg = jnp.zeros(ti.shape, F32)
        for kk in range(TOP_K):
            tg = jnp.where(lane == kk, es[kk] / tot, tg)
        ti_ref[...] = ti
        tg_ref[...] = tg

    @pl.when(i >= n_real)
    def _():
        h_ref[...] = jnp.zeros_like(h_ref)
        ti_ref[...] = jnp.zeros_like(ti_ref)
        tg_ref[...] = jnp.zeros_like(tg_ref)


def _router(x, mod, w_router, b_router, *, seq, tm):
    t, d = x.shape
    ne = w_router.shape[1]
    tm = min(tm, seq)
    n_real = t // tm
    wr = jnp.zeros((d, LANES), F32).at[:, :ne].set(w_router)
    br = jnp.full((1, LANES), -1e30, F32).at[0, :ne].set(b_router)
    clamp = lambda i: jnp.minimum(i, n_real - 1)
    pr = d // (2 * LANES)
    return pl.pallas_call(
        functools.partial(_router_kernel, n_real=n_real),
        out_shape=(
            jax.ShapeDtypeStruct(((t + tm) * pr, LANES), U32),
            jax.ShapeDtypeStruct((t + tm, LANES), I32),
            jax.ShapeDtypeStruct((t + tm, LANES), F32),
        ),
        grid=(n_real + 1,),
        in_specs=[
            pl.BlockSpec((tm, d), lambda i: (clamp(i), 0)),
            pl.BlockSpec((1, 6, d), lambda i: ((clamp(i) * tm) // seq, 0, 0)),
            pl.BlockSpec((d, LANES), lambda i: (0, 0)),
            pl.BlockSpec((1, LANES), lambda i: (0, 0)),
        ],
        out_specs=(
            pl.BlockSpec((tm * pr, LANES), lambda i: (i, 0)),
            pl.BlockSpec((tm, LANES), lambda i: (i, 0)),
            pl.BlockSpec((tm, LANES), lambda i: (i, 0)),
        ),
        compiler_params=_cparams(("parallel",)),
        name="moe_router",
    )(x, mod, wr, br)


def _w1_prep_kernel(w_ref, p_ref, o_ref):
    o_ref[0] = _dot(w_ref[0].astype(BF16), p_ref[...]).astype(BF16)


def _prep_w1(w1):
    ne, d, f2 = w1.shape
    f = f2 // 2
    j = jnp.arange(f2, dtype=I32)
    src_col = jnp.where(j < f, 2 * j, 2 * (j - f) + 1)
    sel = (jnp.arange(f2, dtype=I32)[:, None] == src_col[None, :]).astype(BF16)
    tk = min(512, d)
    return pl.pallas_call(
        _w1_prep_kernel,
        out_shape=jax.ShapeDtypeStruct((ne, d, f2), BF16),
        grid=(ne, d // tk),
        in_specs=[
            pl.BlockSpec((1, tk, f2), lambda e, k: (e, k, 0)),
            pl.BlockSpec((f2, f2), lambda e, k: (0, 0)),
        ],
        out_specs=pl.BlockSpec((1, tk, f2), lambda e, k: (e, k, 0)),
        compiler_params=_cparams(("parallel", "parallel")),
        name="moe_w1_prep",
    )(w1, sel)


def _expert_kernel(be_ref, rows_ref, nv_ref, src_cur_ref, src_nxt_ref, dst_ref, h_hbm, w1_ref, b1_ref, w2_ref, b2_ref,
                   y_hbm, xbuf, ybuf, xs_ref, gsem, ssem, *, blk, nb, f):
    i = pl.program_id(0)
    slot = i % 2
    half = xs_ref.shape[1] // 2
    pr = half // LANES

    def tile(ref, start):
        return ref.at[pl.ds(pl.multiple_of(start, pr), pr)]

    def start_gather(idx_ref, sl):
        def body(r, carry):
            pltpu.make_async_copy(tile(h_hbm, idx_ref[0, r]), tile(xbuf.at[sl], r * pr), gsem.at[sl]).start()
            return carry

        lax.fori_loop(0, blk, body, 0, unroll=8)

    def wait_gather(sl):
        pltpu.make_async_copy(h_hbm.at[pl.ds(0, blk * pr)], xbuf.at[sl], gsem.at[sl]).wait()

    def start_scatter(sl, count):
        def body(r, carry):
            pltpu.make_async_copy(tile(ybuf.at[sl], r * pr), tile(y_hbm, dst_ref[0, r]), ssem.at[sl]).start()
            return carry

        @pl.when(count == blk)
        def _():
            lax.fori_loop(0, blk, body, 0, unroll=8)

        @pl.when(count < blk)
        def _():
            lax.fori_loop(0, count, body, 0)

    def wait_scatter(sl, count):
        @pl.when(count == blk)
        def _():
            pltpu.make_async_copy(ybuf.at[sl], y_hbm.at[pl.ds(0, blk * pr)], ssem.at[sl]).wait()

        @pl.when(count < blk)
        def _():
            def body(r, carry):
                pltpu.make_async_copy(tile(ybuf.at[sl], 0), tile(y_hbm, 0), ssem.at[sl]).wait()
                return carry

            lax.fori_loop(0, count, body, 0)

    @pl.when(jnp.logical_and(i == 0, rows_ref[0] > 0))
    def _():
        start_gather(src_cur_ref, 0)

    nxt = jnp.minimum(i + 1, nb - 1)

    @pl.when(jnp.logical_and(i + 1 < nb, rows_ref[nxt] > 0))
    def _():
        start_gather(src_nxt_ref, 1 - slot)

    @pl.when(rows_ref[i] > 0)
    def _():
        wait_gather(slot)
        for c, (lo, hi) in enumerate(_load_packed_rows(xbuf.at[slot], blk, pr)):
            xs_ref[:, c * LANES:(c + 1) * LANES] = lo.astype(BF16)
            xs_ref[:, half + c * LANES:half + (c + 1) * LANES] = hi.astype(BF16)
        hh = _dot(xs_ref[...], w1_ref[0]) + b1_ref[0]
        glu = jnp.minimum(hh[:, :f], SWIGLU_LIMIT)
        lin = jnp.clip(hh[:, f:], -SWIGLU_LIMIT, SWIGLU_LIMIT)
        act = glu * _sigmoid(SWIGLU_ALPHA * glu) * (lin + 1.0)
        y = _dot(act.astype(BF16), w2_ref[0]) + b2_ref[0]

        @pl.when(i >= 2)
        def _():
            wait_scatter(slot, rows_ref[jnp.maximum(i - 2, 0)])

        _store_packed_rows(ybuf.at[slot], y)
        start_scatter(slot, rows_ref[i])

    @pl.when(i == nb - 1)
    def _():
        nv = nv_ref[0]

        @pl.when(nv >= 1)
        def _():
            wait_scatter((nv - 1) % 2, rows_ref[jnp.maximum(nv - 1, 0)])

        @pl.when(nv >= 2)
        def _():
            wait_scatter(nv % 2, rows_ref[jnp.maximum(nv - 2, 0)])


def _experts(h_rows, plan, w1p, b1p, w2b, b2r, *, e0, blk, n_out):
    block_e, block_rows, n_valid, src_rows, dst_rows = plan
    nb = block_e.shape[0]
    f2 = w1p.shape[2]
    f = f2 // 2
    d = w2b.shape[2]
    pr = d // (2 * LANES)
    wmap = lambda i, be, bv, nv: (e0 + be[i], 0, 0)
    smem_blk = lambda fn: pl.BlockSpec((None, 1, blk), fn, memory_space=pltpu.SMEM)
    grid_spec = pltpu.PrefetchScalarGridSpec(
        num_scalar_prefetch=3,
        grid=(nb,),
        in_specs=[
            smem_blk(lambda i, be, bv, nv: (i, 0, 0)),
            smem_blk(lambda i, be, bv, nv: (jnp.minimum(i + 1, nb - 1), 0, 0)),
            smem_blk(lambda i, be, bv, nv: (i, 0, 0)),
            pl.BlockSpec(memory_space=pl.ANY),
            pl.BlockSpec((1, d, f2), wmap),
            pl.BlockSpec((1, 1, f2), wmap),
            pl.BlockSpec((1, f, d), wmap),
            pl.BlockSpec((1, 1, d), wmap),
        ],
        out_specs=pl.BlockSpec(memory_space=pl.ANY),
        scratch_shapes=[
            pltpu.VMEM((2, blk * pr, LANES), U32),
            pltpu.VMEM((2, blk * pr, LANES), U32),
            pltpu.VMEM((blk, d), BF16),
            pltpu.SemaphoreType.DMA((2,)),
            pltpu.SemaphoreType.DMA((2,)),
        ],
    )
    src3 = src_rows.reshape(nb, 1, blk)
    return pl.pallas_call(
        functools.partial(_expert_kernel, blk=blk, nb=nb, f=f),
        out_shape=jax.ShapeDtypeStruct((n_out * pr, LANES), U32),
        grid_spec=grid_spec,
        compiler_params=_cparams(("arbitrary",)),
        name="moe_experts",
    )(block_e, block_rows, n_valid, src3, src3, dst_rows.reshape(nb, 1, blk), h_rows, w1p, b1p, w2b, b2r)


def _combine_ln_kernel(y0_ref, y1_ref, y2_ref, y3_ref, tg_ref, x_ref, mod_ref, g_ref, b_ref, o_ref, *, alpha):
    tg = tg_ref[...]
    tm, d = x_ref.shape
    pr = d // (2 * LANES)
    los, his = [None] * pr, [None] * pr
    for kk, y_ref in enumerate((y0_ref, y1_ref, y2_ref, y3_ref)):
        gk = tg[:, kk:kk + 1]
        for c, (lo, hi) in enumerate(_load_packed_rows(y_ref, tm, pr)):
            los[c] = gk * lo if kk == 0 else los[c] + gk * lo
            his[c] = gk * hi if kk == 0 else his[c] + gk * hi
    acc = jnp.concatenate(los + his, axis=1)
    gate = mod_ref[0][5:6, :]
    o_ref[...] = _deepnorm_ln(x_ref[...], acc, gate, g_ref[...], b_ref[...], alpha)


def _combine_ln(y_rows, tg, x, mod, ln_g, ln_b, *, seq, alpha, tm):
    t, d = x.shape
    tm = min(tm, seq)
    nt = t // tm
    pr = d // (2 * LANES)
    y_spec = lambda kk: pl.BlockSpec((tm * pr, LANES), lambda i: (kk * nt + i, 0))
    return pl.pallas_call(
        functools.partial(_combine_ln_kernel, alpha=alpha),
        out_shape=jax.ShapeDtypeStruct((t, d), F32),
        grid=(nt,),
        in_specs=[y_spec(kk) for kk in range(TOP_K)] + [
            pl.BlockSpec((tm, LANES), lambda i: (i, 0)),
            pl.BlockSpec((tm, d), lambda i: (i, 0)),
            pl.BlockSpec((1, 6, d), lambda i: ((i * tm) // seq, 0, 0)),
            pl.BlockSpec((1, d), lambda i: (0, 0)),
            pl.BlockSpec((1, d), lambda i: (0, 0)),
        ],
        out_specs=pl.BlockSpec((tm, d), lambda i: (i, 0)),
        compiler_params=_cparams(("parallel",)),
        name="moe_combine_ln",
    )(y_rows, y_rows, y_rows, y_rows, tg, x, mod, ln_g.reshape(1, d), ln_b.reshape(1, d))


def _dispatch_plan(top_i, *, t, ne, blk, row_sublanes):
    n = t * TOP_K
    n_blocks = n // blk + ne
    n_pad = n_blocks * blk
    e_flat = top_i[:t, :TOP_K].reshape(n)
    order = jnp.argsort(e_flat, stable=True).astype(I32)
    counts = jnp.sum((e_flat[:, None] == jnp.arange(ne, dtype=I32)[None, :]).astype(I32), axis=0)
    starts = jnp.cumsum(counts) - counts
    padded = (counts + blk - 1) // blk * blk
    pad_ends = jnp.cumsum(padded)
    pad_starts = pad_ends - padded
    blk_start = jnp.arange(n_blocks, dtype=I32) * blk
    block_e = jnp.minimum(jnp.searchsorted(pad_ends, blk_start, side="right"), ne - 1).astype(I32)
    block_valid = (blk_start < pad_ends[-1]).astype(I32)
    e_slot = jnp.repeat(block_e, blk)
    v_slot = jnp.repeat(block_valid, blk)
    slot = jnp.arange(n_pad, dtype=I32)
    r = slot - pad_starts[e_slot]
    valid = jnp.logical_and(r < counts[e_slot], v_slot == 1)
    p = jnp.clip(starts[e_slot] + r, 0, n - 1)
    flat = order[p]
    src_row = jnp.where(valid, flat // TOP_K, t)
    dst_row = jnp.where(valid, (flat % TOP_K) * t + flat // TOP_K, 0)
    block_rows = jnp.clip(pad_starts[block_e] + counts[block_e] - blk_start, 0, blk) * block_valid
    n_valid = jnp.sum(block_valid).reshape(1)
    plan = (block_e, block_rows.astype(I32), n_valid.astype(I32),
            (src_row * row_sublanes).astype(I32), (dst_row * row_sublanes).astype(I32))
    return plan, n


def _moe_ffn(x, mod, w_router, b_router, w1p, b1p, w2b, b2r, ln_g, ln_b, *, seq, alpha, e0):
    t, d = x.shape
    ne = w_router.shape[1]
    tm_r = min(512, seq)
    h_rows, top_i, top_g = _router(x, mod, w_router, b_router, seq=seq, tm=tm_r)
    plan, n_out = _dispatch_plan(top_i, t=t, ne=ne, blk=MOE_BLOCK, row_sublanes=d // (2 * LANES))
    y_rows = _experts(h_rows, plan, w1p, b1p, w2b, b2r, e0=e0, blk=MOE_BLOCK, n_out=n_out)
    return _combine_ln(y_rows, top_g, x, mod, ln_g, ln_b, seq=seq, alpha=alpha, tm=min(256, seq))


def kernel(x, c, ada_w, ada_b, ln_g, ln_b, mlstm_w_in, mlstm_b_gates, mlstm_norm_w, mlstm_w_out, hgrn_w_in, hgrn_lb_param, hgrn_norm_w, hgrn_w_out, ssm_w_in, ssm_conv_w, ssm_conv_b, ssm_dt_bias, ssm_a_log, ssm_d, ssm_norm_w, ssm_w_out, moe_w_router, moe_b_router, moe_w1, moe_b1, moe_w2, moe_b2):
    bsz, seq, d = x.shape
    depth = ada_w.shape[0]
    alpha = (2 * depth) ** 0.25
    mods = _ada_mods(c, ada_w, ada_b)
    ne, f2 = moe_w1.shape[1], moe_w1.shape[3]
    w1p = _prep_w1(moe_w1.reshape(depth * ne, d, f2))
    b1p = jnp.concatenate([moe_b1[..., 0::2], moe_b1[..., 1::2]], axis=-1).reshape(depth * ne, 1, f2)
    w2b = moe_w2.astype(BF16).reshape(depth * ne, f2 // 2, d)
    b2r = moe_b2.reshape(depth * ne, 1, d)
    xf = x.reshape(bsz * seq, d)
    for i in range(depth):
        mod = mods[i]
        kind, j = i % 3, i // 3
        if kind == 0:
            xf = _mlstm_mixer(xf, mod, mlstm_w_in[j], mlstm_b_gates[j], mlstm_norm_w[j], mlstm_w_out[j],
                              ln_g[i, 0], ln_b[i, 0], bsz=bsz, seq=seq, alpha=alpha)
        elif kind == 1:
            xf = _hgrn_mixer(xf, mod, hgrn_w_in[j], hgrn_lb_param, hgrn_norm_w[j], hgrn_w_out[j],
                             ln_g[i, 0], ln_b[i, 0], bsz=bsz, seq=seq, alpha=alpha, layer=i)
        else:
            xf = _ssd_mixer(xf, mod, ssm_w_in[j], ssm_conv_w[j], ssm_conv_b[j], ssm_dt_bias[j], ssm_a_log[j],
                            ssm_d[j], ssm_norm_w[j], ssm_w_out[j], ln_g[i, 0], ln_b[i, 0],
                            bsz=bsz, seq=seq, alpha=alpha)
        xf = _moe_ffn(xf, mod, moe_w_router[i], moe_b_router[i], w1p, b1p, w2b, b2r,
                      ln_g[i, 1], ln_b[i, 1], seq=seq, alpha=alpha, e0=i * ne)
    return xf.reshape(bsz, seq, d)
```

```python
import functools

import jax
import jax.numpy as jnp
from jax import lax
from jax.experimental import pallas as pl
from jax.experimental.pallas import tpu as pltpu

F32 = jnp.float32
BF16 = jnp.bfloat16
I32 = jnp.int32
U32 = jnp.uint32
HI_HALF_MASK = 0xFFFF0000

MLSTM_GATE_CAP = 15.0
HGRN_HEAD_DIM = 128
SSM_GROUPS = 8
SSM_HEADDIM = 64
TOP_K = 4
SWIGLU_ALPHA = 1.702
SWIGLU_LIMIT = 7.0
LN_EPS = 1e-5
RMS_EPS = 1e-6

LANES = 128
SUBLANES = 8
VMEM_LIMIT_BYTES = 56 * 1024 * 1024

MLSTM_CHUNK = 128
HGRN_CHUNK = 64
HGRN_SUB = 16
HGRN_ROWS_PER_STEP = 512
HGRN_HEADS_PER_STEP = 4
SSD_CHUNK = 128
MOE_BLOCK = 512

_HI = lax.Precision.HIGHEST


def _cparams(sem):
    return pltpu.CompilerParams(dimension_semantics=sem, vmem_limit_bytes=VMEM_LIMIT_BYTES)


def _dot(a, b):
    return jnp.dot(a, b, preferred_element_type=F32)


def _dot_nt(a, b):
    return lax.dot_general(a, b, (((1,), (1,)), ((), ())), preferred_element_type=F32)


def _dot_tn(a, b):
    return lax.dot_general(a, b, (((0,), (0,)), ((), ())), preferred_element_type=F32)


def _dot_hi(a, b):
    return jnp.dot(a, b, preferred_element_type=F32, precision=_HI)


def _sigmoid(x):
    return 1.0 / (1.0 + jnp.exp(-x))


def _silu(x):
    return x * _sigmoid(x)


def _log_sigmoid(x):
    return jnp.minimum(x, 0.0) - jnp.log1p(jnp.exp(-jnp.abs(x)))


def _softplus(x):
    return jnp.maximum(x, 0.0) + jnp.log1p(jnp.exp(-jnp.abs(x)))


def _lower_tri(n):
    r = lax.broadcasted_iota(I32, (n, n), 0)
    c = lax.broadcasted_iota(I32, (n, n), 1)
    return r >= c


def _ada_kernel(c_ref, w_ref, b_ref, o_ref):
    cond = _silu(c_ref[...])
    o_ref[0] = _dot(cond.astype(BF16), w_ref[0].astype(BF16)) + b_ref[0]


def _ada_mods(c, ada_w, ada_b):
    depth, d, n6 = ada_w.shape
    bsz = c.shape[0]
    rows = -(-bsz // SUBLANES) * SUBLANES
    c_pad = jnp.zeros((rows, d), F32).at[:bsz].set(c)
    tn = 1536 if n6 % 1536 == 0 else n6
    out = pl.pallas_call(
        _ada_kernel,
        out_shape=jax.ShapeDtypeStruct((depth, rows, n6), F32),
        grid=(depth, n6 // tn),
        in_specs=[
            pl.BlockSpec((rows, d), lambda l, j: (0, 0)),
            pl.BlockSpec((1, d, tn), lambda l, j: (l, 0, j)),
            pl.BlockSpec((1, 1, tn), lambda l, j: (l, 0, j)),
        ],
        out_specs=pl.BlockSpec((1, rows, tn), lambda l, j: (l, 0, j)),
        compiler_params=_cparams(("parallel", "parallel")),
        name="ada_mods",
    )(c_pad, ada_w, ada_b.reshape(depth, 1, n6))
    return out[:, :bsz].reshape(depth, bsz, 6, d)


def _linear_mod_kernel(x_ref, mod_ref, w_ref, o_ref, h_ref, *, shift_row, scale_row):
    @pl.when(pl.program_id(1) == 0)
    def _():
        m = mod_ref[0]
        h = x_ref[...] * (1.0 + m[scale_row:scale_row + 1, :]) + m[shift_row:shift_row + 1, :]
        h_ref[...] = h.astype(BF16)

    o_ref[...] = _dot(h_ref[...], w_ref[...]).astype(o_ref.dtype)


def _linear_mod(x, mod, w, *, seq, shift_row, scale_row, out_dtype, tm, tn, name):
    t, d = x.shape
    n = w.shape[1]
    tm = min(tm, seq)
    tn = min(tn, n)
    assert t % tm == 0 and seq % tm == 0 and n % tn == 0
    return pl.pallas_call(
        functools.partial(_linear_mod_kernel, shift_row=shift_row, scale_row=scale_row),
        out_shape=jax.ShapeDtypeStruct((t, n), out_dtype),
        grid=(t // tm, n // tn),
        in_specs=[
            pl.BlockSpec((tm, d), lambda i, j: (i, 0)),
            pl.BlockSpec((1, 6, d), lambda i, j: ((i * tm) // seq, 0, 0)),
            pl.BlockSpec((d, tn), lambda i, j: (0, j)),
        ],
        out_specs=pl.BlockSpec((tm, tn), lambda i, j: (i, j)),
        scratch_shapes=[pltpu.VMEM((tm, d), BF16)],
        compiler_params=_cparams(("parallel", "arbitrary")),
        name=name,
    )(x, mod, w)


def _deepnorm_ln(x, y, gate, g, b, alpha):
    v = alpha * x + (1.0 + gate) * y
    mu = jnp.mean(v, axis=-1, keepdims=True)
    dv = v - mu
    var = jnp.mean(dv * dv, axis=-1, keepdims=True)
    return dv * lax.rsqrt(var + LN_EPS) * g + b


def _outproj_ln_kernel(y_ref, w_ref, x_ref, mod_ref, g_ref, b_ref, o_ref, acc_ref, *, gate_row, nk, alpha):
    k = pl.program_id(1)
    part = _dot(y_ref[...], w_ref[...])

    def finish(acc):
        gate = mod_ref[0][gate_row:gate_row + 1, :]
        o_ref[...] = _deepnorm_ln(x_ref[...], acc, gate, g_ref[...], b_ref[...], alpha)

    if nk == 1:
        finish(part)
    else:
        @pl.when(k == 0)
        def _():
            acc_ref[...] = part

        @pl.when(jnp.logical_and(k > 0, k < nk - 1))
        def _():
            acc_ref[...] += part

        @pl.when(k == nk - 1)
        def _():
            finish(acc_ref[...] + part)


def _outproj_ln(y, w, x, mod, ln_g, ln_b, *, seq, gate_row, alpha, tm, tk, name):
    t, kdim = y.shape
    d = w.shape[1]
    tm = min(tm, seq)
    tk = min(tk, kdim)
    assert t % tm == 0 and seq % tm == 0 and kdim % tk == 0
    nk = kdim // tk
    return pl.pallas_call(
        functools.partial(_outproj_ln_kernel, gate_row=gate_row, nk=nk, alpha=alpha),
        out_shape=jax.ShapeDtypeStruct((t, d), F32),
        grid=(t // tm, nk),
        in_specs=[
            pl.BlockSpec((tm, tk), lambda i, k: (i, k)),
            pl.BlockSpec((tk, d), lambda i, k: (k, 0)),
            pl.BlockSpec((tm, d), lambda i, k: (i, 0)),
            pl.BlockSpec((1, 6, d), lambda i, k: ((i * tm) // seq, 0, 0)),
            pl.BlockSpec((1, d), lambda i, k: (0, 0)),
            pl.BlockSpec((1, d), lambda i, k: (0, 0)),
        ],
        out_specs=pl.BlockSpec((tm, d), lambda i, k: (i, 0)),
        scratch_shapes=[pltpu.VMEM((tm, d), F32)],
        compiler_params=_cparams(("parallel", "arbitrary")),
        name=name,
    )(y, w, x, mod, ln_g.reshape(1, d), ln_b.reshape(1, d))


def _mlstm_kernel(q_ref, k_ref, v_ref, o_ref, aux_ref, bias_ref, nw_ref, y_ref,
                  c_ref, n_ref, m_ref, *, heads, dqk, dv, chunk):
    @pl.when(pl.program_id(1) == 0)
    def _():
        c_ref[...] = jnp.zeros_like(c_ref)
        n_ref[...] = jnp.zeros_like(n_ref)
        m_ref[...] = jnp.zeros_like(m_ref)

    scale = dqk ** -0.5
    g = MLSTM_GATE_CAP * jnp.tanh((aux_ref[...] + bias_ref[...]) / MLSTM_GATE_CAP)
    lane = lax.broadcasted_iota(I32, g.shape, 1)
    logsig = _log_sigmoid(g)
    is_f = jnp.logical_and(lane >= heads, lane < 2 * heads)
    tri = _lower_tri(chunk)
    bcum = _dot_hi(tri.astype(F32), jnp.where(is_f, logsig, 0.0))
    g_t = g.T
    bcum_t = bcum.T

    for h in range(heads):
        ig_col = g[:, h:h + 1]
        ig_row = g_t[h:h + 1, :]
        b_col = bcum[:, heads + h:heads + h + 1]
        b_row = bcum_t[heads + h:heads + h + 1, :]
        m_prev = m_ref[h][0:1, 0:1]

        log_d = jnp.where(tri, b_col - b_row + ig_row, -jnp.inf)
        m_inter = b_col + m_prev
        m_t = jnp.maximum(jnp.max(log_d, axis=1, keepdims=True), m_inter)
        dmat = jnp.exp(log_d - m_t)

        q = q_ref[:, h * dqk:(h + 1) * dqk]
        k = k_ref[:, h * dqk:(h + 1) * dqk]
        v = v_ref[:, h * dv:(h + 1) * dv]
        scores = _dot_nt(q, k) * scale * dmat
        inter = jnp.exp(m_inter - m_t)
        c_st = c_ref[h]
        n_st = n_ref[h][0:1, :]
        num = _dot(scores.astype(BF16), v) + inter * (_dot_nt(q, c_st.astype(BF16)) * scale)
        qn = jnp.sum(q.astype(F32) * n_st, axis=1, keepdims=True) * scale
        den = jnp.sum(scores, axis=1, keepdims=True) + inter * qn
        hout = num * (1.0 / jnp.maximum(jnp.abs(den), jnp.exp(-m_t)))

        b_last = b_col[chunk - 1:chunk, :]
        log_w = b_last - b_col + ig_col
        m_new = jnp.maximum(b_last + m_prev, jnp.max(log_w, axis=0, keepdims=True))
        w = jnp.exp(log_w - m_new)
        decay = jnp.exp(b_last + m_prev - m_new)
        vw = (v.astype(F32) * w).astype(BF16)
        c_ref[h] = decay * c_st + _dot_tn(vw, k)
        n_new = decay * n_st + jnp.sum(k.astype(F32) * w, axis=0, keepdims=True)
        n_ref[h] = jnp.broadcast_to(n_new, n_ref.shape[1:])
        m_ref[h] = jnp.broadcast_to(m_new, m_ref.shape[1:])

        ms = jnp.mean(hout * hout, axis=1, keepdims=True)
        hn = hout * lax.rsqrt(ms + RMS_EPS) * nw_ref[:, h * dv:(h + 1) * dv]
        og = o_ref[:, h * dv:(h + 1) * dv].astype(F32)
        y_ref[:, h * dv:(h + 1) * dv] = (hn * _sigmoid(og)).astype(y_ref.dtype)


def _mlstm_recurrence(main, aux, bias, norm_w, *, bsz, seq, heads, dqk, dv):
    t = main.shape[0]
    nqk, nv = heads * dqk, heads * dv
    chunk = min(MLSTM_CHUNK, seq)
    nc = seq // chunk
    assert nv == 2 * nqk and seq % chunk == 0
    row = lambda b, c: b * nc + c
    return pl.pallas_call(
        functools.partial(_mlstm_kernel, heads=heads, dqk=dqk, dv=dv, chunk=chunk),
        out_shape=jax.ShapeDtypeStruct((t, nv), BF16),
        grid=(bsz, nc),
        in_specs=[
            pl.BlockSpec((chunk, nqk), lambda b, c: (row(b, c), 0)),
            pl.BlockSpec((chunk, nqk), lambda b, c: (row(b, c), 1)),
            pl.BlockSpec((chunk, nv), lambda b, c: (row(b, c), 1)),
            pl.BlockSpec((chunk, nv), lambda b, c: (row(b, c), 2)),
            pl.BlockSpec((chunk, LANES), lambda b, c: (row(b, c), 0)),
            pl.BlockSpec((1, LANES), lambda b, c: (0, 0)),
            pl.BlockSpec((1, nv), lambda b, c: (0, 0)),
        ],
        out_specs=pl.BlockSpec((chunk, nv), lambda b, c: (row(b, c), 0)),
        scratch_shapes=[
            pltpu.VMEM((heads, dv, dqk), F32),
            pltpu.VMEM((heads, SUBLANES, dqk), F32),
            pltpu.VMEM((heads, SUBLANES, LANES), F32),
        ],
        compiler_params=_cparams(("parallel", "arbitrary")),
        name="mlstm_recurrence",
    )(main, main, main, main, aux, bias, norm_w.reshape(1, nv))


def _mlstm_mixer(x, mod, w_in, b_gates, norm_w, w_out, ln_g, ln_b, *, bsz, seq, alpha):
    d = x.shape[1]
    heads = b_gates.shape[0] // 2
    nv = w_out.shape[0]
    nqk = (w_in.shape[1] - 2 * nv - 2 * heads) // 2
    n_main = 2 * nqk + 2 * nv
    w_main = w_in[:, :n_main].astype(BF16)
    w_aux = jnp.zeros((d, LANES), BF16).at[:, :2 * heads].set(w_in[:, n_main:].astype(BF16))
    bias = jnp.zeros((1, LANES), F32).at[0, :2 * heads].set(b_gates)
    main = _linear_mod(x, mod, w_main, seq=seq, shift_row=0, scale_row=1, out_dtype=BF16,
                       tm=1024, tn=512, name="mlstm_in_main")
    aux = _linear_mod(x, mod, w_aux, seq=seq, shift_row=0, scale_row=1, out_dtype=F32,
                      tm=1024, tn=LANES, name="mlstm_in_gates")
    y = _mlstm_recurrence(main, aux, bias, norm_w, bsz=bsz, seq=seq, heads=heads,
                          dqk=nqk // heads, dv=nv // heads)
    return _outproj_ln(y, w_out.astype(BF16), x, mod, ln_g, ln_b, seq=seq, gate_row=2, alpha=alpha,
                       tm=512, tk=2048, name="mlstm_out_ln")


def _hgrn_kernel(main_ref, lbp_ref, nw_ref, y_ref, st_ref, *, layer, chunk, sub, nchunks, hps):
    dk = HGRN_HEAD_DIM

    @pl.when(pl.program_id(2) == 0)
    def _():
        st_ref[...] = jnp.zeros_like(st_ref)

    p = lbp_ref[...]
    e = jnp.exp(p - jnp.max(p, axis=0, keepdims=True))
    sm = e / jnp.sum(e, axis=0, keepdims=True)
    lb_all = jnp.zeros((1, hps * dk), F32)
    for r in range(1, layer + 1):
        lb_all = lb_all + sm[r:r + 1, :]
    tri = _lower_tri(chunk).astype(F32)
    nsub = chunk // sub
    sub_row = lax.broadcasted_iota(I32, (sub, dk), 0)
    ones_sq = jnp.ones((dk, dk), BF16)
    nw = nw_ref[...]

    def head_chunk(r0, hh):
        lb = lb_all[:, hh * dk:(hh + 1) * dk]
        log_lb = jnp.log(lb)
        log_1m = jnp.log1p(-lb)
        one_m = 1.0 - lb
        c0 = hh * 4 * dk
        blk = main_ref[pl.ds(r0, chunk), c0:c0 + 4 * dk]
        qraw = blk[:, 0:dk].astype(F32)
        fraw = blk[:, dk:2 * dk].astype(F32)
        iv_b = blk[:, 2 * dk:3 * dk]
        iv = iv_b.astype(F32)
        graw = blk[:, 3 * dk:4 * dk].astype(F32)

        q = _silu(qraw)
        b2 = log_1m + _log_sigmoid(fraw)
        logf = jnp.maximum(log_lb, b2) + jnp.log1p(jnp.exp(-jnp.abs(log_lb - b2)))
        k = one_m * _sigmoid(-fraw)
        gcum = _dot_hi(tri, logf)

        st = st_ref[hh]
        o_inter = _dot_nt((q * jnp.exp(gcum)).astype(BF16), st.astype(BF16))

        outs = []
        for bi in range(nsub):
            lo = bi * sub
            g_i = gcum[lo:lo + sub]
            q_i = q[lo:lo + sub]
            k_i = k[lo:lo + sub]
            i_i = iv[lo:lo + sub]
            acc = o_inter[lo:lo + sub]
            if bi > 0:
                ref_row = gcum[lo - 1:lo]
                qt = (q_i * jnp.exp(g_i - ref_row)).astype(BF16)
                kt = (k[0:lo] * jnp.exp(ref_row - gcum[0:lo])).astype(BF16)
                a = _dot_nt(qt, kt)
                acc = acc + _dot(a.astype(BF16), iv_b[0:lo])
            pws = []
            for s in range(sub):
                arg = jnp.where(sub_row >= s, g_i - g_i[s:s + 1], -jnp.inf)
                pws.append((q_i * jnp.exp(arg) * k_i[s:s + 1]).astype(BF16))
            rs = _dot(jnp.concatenate(pws, axis=0), ones_sq)
            for s in range(sub):
                acc = acc + rs[s * sub:(s + 1) * sub] * i_i[s:s + 1]
            outs.append(acc)
        o = jnp.concatenate(outs, axis=0)

        g_last = gcum[chunk - 1:chunk]
        kdec = (k * jnp.exp(g_last - gcum)).astype(BF16)
        st_ref[hh] = st * jnp.exp(g_last) + _dot_tn(iv_b, kdec)

        ms = jnp.mean(o * o, axis=1, keepdims=True)
        y = o * lax.rsqrt(ms + RMS_EPS) * nw * _silu(graw)
        y_ref[pl.ds(r0, chunk), hh * dk:(hh + 1) * dk] = y.astype(y_ref.dtype)

    def body(ci, carry):
        r0 = pl.multiple_of(ci * chunk, chunk)
        for hh in range(hps):
            head_chunk(r0, hh)
        return carry

    lax.fori_loop(0, nchunks, body, 0)


def _hgrn_recurrence(main, lb_param, norm_w, *, bsz, seq, heads, layer):
    t = main.shape[0]
    dk = HGRN_HEAD_DIM
    depth = lb_param.shape[0]
    rows = min(HGRN_ROWS_PER_STEP, seq)
    chunk = min(HGRN_CHUNK, rows)
    hps = HGRN_HEADS_PER_STEP
    assert seq % rows == 0 and rows % chunk == 0 and chunk % HGRN_SUB == 0 and heads % hps == 0
    nr = seq // rows
    return pl.pallas_call(
        functools.partial(_hgrn_kernel, layer=layer, chunk=chunk, sub=HGRN_SUB, nchunks=rows // chunk, hps=hps),
        out_shape=jax.ShapeDtypeStruct((t, heads * dk), BF16),
        grid=(bsz, heads // hps, nr),
        in_specs=[
            pl.BlockSpec((rows, hps * 4 * dk), lambda b, h, c: (b * nr + c, h)),
            pl.BlockSpec((depth, hps * dk), lambda b, h, c: (0, h)),
            pl.BlockSpec((1, dk), lambda b, h, c: (0, 0)),
        ],
        out_specs=pl.BlockSpec((rows, hps * dk), lambda b, h, c: (b * nr + c, h)),
        scratch_shapes=[pltpu.VMEM((hps, dk, dk), F32)],
        compiler_params=_cparams(("parallel", "parallel", "arbitrary")),
        name="hgrn_recurrence",
    )(main, lb_param, norm_w.reshape(1, dk))


def _hgrn_mixer(x, mod, w_in, lb_param, norm_w, w_out, ln_g, ln_b, *, bsz, seq, alpha, layer):
    d = x.shape[1]
    dk = HGRN_HEAD_DIM
    heads = d // dk
    w_perm = w_in.reshape(d, 4, heads, dk).transpose(0, 2, 1, 3).reshape(d, 4 * d).astype(BF16)
    main = _linear_mod(x, mod, w_perm, seq=seq, shift_row=0, scale_row=1, out_dtype=BF16,
                       tm=1024, tn=512, name="hgrn_in")
    y = _hgrn_recurrence(main, lb_param, norm_w, bsz=bsz, seq=seq, heads=heads, layer=layer)
    return _outproj_ln(y, w_out.astype(BF16), x, mod, ln_g, ln_b, seq=seq, gate_row=2, alpha=alpha,
                       tm=512, tk=2048, name="hgrn_out_ln")


def _conv_kernel(x_ref, halo_ref, w_ref, b_ref, o_ref, *, taps, steps_per_seq):
    first = (pl.program_id(0) % steps_per_seq) == 0
    x = x_ref[...].astype(F32)
    halo = jnp.where(first, 0.0, halo_ref[...].astype(F32))
    xc = jnp.concatenate([halo, x], axis=0)
    hr, tm = halo.shape[0], x.shape[0]
    acc = jnp.broadcast_to(b_ref[...], x.shape)
    for kk in range(taps):
        shift = taps - 1 - kk
        xs = xc if shift == 0 else pltpu.roll(xc, shift, 0)
        acc = acc + w_ref[kk:kk + 1, :] * xs[hr:hr + tm]
    o_ref[...] = _silu(acc).astype(o_ref.dtype)


def _ssd_conv(main, conv_w, conv_b, *, seq, col0):
    t = main.shape[0]
    taps, cdim = conv_w.shape
    tm = min(512, seq)
    tc = 512
    halo = 2 * SUBLANES
    assert seq % tm == 0 and cdim % tc == 0 and col0 % tc == 0 and taps - 1 <= halo and tm % halo == 0
    cb0 = col0 // tc
    hb = tm // halo
    return pl.pallas_call(
        functools.partial(_conv_kernel, taps=taps, steps_per_seq=seq // tm),
        out_shape=jax.ShapeDtypeStruct((t, cdim), BF16),
        grid=(t // tm, cdim // tc),
        in_specs=[
            pl.BlockSpec((tm, tc), lambda i, j: (i, cb0 + j)),
            pl.BlockSpec((halo, tc), lambda i, j: (jnp.maximum(i * hb - 1, 0), cb0 + j)),
            pl.BlockSpec((taps, tc), lambda i, j: (0, j)),
            pl.BlockSpec((1, tc), lambda i, j: (0, j)),
        ],
        out_specs=pl.BlockSpec((tm, tc), lambda i, j: (i, j)),
        compiler_params=_cparams(("parallel", "parallel")),
        name="ssd_conv",
    )(main, main, conv_w, conv_b.reshape(1, cdim))


def _ssd_kernel(xs_ref, b_ref, c_ref, z_ref, aux_ref, dtb_ref, alog_ref, dsk_ref, nw_ref, y_ref,
                state_ref, dt_ref, cum_ref, dt_t_ref, cum_t_ref, *, chunk, hpg, hd):
    cidx = pl.program_id(1)
    g = pl.program_id(2)

    @pl.when(jnp.logical_and(cidx == 0, g == 0))
    def _():
        state_ref[...] = jnp.zeros_like(state_ref)

    tri = _lower_tri(chunk)

    @pl.when(g == 0)
    def _():
        dt = _softplus(aux_ref[...] + dtb_ref[...])
        la = dt * (-jnp.exp(alog_ref[...]))
        cum = _dot_hi(tri.astype(F32), la)
        dt_ref[...] = dt
        cum_ref[...] = cum
        dt_t_ref[...] = dt.T
        cum_t_ref[...] = cum.T

    dt_nat = dt_ref[...]
    cum_nat = cum_ref[...]
    g8 = pl.multiple_of(g * hpg, hpg)
    cum_t = cum_t_ref[pl.ds(g8, hpg), :]
    dt_t = dt_t_ref[pl.ds(g8, hpg), :]
    lane = lax.broadcasted_iota(I32, cum_nat.shape, 1)

    xs = xs_ref[...].astype(F32)
    x_t = xs.T
    bm = b_ref[...]
    cm = c_ref[...]
    cb_t = _dot_nt(bm, cm)
    st = state_ref[g]
    inter_t = _dot_nt(st.astype(BF16), cm)
    last = cum_t[:, chunk - 1:chunk]
    w_t = jnp.exp(last - cum_t) * dt_t
    ecum_t = jnp.exp(cum_t)
    s_le_t = jnp.logical_not(tri) | (lax.broadcasted_iota(I32, (chunk, chunk), 0)
                                     == lax.broadcasted_iota(I32, (chunk, chunk), 1))

    y_parts = []
    xw_parts = []
    st_parts = []
    for j in range(hpg):
        col = g8 + j
        sel = lane == col
        cum_col = jnp.sum(jnp.where(sel, cum_nat, 0.0), axis=1, keepdims=True)
        dt_col = jnp.sum(jnp.where(sel, dt_nat, 0.0), axis=1, keepdims=True)
        cum_row = cum_t[j:j + 1, :]
        dec_t = jnp.exp(jnp.where(s_le_t, cum_row - cum_col, -jnp.inf))
        m_t = (cb_t * dec_t * dt_col).astype(BF16)
        x_j = x_t[j * hd:(j + 1) * hd, :]
        y_j = _dot(x_j.astype(BF16), m_t) + inter_t[j * hd:(j + 1) * hd, :] * ecum_t[j:j + 1, :]
        y_parts.append(y_j)
        xw_parts.append((x_j * w_t[j:j + 1, :]).astype(BF16))
        st_parts.append(st[j * hd:(j + 1) * hd, :] * jnp.exp(last[j:j + 1, :]))
    y_t = jnp.concatenate(y_parts, axis=0)
    xw_t = jnp.concatenate(xw_parts, axis=0)
    state_ref[g] = jnp.concatenate(st_parts, axis=0) + _dot(xw_t, bm)

    y = y_t.T + dsk_ref[...] * xs
    y = y * _silu(z_ref[...].astype(F32))
    ms = jnp.mean(y * y, axis=1, keepdims=True)
    y_ref[...] = (y * lax.rsqrt(ms + RMS_EPS) * nw_ref[...]).astype(y_ref.dtype)


def _ssd_recurrence(main, conv, aux, dt_bias, a_log, d_skip, norm_w, *, bsz, seq, d_inner, nstate):
    t = main.shape[0]
    groups = SSM_GROUPS
    gw = d_inner // groups
    hpg = gw // SSM_HEADDIM
    nheads = groups * hpg
    chunk = min(SSD_CHUNK, seq)
    assert seq % chunk == 0 and nstate == LANES and chunk == LANES and nheads <= LANES
    nc = seq // chunk
    row = lambda b, c, g: b * nc + c
    pad = lambda v: jnp.zeros((1, LANES), F32).at[0, :nheads].set(v)
    dsk = jnp.repeat(d_skip, SSM_HEADDIM).reshape(1, d_inner)
    b0 = d_inner // nstate
    return pl.pallas_call(
        functools.partial(_ssd_kernel, chunk=chunk, hpg=hpg, hd=SSM_HEADDIM),
        out_shape=jax.ShapeDtypeStruct((t, d_inner), BF16),
        grid=(bsz, nc, groups),
        in_specs=[
            pl.BlockSpec((chunk, gw), lambda b, c, g: (row(b, c, g), g)),
            pl.BlockSpec((chunk, nstate), lambda b, c, g: (row(b, c, g), b0 + g)),
            pl.BlockSpec((chunk, nstate), lambda b, c, g: (row(b, c, g), b0 + groups + g)),
            pl.BlockSpec((chunk, gw), lambda b, c, g: (row(b, c, g), g)),
            pl.BlockSpec((chunk, LANES), lambda b, c, g: (row(b, c, g), 0)),
            pl.BlockSpec((1, LANES), lambda b, c, g: (0, 0)),
            pl.BlockSpec((1, LANES), lambda b, c, g: (0, 0)),
            pl.BlockSpec((1, gw), lambda b, c, g: (0, g)),
            pl.BlockSpec((1, gw), lambda b, c, g: (0, g)),
        ],
        out_specs=pl.BlockSpec((chunk, gw), lambda b, c, g: (row(b, c, g), g)),
        scratch_shapes=[
            pltpu.VMEM((groups, gw, nstate), F32),
            pltpu.VMEM((chunk, LANES), F32),
            pltpu.VMEM((chunk, LANES), F32),
            pltpu.VMEM((LANES, chunk), F32),
            pltpu.VMEM((LANES, chunk), F32),
        ],
        compiler_params=_cparams(("parallel", "arbitrary", "arbitrary")),
        name="ssd_recurrence",
    )(conv, conv, conv, main, aux, pad(dt_bias), pad(a_log), dsk, norm_w.reshape(1, d_inner))


def _ssd_mixer(x, mod, w_in, conv_w, conv_b, dt_bias, a_log, d_skip, norm_w, w_out, ln_g, ln_b,
               *, bsz, seq, alpha):
    d = x.shape[1]
    d_inner = w_out.shape[0]
    cdim = conv_w.shape[-1]
    nheads = dt_bias.shape[0]
    nstate = (cdim - d_inner) // (2 * SSM_GROUPS)
    n_main = d_inner + cdim
    w_main = w_in[:, :n_main].astype(BF16)
    w_aux = jnp.zeros((d, LANES), BF16).at[:, :nheads].set(w_in[:, n_main:].astype(BF16))
    main = _linear_mod(x, mod, w_main, seq=seq, shift_row=0, scale_row=1, out_dtype=BF16,
                       tm=1024, tn=512, name="ssd_in_main")
    aux = _linear_mod(x, mod, w_aux, seq=seq, shift_row=0, scale_row=1, out_dtype=F32,
                      tm=1024, tn=LANES, name="ssd_in_dt")
    conv = _ssd_conv(main, conv_w.reshape(conv_w.shape[0], cdim), conv_b, seq=seq, col0=d_inner)
    y = _ssd_recurrence(main, conv, aux, dt_bias, a_log, d_skip, norm_w, bsz=bsz, seq=seq,
                        d_inner=d_inner, nstate=nstate)
    return _outproj_ln(y, w_out.astype(BF16), x, mod, ln_g, ln_b, seq=seq, gate_row=2, alpha=alpha,
                       tm=512, tk=2048, name="ssd_out_ln")


def _pack_pairs(lo, hi):
    lo_b = lax.bitcast_convert_type(lo.astype(BF16).astype(F32), U32) >> 16
    hi_b = lax.bitcast_convert_type(hi.astype(BF16).astype(F32), U32) & jnp.uint32(HI_HALF_MASK)
    return lo_b | hi_b


def _unpack_pairs(w):
    lo = lax.bitcast_convert_type(w << 16, F32)
    hi = lax.bitcast_convert_type(w & jnp.uint32(HI_HALF_MASK), F32)
    return lo, hi


def _store_packed_rows(dst_ref, v):
    rows, d = v.shape
    half = d // 2
    pr = half // LANES
    w = _pack_pairs(v[:, :half], v[:, half:])
    for c in range(pr):
        dst_ref[pl.ds(c, rows, stride=pr), :] = w[:, c * LANES:(c + 1) * LANES]


def _load_packed_rows(src_ref, rows, pr, base=0):
    return [_unpack_pairs(src_ref[pl.ds(base + c, rows, stride=pr), :]) for c in range(pr)]


def _router_kernel(x_ref, mod_ref, wr_ref, br_ref, h_ref, ti_ref, tg_ref, *, n_real):
    i = pl.program_id(0)

    @pl.when(i < n_real)
    def _():
        m = mod_ref[0]
        h = x_ref[...] * (1.0 + m[4:5, :]) + m[3:4, :]
        _store_packed_rows(h_ref, h)
        logits = _dot_hi(h, wr_ref[...]) + br_ref[...]
        lane = lax.broadcasted_iota(I32, logits.shape, 1)
        lane_f = lane.astype(F32)
        vals = []
        ti = jnp.zeros(logits.shape, I32)
        for kk in range(TOP_K):
            mx = jnp.max(logits, axis=1, keepdims=True)
            idx = jnp.min(jnp.where(logits == mx, lane_f, float(LANES)), axis=1, keepdims=True).astype(I32)
            vals.append(mx)
            ti = jnp.where(lane == kk, idx, ti)
            logits = jnp.where(lane == idx, -jnp.inf, logits)
        es = [jnp.exp(v - vals[0]) for v in vals]
        tot = es[0]
        for e in es[1:]:
            tot = tot + e
        tg = jnp.zeros(ti.shape, F32)
        for kk in range(TOP_K):
            tg = jnp.where(lane == kk, es[kk] / tot, tg)
        ti_ref[...] = ti
        tg_ref[...] = tg

    @pl.when(i >= n_real)
    def _():
        h_ref[...] = jnp.zeros_like(h_ref)
        ti_ref[...] = jnp.zeros_like(ti_ref)
        tg_ref[...] = jnp.zeros_like(tg_ref)


def _router(x, mod, w_router, b_router, *, seq, tm):
    t, d = x.shape
    ne = w_router.shape[1]
    tm = min(tm, seq)
    n_real = t // tm
    wr = jnp.zeros((d, LANES), F32).at[:, :ne].set(w_router)
    br = jnp.full((1, LANES), -1e30, F32).at[0, :ne].set(b_router)
    clamp = lambda i: jnp.minimum(i, n_real - 1)
    pr = d // (2 * LANES)
    return pl.pallas_call(
        functools.partial(_router_kernel, n_real=n_real),
        out_shape=(
            jax.ShapeDtypeStruct(((t + tm) * pr, LANES), U32),
            jax.ShapeDtypeStruct((t + tm, LANES), I32),
            jax.ShapeDtypeStruct((t + tm, LANES), F32),
        ),
        grid=(n_real + 1,),
        in_specs=[
            pl.BlockSpec((tm, d), lambda i: (clamp(i), 0)),
            pl.BlockSpec((1, 6, d), lambda i: ((clamp(i) * tm) // seq, 0, 0)),
            pl.BlockSpec((d, LANES), lambda i: (0, 0)),
            pl.BlockSpec((1, LANES), lambda i: (0, 0)),
        ],
        out_specs=(
            pl.BlockSpec((tm * pr, LANES), lambda i: (i, 0)),
            pl.BlockSpec((tm, LANES), lambda i: (i, 0)),
            pl.BlockSpec((tm, LANES), lambda i: (i, 0)),
        ),
        compiler_params=_cparams(("parallel",)),
        name="moe_router",
    )(x, mod, wr, br)


def _w1_prep_kernel(w_ref, p_ref, o_ref):
    o_ref[0] = _dot(w_ref[0].astype(BF16), p_ref[...]).astype(BF16)


def _prep_w1(w1):
    ne, d, f2 = w1.shape
    f = f2 // 2
    j = jnp.arange(f2, dtype=I32)
    src_col = jnp.where(j < f, 2 * j, 2 * (j - f) + 1)
    sel = (jnp.arange(f2, dtype=I32)[:, None] == src_col[None, :]).astype(BF16)
    tk = min(512, d)
    return pl.pallas_call(
        _w1_prep_kernel,
        out_shape=jax.ShapeDtypeStruct((ne, d, f2), BF16),
        grid=(ne, d // tk),
        in_specs=[
            pl.BlockSpec((1, tk, f2), lambda e, k: (e, k, 0)),
            pl.BlockSpec((f2, f2), lambda e, k: (0, 0)),
        ],
        out_specs=pl.BlockSpec((1, tk, f2), lambda e, k: (e, k, 0)),
        compiler_params=_cparams(("parallel", "parallel")),
        name="moe_w1_prep",
    )(w1, sel)


def _gather_rows_start(src_hbm, idx_ref, dst_ref, sem, *, count, pr):
    def body(r2, carry):
        for p in range(2):
            r = r2 * 2 + p
            src = src_hbm.at[pl.ds(pl.multiple_of(idx_ref[0, r], pr), pr)]
            dst = dst_ref.at[pl.ds(pl.multiple_of(r * pr, pr), pr)]
            pltpu.make_async_copy(src, dst, sem).start(priority=p)
        return carry

    lax.fori_loop(0, count // 2, body, 0, unroll=4)


def _gather_rows_wait(src_hbm, dst_ref, sem):
    pltpu.make_async_copy(src_hbm.at[pl.ds(0, dst_ref.shape[0])], dst_ref, sem).wait()


def _expert_kernel(be_ref, rows_ref, src_cur_ref, src_nxt_ref, h_hbm, w1_ref, b1_ref, w2_ref, b2_ref, y_ref,
                   xbuf, xs_ref, gsem, *, blk, nb, f):
    i = pl.program_id(0)
    slot = i % 2
    half = xs_ref.shape[1] // 2
    pr = half // LANES

    @pl.when(jnp.logical_and(i == 0, rows_ref[0] > 0))
    def _():
        _gather_rows_start(h_hbm, src_cur_ref, xbuf.at[0], gsem.at[0], count=blk, pr=pr)

    nxt = jnp.minimum(i + 1, nb - 1)

    @pl.when(jnp.logical_and(i + 1 < nb, rows_ref[nxt] > 0))
    def _():
        _gather_rows_start(h_hbm, src_nxt_ref, xbuf.at[1 - slot], gsem.at[1 - slot], count=blk, pr=pr)

    @pl.when(rows_ref[i] > 0)
    def _():
        _gather_rows_wait(h_hbm, xbuf.at[slot], gsem.at[slot])
        for c, (lo, hi) in enumerate(_load_packed_rows(xbuf.at[slot], blk, pr)):
            xs_ref[:, c * LANES:(c + 1) * LANES] = lo.astype(BF16)
            xs_ref[:, half + c * LANES:half + (c + 1) * LANES] = hi.astype(BF16)
        hh = _dot(xs_ref[...], w1_ref[0]) + b1_ref[0]
        glu = jnp.minimum(hh[:, :f], SWIGLU_LIMIT)
        lin = jnp.clip(hh[:, f:], -SWIGLU_LIMIT, SWIGLU_LIMIT)
        act = glu * _sigmoid(SWIGLU_ALPHA * glu) * (lin + 1.0)
        _store_packed_rows(y_ref, _dot(act.astype(BF16), w2_ref[0]) + b2_ref[0])

    @pl.when(rows_ref[i] == 0)
    def _():
        y_ref[...] = jnp.zeros_like(y_ref)


def _experts(h_rows, plan, w1p, b1p, w2b, b2r, *, e0, blk):
    block_e, block_rows, src_rows = plan
    nb = block_e.shape[0]
    f2 = w1p.shape[2]
    f = f2 // 2
    d = w2b.shape[2]
    pr = d // (2 * LANES)
    wmap = lambda i, be, br: (e0 + be[i], 0, 0)
    smem_blk = lambda fn: pl.BlockSpec((None, 1, blk), fn, memory_space=pltpu.SMEM)
    grid_spec = pltpu.PrefetchScalarGridSpec(
        num_scalar_prefetch=2,
        grid=(nb,),
        in_specs=[
            smem_blk(lambda i, be, br: (i, 0, 0)),
            smem_blk(lambda i, be, br: (jnp.minimum(i + 1, nb - 1), 0, 0)),
            pl.BlockSpec(memory_space=pl.ANY),
            pl.BlockSpec((1, d, f2), wmap),
            pl.BlockSpec((1, 1, f2), wmap),
            pl.BlockSpec((1, f, d), wmap),
            pl.BlockSpec((1, 1, d), wmap),
        ],
        out_specs=pl.BlockSpec((blk * pr, LANES), lambda i, be, br: (i, 0)),
        scratch_shapes=[
            pltpu.VMEM((2, blk * pr, LANES), U32),
            pltpu.VMEM((blk, d), BF16),
            pltpu.SemaphoreType.DMA((2,)),
        ],
    )
    src3 = src_rows.reshape(nb, 1, blk)
    return pl.pallas_call(
        functools.partial(_expert_kernel, blk=blk, nb=nb, f=f),
        out_shape=jax.ShapeDtypeStruct((nb * blk * pr, LANES), U32),
        grid_spec=grid_spec,
        compiler_params=_cparams(("arbitrary",)),
        name="moe_experts",
    )(block_e, block_rows, src3, src3, h_rows, w1p, b1p, w2b, b2r)


def _combine_ln_kernel(idx_cur_ref, idx_nxt_ref, y_hbm, tg_ref, x_ref, mod_ref, g_ref, b_ref, o_ref,
                       ybuf, sem, *, alpha, nt):
    i = pl.program_id(0)
    slot = i % 2
    tm, d = x_ref.shape
    pr = d // (2 * LANES)
    count = TOP_K * tm

    @pl.when(i == 0)
    def _():
        _gather_rows_start(y_hbm, idx_cur_ref, ybuf.at[0], sem.at[0], count=count, pr=pr)

    @pl.when(i + 1 < nt)
    def _():
        _gather_rows_start(y_hbm, idx_nxt_ref, ybuf.at[1 - slot], sem.at[1 - slot], count=count, pr=pr)

    _gather_rows_wait(y_hbm, ybuf.at[slot], sem.at[slot])
    tg = tg_ref[...]
    los, his = [None] * pr, [None] * pr
    for kk in range(TOP_K):
        gk = tg[:, kk:kk + 1]
        for c, (lo, hi) in enumerate(_load_packed_rows(ybuf.at[slot], tm, pr, base=kk * tm * pr)):
            los[c] = gk * lo if kk == 0 else los[c] + gk * lo
            his[c] = gk * hi if kk == 0 else his[c] + gk * hi
    acc = jnp.concatenate(los + his, axis=1)
    gate = mod_ref[0][5:6, :]
    o_ref[...] = _deepnorm_ln(x_ref[...], acc, gate, g_ref[...], b_ref[...], alpha)


def _combine_ln(y_rows, slot_rows, tg, x, mod, ln_g, ln_b, *, seq, alpha, tm):
    t, d = x.shape
    nt = t // tm
    pr = d // (2 * LANES)
    idx3 = slot_rows.reshape(nt, 1, TOP_K * tm)
    smem_blk = lambda fn: pl.BlockSpec((None, 1, TOP_K * tm), fn, memory_space=pltpu.SMEM)
    return pl.pallas_call(
        functools.partial(_combine_ln_kernel, alpha=alpha, nt=nt),
        out_shape=jax.ShapeDtypeStruct((t, d), F32),
        grid=(nt,),
        in_specs=[
            smem_blk(lambda i: (i, 0, 0)),
            smem_blk(lambda i: (jnp.minimum(i + 1, nt - 1), 0, 0)),
            pl.BlockSpec(memory_space=pl.ANY),
            pl.BlockSpec((tm, LANES), lambda i: (i, 0)),
            pl.BlockSpec((tm, d), lambda i: (i, 0)),
            pl.BlockSpec((1, 6, d), lambda i: ((i * tm) // seq, 0, 0)),
            pl.BlockSpec((1, d), lambda i: (0, 0)),
            pl.BlockSpec((1, d), lambda i: (0, 0)),
        ],
        out_specs=pl.BlockSpec((tm, d), lambda i: (i, 0)),
        scratch_shapes=[
            pltpu.VMEM((2, TOP_K * tm * pr, LANES), U32),
            pltpu.SemaphoreType.DMA((2,)),
        ],
        compiler_params=_cparams(("arbitrary",)),
        name="moe_combine_ln",
    )(idx3, idx3, y_rows, tg, x, mod, ln_g.reshape(1, d), ln_b.reshape(1, d))


def _dispatch_plan(top_i, *, t, ne, blk, row_sublanes, tm):
    n = t * TOP_K
    n_blocks = n // blk + ne
    n_pad = n_blocks * blk
    e_flat = top_i[:t, :TOP_K].reshape(n)
    e_sorted, order = lax.sort((e_flat, jnp.arange(n, dtype=I32)), num_keys=1, is_stable=True)
    counts = jnp.sum((e_flat[:, None] == jnp.arange(ne, dtype=I32)[None, :]).astype(I32), axis=0)
    starts = jnp.cumsum(counts) - counts
    padded = (counts + blk - 1) // blk * blk
    pad_ends = jnp.cumsum(padded)
    pad_starts = pad_ends - padded
    blk_start = jnp.arange(n_blocks, dtype=I32) * blk
    block_e = jnp.minimum(jnp.searchsorted(pad_ends, blk_start, side="right"), ne - 1).astype(I32)
    block_valid = (blk_start < pad_ends[-1]).astype(I32)
    e_slot = jnp.repeat(block_e, blk)
    v_slot = jnp.repeat(block_valid, blk)
    slot = jnp.arange(n_pad, dtype=I32)
    r = slot - pad_starts[e_slot]
    valid = jnp.logical_and(r < counts[e_slot], v_slot == 1)
    p = jnp.clip(starts[e_slot] + r, 0, n - 1)
    src_row = jnp.where(valid, order[p] // TOP_K, t)
    block_rows = jnp.clip(pad_starts[block_e] + counts[block_e] - blk_start, 0, blk) * block_valid
    slot_sorted = pad_starts[e_sorted] + jnp.arange(n, dtype=I32) - starts[e_sorted]
    _, slot_of = lax.sort((order, slot_sorted.astype(I32)), num_keys=1)
    slot_rows = slot_of.reshape(t // tm, tm, TOP_K).transpose(0, 2, 1).reshape(t // tm, TOP_K * tm)
    plan = (block_e, block_rows.astype(I32), (src_row * row_sublanes).astype(I32))
    return plan, (slot_rows * row_sublanes).astype(I32)


def _moe_ffn(x, mod, w_router, b_router, w1p, b1p, w2b, b2r, ln_g, ln_b, *, seq, alpha, e0):
    t, d = x.shape
    ne = w_router.shape[1]
    tm_r = min(512, seq)
    tm_c = min(256, seq)
    h_rows, top_i, top_g = _router(x, mod, w_router, b_router, seq=seq, tm=tm_r)
    plan, slot_rows = _dispatch_plan(top_i, t=t, ne=ne, blk=MOE_BLOCK, row_sublanes=d // (2 * LANES), tm=tm_c)
    y_rows = _experts(h_rows, plan, w1p, b1p, w2b, b2r, e0=e0, blk=MOE_BLOCK)
    return _combine_ln(y_rows, slot_rows, top_g, x, mod, ln_g, ln_b, seq=seq, alpha=alpha, tm=tm_c)


def kernel(x, c, ada_w, ada_b, ln_g, ln_b, mlstm_w_in, mlstm_b_gates, mlstm_norm_w, mlstm_w_out, hgrn_w_in, hgrn_lb_param, hgrn_norm_w, hgrn_w_out, ssm_w_in, ssm_conv_w, ssm_conv_b, ssm_dt_bias, ssm_a_log, ssm_d, ssm_norm_w, ssm_w_out, moe_w_router, moe_b_router, moe_w1, moe_b1, moe_w2, moe_b2):
    bsz, seq, d = x.shape
    depth = ada_w.shape[0]
    alpha = (2 * depth) ** 0.25
    mods = _ada_mods(c, ada_w, ada_b)
    ne, f2 = moe_w1.shape[1], moe_w1.shape[3]
    w1p = _prep_w1(moe_w1.reshape(depth * ne, d, f2))
    b1p = jnp.concatenate([moe_b1[..., 0::2], moe_b1[..., 1::2]], axis=-1).reshape(depth * ne, 1, f2)
    w2b = moe_w2.astype(BF16).reshape(depth * ne, f2 // 2, d)
    b2r = moe_b2.reshape(depth * ne, 1, d)
    xf = x.reshape(bsz * seq, d)
    for i in range(depth):
        mod = mods[i]
        kind, j = i % 3, i // 3
        if kind == 0:
            xf = _mlstm_mixer(xf, mod, mlstm_w_in[j], mlstm_b_gates[j], mlstm_norm_w[j], mlstm_w_out[j],
                              ln_g[i, 0], ln_b[i, 0], bsz=bsz, seq=seq, alpha=alpha)
        elif kind == 1:
            xf = _hgrn_mixer(xf, mod, hgrn_w_in[j], hgrn_lb_param, hgrn_norm_w[j], hgrn_w_out[j],
                             ln_g[i, 0], ln_b[i, 0], bsz=bsz, seq=seq, alpha=alpha, layer=i)
        else:
            xf = _ssd_mixer(xf, mod, ssm_w_in[j], ssm_conv_w[j], ssm_conv_b[j], ssm_dt_bias[j], ssm_a_log[j],
                            ssm_d[j], ssm_norm_w[j], ssm_w_out[j], ln_g[i, 0], ln_b[i, 0],
                            bsz=bsz, seq=seq, alpha=alpha)
        xf = _moe_ffn(xf, mod, moe_w_router[i], moe_b_router[i], w1p, b1p, w2b, b2r,
                      ln_g[i, 1], ln_b[i, 1], seq=seq, alpha=alpha, e0=i * ne)
    return xf.reshape(bsz, seq, d)
```

```python
import functools

import jax
import jax.numpy as jnp
from jax import lax
from jax.experimental import pallas as pl
from jax.experimental.pallas import tpu as pltpu

F32 = jnp.float32
BF16 = jnp.bfloat16
I32 = jnp.int32
U32 = jnp.uint32
HI_HALF_MASK = 0xFFFF0000

MLSTM_GATE_CAP = 15.0
HGRN_HEAD_DIM = 128
SSM_GROUPS = 8
SSM_HEADDIM = 64
TOP_K = 4
SWIGLU_ALPHA = 1.702
SWIGLU_LIMIT = 7.0
LN_EPS = 1e-5
RMS_EPS = 1e-6

LANES = 128
SUBLANES = 8
BF16_ROWS = 16
MXU_COLS = 256
VMEM_LIMIT_BYTES = 56 * 1024 * 1024

MLSTM_CHUNK = 128
HGRN_CHUNK = 64
HGRN_SUB = 8
HGRN_ROWS_PER_STEP = 512
HGRN_HEADS_PER_STEP = 4
SSD_CHUNK = 128
MOE_BLOCK = 512

_HI = lax.Precision.HIGHEST


def _cparams(sem):
    return pltpu.CompilerParams(dimension_semantics=sem, vmem_limit_bytes=VMEM_LIMIT_BYTES)


def _dot(a, b):
    return jnp.dot(a, b, preferred_element_type=F32)


def _dot_nt(a, b):
    return lax.dot_general(a, b, (((1,), (1,)), ((), ())), preferred_element_type=F32)


def _dot_tn(a, b):
    return lax.dot_general(a, b, (((0,), (0,)), ((), ())), preferred_element_type=F32)


def _dot_hi(a, b):
    return jnp.dot(a, b, preferred_element_type=F32, precision=_HI)


def _sigmoid(x):
    return 1.0 / (1.0 + jnp.exp(-x))


def _silu(x):
    return x * _sigmoid(x)


def _log_sigmoid(x):
    return jnp.minimum(x, 0.0) - jnp.log1p(jnp.exp(-jnp.abs(x)))


def _softplus(x):
    return jnp.maximum(x, 0.0) + jnp.log1p(jnp.exp(-jnp.abs(x)))


def _lower_tri(n):
    r = lax.broadcasted_iota(I32, (n, n), 0)
    c = lax.broadcasted_iota(I32, (n, n), 1)
    return r >= c


def _ada_kernel(c_ref, w_ref, b_ref, o_ref):
    cond = _silu(c_ref[...])
    o_ref[0] = _dot(cond.astype(BF16), w_ref[0].astype(BF16)) + b_ref[0]


def _ada_mods(c, ada_w, ada_b):
    depth, d, n6 = ada_w.shape
    bsz = c.shape[0]
    rows = -(-bsz // SUBLANES) * SUBLANES
    c_pad = jnp.zeros((rows, d), F32).at[:bsz].set(c)
    tn = 1536 if n6 % 1536 == 0 else n6
    out = pl.pallas_call(
        _ada_kernel,
        out_shape=jax.ShapeDtypeStruct((depth, rows, n6), F32),
        grid=(depth, n6 // tn),
        in_specs=[
            pl.BlockSpec((rows, d), lambda l, j: (0, 0)),
            pl.BlockSpec((1, d, tn), lambda l, j: (l, 0, j)),
            pl.BlockSpec((1, 1, tn), lambda l, j: (l, 0, j)),
        ],
        out_specs=pl.BlockSpec((1, rows, tn), lambda l, j: (l, 0, j)),
        compiler_params=_cparams(("parallel", "parallel")),
        name="ada_mods",
    )(c_pad, ada_w, ada_b.reshape(depth, 1, n6))
    return out[:, :bsz].reshape(depth, bsz, 6, d)


def _linear_mod_kernel(x_ref, mod_ref, w_ref, o_ref, h_ref, *, shift_row, scale_row):
    @pl.when(pl.program_id(1) == 0)
    def _():
        m = mod_ref[0]
        h = x_ref[...] * (1.0 + m[scale_row:scale_row + 1, :]) + m[shift_row:shift_row + 1, :]
        h_ref[...] = h.astype(BF16)

    o_ref[...] = _dot(h_ref[...], w_ref[...]).astype(o_ref.dtype)


def _linear_mod(x, mod, w, *, seq, shift_row, scale_row, out_dtype, tm, tn, name):
    t, d = x.shape
    n = w.shape[1]
    tm = min(tm, seq)
    tn = min(tn, n)
    assert t % tm == 0 and seq % tm == 0 and n % tn == 0
    return pl.pallas_call(
        functools.partial(_linear_mod_kernel, shift_row=shift_row, scale_row=scale_row),
        out_shape=jax.ShapeDtypeStruct((t, n), out_dtype),
        grid=(t // tm, n // tn),
        in_specs=[
            pl.BlockSpec((tm, d), lambda i, j: (i, 0)),
            pl.BlockSpec((1, 6, d), lambda i, j: ((i * tm) // seq, 0, 0)),
            pl.BlockSpec((d, tn), lambda i, j: (0, j)),
        ],
        out_specs=pl.BlockSpec((tm, tn), lambda i, j: (i, j)),
        scratch_shapes=[pltpu.VMEM((tm, d), BF16)],
        compiler_params=_cparams(("parallel", "arbitrary")),
        name=name,
    )(x, mod, w)


def _deepnorm_ln(x, y, gate, g, b, alpha):
    v = alpha * x + (1.0 + gate) * y
    mu = jnp.mean(v, axis=-1, keepdims=True)
    dv = v - mu
    var = jnp.mean(dv * dv, axis=-1, keepdims=True)
    return dv * lax.rsqrt(var + LN_EPS) * g + b


def _outproj_ln_kernel(y_ref, w_ref, x_ref, mod_ref, g_ref, b_ref, o_ref, acc_ref, *, gate_row, nk, alpha):
    k = pl.program_id(1)
    part = _dot(y_ref[...], w_ref[...])

    def finish(acc):
        gate = mod_ref[0][gate_row:gate_row + 1, :]
        o_ref[...] = _deepnorm_ln(x_ref[...], acc, gate, g_ref[...], b_ref[...], alpha)

    if nk == 1:
        finish(part)
    else:
        @pl.when(k == 0)
        def _():
            acc_ref[...] = part

        @pl.when(jnp.logical_and(k > 0, k < nk - 1))
        def _():
            acc_ref[...] += part

        @pl.when(k == nk - 1)
        def _():
            finish(acc_ref[...] + part)


def _outproj_ln(y, w, x, mod, ln_g, ln_b, *, seq, gate_row, alpha, tm, tk, name):
    t, kdim = y.shape
    d = w.shape[1]
    tm = min(tm, seq)
    tk = min(tk, kdim)
    assert t % tm == 0 and seq % tm == 0 and kdim % tk == 0
    nk = kdim // tk
    return pl.pallas_call(
        functools.partial(_outproj_ln_kernel, gate_row=gate_row, nk=nk, alpha=alpha),
        out_shape=jax.ShapeDtypeStruct((t, d), F32),
        grid=(t // tm, nk),
        in_specs=[
            pl.BlockSpec((tm, tk), lambda i, k: (i, k)),
            pl.BlockSpec((tk, d), lambda i, k: (k, 0)),
            pl.BlockSpec((tm, d), lambda i, k: (i, 0)),
            pl.BlockSpec((1, 6, d), lambda i, k: ((i * tm) // seq, 0, 0)),
            pl.BlockSpec((1, d), lambda i, k: (0, 0)),
            pl.BlockSpec((1, d), lambda i, k: (0, 0)),
        ],
        out_specs=pl.BlockSpec((tm, d), lambda i, k: (i, 0)),
        scratch_shapes=[pltpu.VMEM((tm, d), F32)],
        compiler_params=_cparams(("parallel", "arbitrary")),
        name=name,
    )(y, w, x, mod, ln_g.reshape(1, d), ln_b.reshape(1, d))


def _mlstm_kernel(q_ref, k_ref, v_ref, o_ref, aux_ref, bias_ref, nw_ref, y_ref,
                  c_ref, n_ref, m_ref, *, heads, dqk, dv, chunk):
    @pl.when(pl.program_id(1) == 0)
    def _():
        c_ref[...] = jnp.zeros_like(c_ref)
        n_ref[...] = jnp.zeros_like(n_ref)
        m_ref[...] = jnp.zeros_like(m_ref)

    scale = dqk ** -0.5
    g = MLSTM_GATE_CAP * jnp.tanh((aux_ref[...] + bias_ref[...]) / MLSTM_GATE_CAP)
    lane = lax.broadcasted_iota(I32, g.shape, 1)
    logsig = _log_sigmoid(g)
    is_f = jnp.logical_and(lane >= heads, lane < 2 * heads)
    tri = _lower_tri(chunk)
    bcum = _dot_hi(tri.astype(F32), jnp.where(is_f, logsig, 0.0))
    g_t = g.T
    bcum_t = bcum.T

    for h in range(heads):
        ig_col = g[:, h:h + 1]
        ig_row = g_t[h:h + 1, :]
        b_col = bcum[:, heads + h:heads + h + 1]
        b_row = bcum_t[heads + h:heads + h + 1, :]
        m_prev = m_ref[h][0:1, 0:1]

        log_d = jnp.where(tri, b_col - b_row + ig_row, -jnp.inf)
        m_inter = b_col + m_prev
        m_t = jnp.maximum(jnp.max(log_d, axis=1, keepdims=True), m_inter)
        dmat = jnp.exp(log_d - m_t)

        q = q_ref[:, h * dqk:(h + 1) * dqk]
        k = k_ref[:, h * dqk:(h + 1) * dqk]
        v = v_ref[:, h * dv:(h + 1) * dv]
        scores = _dot_nt(q, k) * scale * dmat
        inter = jnp.exp(m_inter - m_t)
        c_st = c_ref[h]
        n_st = n_ref[h][0:1, :]
        num = _dot(scores.astype(BF16), v) + inter * (_dot_nt(q, c_st.astype(BF16)) * scale)
        qn = jnp.sum(q.astype(F32) * n_st, axis=1, keepdims=True) * scale
        den = jnp.sum(scores, axis=1, keepdims=True) + inter * qn
        hout = num * (1.0 / jnp.maximum(jnp.abs(den), jnp.exp(-m_t)))

        b_last = b_col[chunk - 1:chunk, :]
        log_w = b_last - b_col + ig_col
        m_new = jnp.maximum(b_last + m_prev, jnp.max(log_w, axis=0, keepdims=True))
        w = jnp.exp(log_w - m_new)
        decay = jnp.exp(b_last + m_prev - m_new)
        vw = (v.astype(F32) * w).astype(BF16)
        c_ref[h] = decay * c_st + _dot_tn(vw, k)
        n_new = decay * n_st + jnp.sum(k.astype(F32) * w, axis=0, keepdims=True)
        n_ref[h] = jnp.broadcast_to(n_new, n_ref.shape[1:])
        m_ref[h] = jnp.broadcast_to(m_new, m_ref.shape[1:])

        ms = jnp.mean(hout * hout, axis=1, keepdims=True)
        hn = hout * lax.rsqrt(ms + RMS_EPS) * nw_ref[:, h * dv:(h + 1) * dv]
        og = o_ref[:, h * dv:(h + 1) * dv].astype(F32)
        y_ref[:, h * dv:(h + 1) * dv] = (hn * _sigmoid(og)).astype(y_ref.dtype)


def _mlstm_recurrence(main, aux, bias, norm_w, *, bsz, seq, heads, dqk, dv):
    t = main.shape[0]
    nqk, nv = heads * dqk, heads * dv
    chunk = min(MLSTM_CHUNK, seq)
    nc = seq // chunk
    assert nv == 2 * nqk and seq % chunk == 0
    row = lambda b, c: b * nc + c
    return pl.pallas_call(
        functools.partial(_mlstm_kernel, heads=heads, dqk=dqk, dv=dv, chunk=chunk),
        out_shape=jax.ShapeDtypeStruct((t, nv), BF16),
        grid=(bsz, nc),
        in_specs=[
            pl.BlockSpec((chunk, nqk), lambda b, c: (row(b, c), 0)),
            pl.BlockSpec((chunk, nqk), lambda b, c: (row(b, c), 1)),
            pl.BlockSpec((chunk, nv), lambda b, c: (row(b, c), 1)),
            pl.BlockSpec((chunk, nv), lambda b, c: (row(b, c), 2)),
            pl.BlockSpec((chunk, LANES), lambda b, c: (row(b, c), 0)),
            pl.BlockSpec((1, LANES), lambda b, c: (0, 0)),
            pl.BlockSpec((1, nv), lambda b, c: (0, 0)),
        ],
        out_specs=pl.BlockSpec((chunk, nv), lambda b, c: (row(b, c), 0)),
        scratch_shapes=[
            pltpu.VMEM((heads, dv, dqk), F32),
            pltpu.VMEM((heads, SUBLANES, dqk), F32),
            pltpu.VMEM((heads, SUBLANES, LANES), F32),
        ],
        compiler_params=_cparams(("parallel", "arbitrary")),
        name="mlstm_recurrence",
    )(main, main, main, main, aux, bias, norm_w.reshape(1, nv))


def _mlstm_mixer(x, mod, w_in, b_gates, norm_w, w_out, ln_g, ln_b, *, bsz, seq, alpha):
    d = x.shape[1]
    heads = b_gates.shape[0] // 2
    nv = w_out.shape[0]
    nqk = (w_in.shape[1] - 2 * nv - 2 * heads) // 2
    n_main = 2 * nqk + 2 * nv
    w_main = w_in[:, :n_main].astype(BF16)
    w_aux = jnp.zeros((d, LANES), BF16).at[:, :2 * heads].set(w_in[:, n_main:].astype(BF16))
    bias = jnp.zeros((1, LANES), F32).at[0, :2 * heads].set(b_gates)
    main = _linear_mod(x, mod, w_main, seq=seq, shift_row=0, scale_row=1, out_dtype=BF16,
                       tm=1024, tn=512, name="mlstm_in_main")
    aux = _linear_mod(x, mod, w_aux, seq=seq, shift_row=0, scale_row=1, out_dtype=F32,
                      tm=1024, tn=LANES, name="mlstm_in_gates")
    y = _mlstm_recurrence(main, aux, bias, norm_w, bsz=bsz, seq=seq, heads=heads,
                          dqk=nqk // heads, dv=nv // heads)
    return _outproj_ln(y, w_out.astype(BF16), x, mod, ln_g, ln_b, seq=seq, gate_row=2, alpha=alpha,
                       tm=512, tk=2048, name="mlstm_out_ln")


def _hgrn_kernel(main_ref, lbp_ref, nw_ref, y_ref, st_ref, *, layer, chunk, sub, nchunks, hps):
    dk = HGRN_HEAD_DIM

    @pl.when(pl.program_id(2) == 0)
    def _():
        st_ref[...] = jnp.zeros_like(st_ref)

    p = lbp_ref[...]
    e = jnp.exp(p - jnp.max(p, axis=0, keepdims=True))
    sm = e / jnp.sum(e, axis=0, keepdims=True)
    lb_all = jnp.zeros((1, hps * dk), F32)
    for r in range(1, layer + 1):
        lb_all = lb_all + sm[r:r + 1, :]
    tri = _lower_tri(chunk).astype(F32)
    nsub = chunk // sub
    sub_row = lax.broadcasted_iota(I32, (sub, dk), 0)
    ones_sq = jnp.ones((dk, dk), BF16)
    nw = nw_ref[...]

    def head_chunk(r0, hh):
        lb = lb_all[:, hh * dk:(hh + 1) * dk]
        log_lb = jnp.log(lb)
        log_1m = jnp.log1p(-lb)
        one_m = 1.0 - lb
        c0 = hh * 4 * dk
        blk = main_ref[pl.ds(r0, chunk), c0:c0 + 4 * dk]
        qraw = blk[:, 0:dk].astype(F32)
        fraw = blk[:, dk:2 * dk].astype(F32)
        iv_b = blk[:, 2 * dk:3 * dk]
        iv = iv_b.astype(F32)
        graw = blk[:, 3 * dk:4 * dk].astype(F32)

        q = _silu(qraw)
        b2 = log_1m + _log_sigmoid(fraw)
        logf = jnp.maximum(log_lb, b2) + jnp.log1p(jnp.exp(-jnp.abs(log_lb - b2)))
        k = one_m * _sigmoid(-fraw)
        gcum = _dot_hi(tri, logf)

        st = st_ref[hh]
        o_inter = _dot_nt((q * jnp.exp(gcum)).astype(BF16), st.astype(BF16))

        outs = []
        for bi in range(nsub):
            lo = bi * sub
            g_i = gcum[lo:lo + sub]
            q_i = q[lo:lo + sub]
            k_i = k[lo:lo + sub]
            i_i = iv[lo:lo + sub]
            acc = o_inter[lo:lo + sub]
            if bi > 0:
                ref_row = gcum[lo - 1:lo]
                hi_rows = -(-lo // BF16_ROWS) * BF16_ROWS
                expo = ref_row - gcum[0:hi_rows]
                if hi_rows != lo:
                    expo = jnp.where(lax.broadcasted_iota(I32, expo.shape, 0) < lo, expo, -jnp.inf)
                qt = (q_i * jnp.exp(g_i - ref_row)).astype(BF16)
                kt = (k[0:hi_rows] * jnp.exp(expo)).astype(BF16)
                a = _dot_nt(qt, kt)
                acc = acc + _dot(a.astype(BF16), iv_b[0:hi_rows])
            pws = []
            for s in range(sub):
                arg = jnp.where(sub_row >= s, g_i - g_i[s:s + 1], -jnp.inf)
                pws.append(q_i * jnp.exp(arg) * k_i[s:s + 1])
            rs = _dot(jnp.concatenate(pws, axis=0).astype(BF16), ones_sq)
            for s in range(sub):
                acc = acc + rs[s * sub:(s + 1) * sub] * i_i[s:s + 1]
            outs.append(acc)
        o = jnp.concatenate(outs, axis=0)

        g_last = gcum[chunk - 1:chunk]
        kdec = (k * jnp.exp(g_last - gcum)).astype(BF16)
        st_ref[hh] = st * jnp.exp(g_last) + _dot_tn(iv_b, kdec)

        ms = jnp.mean(o * o, axis=1, keepdims=True)
        y = o * lax.rsqrt(ms + RMS_EPS) * nw * _silu(graw)
        y_ref[pl.ds(r0, chunk), hh * dk:(hh + 1) * dk] = y.astype(y_ref.dtype)

    def body(ci, carry):
        r0 = pl.multiple_of(ci * chunk, chunk)
        for hh in range(hps):
            head_chunk(r0, hh)
        return carry

    lax.fori_loop(0, nchunks, body, 0)


def _hgrn_recurrence(main, lb_param, norm_w, *, bsz, seq, heads, layer):
    t = main.shape[0]
    dk = HGRN_HEAD_DIM
    depth = lb_param.shape[0]
    rows = min(HGRN_ROWS_PER_STEP, seq)
    chunk = min(HGRN_CHUNK, rows)
    hps = HGRN_HEADS_PER_STEP
    assert seq % rows == 0 and rows % chunk == 0 and chunk % HGRN_SUB == 0 and heads % hps == 0
    nr = seq // rows
    return pl.pallas_call(
        functools.partial(_hgrn_kernel, layer=layer, chunk=chunk, sub=HGRN_SUB, nchunks=rows // chunk, hps=hps),
        out_shape=jax.ShapeDtypeStruct((t, heads * dk), BF16),
        grid=(bsz, heads // hps, nr),
        in_specs=[
            pl.BlockSpec((rows, hps * 4 * dk), lambda b, h, c: (b * nr + c, h)),
            pl.BlockSpec((depth, hps * dk), lambda b, h, c: (0, h)),
            pl.BlockSpec((1, dk), lambda b, h, c: (0, 0)),
        ],
        out_specs=pl.BlockSpec((rows, hps * dk), lambda b, h, c: (b * nr + c, h)),
        scratch_shapes=[pltpu.VMEM((hps, dk, dk), F32)],
        compiler_params=_cparams(("parallel", "parallel", "arbitrary")),
        name="hgrn_recurrence",
    )(main, lb_param, norm_w.reshape(1, dk))


def _hgrn_mixer(x, mod, w_in, lb_param, norm_w, w_out, ln_g, ln_b, *, bsz, seq, alpha, layer):
    d = x.shape[1]
    dk = HGRN_HEAD_DIM
    heads = d // dk
    w_perm = w_in.reshape(d, 4, heads, dk).transpose(0, 2, 1, 3).reshape(d, 4 * d).astype(BF16)
    main = _linear_mod(x, mod, w_perm, seq=seq, shift_row=0, scale_row=1, out_dtype=BF16,
                       tm=1024, tn=512, name="hgrn_in")
    y = _hgrn_recurrence(main, lb_param, norm_w, bsz=bsz, seq=seq, heads=heads, layer=layer)
    return _outproj_ln(y, w_out.astype(BF16), x, mod, ln_g, ln_b, seq=seq, gate_row=2, alpha=alpha,
                       tm=512, tk=2048, name="hgrn_out_ln")


def _conv_kernel(x_ref, halo_ref, w_ref, b_ref, o_ref, *, taps, steps_per_seq):
    first = (pl.program_id(0) % steps_per_seq) == 0
    x = x_ref[...].astype(F32)
    halo = jnp.where(first, 0.0, halo_ref[...].astype(F32))
    xc = jnp.concatenate([halo, x], axis=0)
    hr, tm = halo.shape[0], x.shape[0]
    acc = jnp.broadcast_to(b_ref[...], x.shape)
    for kk in range(taps):
        shift = taps - 1 - kk
        xs = xc if shift == 0 else pltpu.roll(xc, shift, 0)
        acc = acc + w_ref[kk:kk + 1, :] * xs[hr:hr + tm]
    o_ref[...] = _silu(acc).astype(o_ref.dtype)


def _ssd_conv(main, conv_w, conv_b, *, seq, col0):
    t = main.shape[0]
    taps, cdim = conv_w.shape
    tm = min(512, seq)
    tc = 512
    halo = 2 * SUBLANES
    assert seq % tm == 0 and cdim % tc == 0 and col0 % tc == 0 and taps - 1 <= halo and tm % halo == 0
    cb0 = col0 // tc
    hb = tm // halo
    return pl.pallas_call(
        functools.partial(_conv_kernel, taps=taps, steps_per_seq=seq // tm),
        out_shape=jax.ShapeDtypeStruct((t, cdim), BF16),
        grid=(t // tm, cdim // tc),
        in_specs=[
            pl.BlockSpec((tm, tc), lambda i, j: (i, cb0 + j)),
            pl.BlockSpec((halo, tc), lambda i, j: (jnp.maximum(i * hb - 1, 0), cb0 + j)),
            pl.BlockSpec((taps, tc), lambda i, j: (0, j)),
            pl.BlockSpec((1, tc), lambda i, j: (0, j)),
        ],
        out_specs=pl.BlockSpec((tm, tc), lambda i, j: (i, j)),
        compiler_params=_cparams(("parallel", "parallel")),
        name="ssd_conv",
    )(main, main, conv_w, conv_b.reshape(1, cdim))


def _ssd_kernel(xs_ref, b_ref, c_ref, z_ref, aux_ref, dtb_ref, alog_ref, dsk_ref, nw_ref, y_ref,
                state_ref, dt_ref, cum_ref, dt_t_ref, cum_t_ref, *, chunk, hpg, hd):
    cidx = pl.program_id(1)
    g = pl.program_id(2)

    @pl.when(jnp.logical_and(cidx == 0, g == 0))
    def _():
        state_ref[...] = jnp.zeros_like(state_ref)

    tri = _lower_tri(chunk)

    @pl.when(g == 0)
    def _():
        dt = _softplus(aux_ref[...] + dtb_ref[...])
        la = dt * (-jnp.exp(alog_ref[...]))
        cum = _dot_hi(tri.astype(F32), la)
        dt_ref[...] = dt
        cum_ref[...] = cum
        dt_t_ref[...] = dt.T
        cum_t_ref[...] = cum.T

    dt_nat = dt_ref[...]
    cum_nat = cum_ref[...]
    g8 = pl.multiple_of(g * hpg, hpg)
    cum_t = cum_t_ref[pl.ds(g8, hpg), :]
    dt_t = dt_t_ref[pl.ds(g8, hpg), :]
    lane = lax.broadcasted_iota(I32, cum_nat.shape, 1)

    xs = xs_ref[...].astype(F32)
    x_t = xs.T
    bm = b_ref[...]
    cm = c_ref[...]
    cb_t = _dot_nt(bm, cm)
    st = state_ref[g]
    inter_t = _dot_nt(st.astype(BF16), cm)
    last = cum_t[:, chunk - 1:chunk]
    w_t = jnp.exp(last - cum_t) * dt_t
    ecum_t = jnp.exp(cum_t)
    s_le_t = jnp.logical_not(tri) | (lax.broadcasted_iota(I32, (chunk, chunk), 0)
                                     == lax.broadcasted_iota(I32, (chunk, chunk), 1))

    y_parts = []
    xw_parts = []
    st_parts = []
    for j in range(hpg):
        col = g8 + j
        sel = lane == col
        cum_col = jnp.sum(jnp.where(sel, cum_nat, 0.0), axis=1, keepdims=True)
        dt_col = jnp.sum(jnp.where(sel, dt_nat, 0.0), axis=1, keepdims=True)
        cum_row = cum_t[j:j + 1, :]
        dec_t = jnp.exp(jnp.where(s_le_t, cum_row - cum_col, -jnp.inf))
        m_t = (cb_t * dec_t * dt_col).astype(BF16)
        x_j = x_t[j * hd:(j + 1) * hd, :]
        y_j = _dot(x_j.astype(BF16), m_t) + inter_t[j * hd:(j + 1) * hd, :] * ecum_t[j:j + 1, :]
        y_parts.append(y_j)
        xw_parts.append((x_j * w_t[j:j + 1, :]).astype(BF16))
        st_parts.append(st[j * hd:(j + 1) * hd, :] * jnp.exp(last[j:j + 1, :]))
    y_t = jnp.concatenate(y_parts, axis=0)
    xw_t = jnp.concatenate(xw_parts, axis=0)
    state_ref[g] = jnp.concatenate(st_parts, axis=0) + _dot(xw_t, bm)

    y = y_t.T + dsk_ref[...] * xs
    y = y * _silu(z_ref[...].astype(F32))
    ms = jnp.mean(y * y, axis=1, keepdims=True)
    y_ref[...] = (y * lax.rsqrt(ms + RMS_EPS) * nw_ref[...]).astype(y_ref.dtype)


def _ssd_recurrence(main, conv, aux, dt_bias, a_log, d_skip, norm_w, *, bsz, seq, d_inner, nstate):
    t = main.shape[0]
    groups = SSM_GROUPS
    gw = d_inner // groups
    hpg = gw // SSM_HEADDIM
    nheads = groups * hpg
    chunk = min(SSD_CHUNK, seq)
    assert seq % chunk == 0 and nstate == LANES and chunk == LANES and nheads <= LANES
    nc = seq // chunk
    row = lambda b, c, g: b * nc + c
    pad = lambda v: jnp.zeros((1, LANES), F32).at[0, :nheads].set(v)
    dsk = jnp.repeat(d_skip, SSM_HEADDIM).reshape(1, d_inner)
    b0 = d_inner // nstate
    return pl.pallas_call(
        functools.partial(_ssd_kernel, chunk=chunk, hpg=hpg, hd=SSM_HEADDIM),
        out_shape=jax.ShapeDtypeStruct((t, d_inner), BF16),
        grid=(bsz, nc, groups),
        in_specs=[
            pl.BlockSpec((chunk, gw), lambda b, c, g: (row(b, c, g), g)),
            pl.BlockSpec((chunk, nstate), lambda b, c, g: (row(b, c, g), b0 + g)),
            pl.BlockSpec((chunk, nstate), lambda b, c, g: (row(b, c, g), b0 + groups + g)),
            pl.BlockSpec((chunk, gw), lambda b, c, g: (row(b, c, g), g)),
            pl.BlockSpec((chunk, LANES), lambda b, c, g: (row(b, c, g), 0)),
            pl.BlockSpec((1, LANES), lambda b, c, g: (0, 0)),
            pl.BlockSpec((1, LANES), lambda b, c, g: (0, 0)),
            pl.BlockSpec((1, gw), lambda b, c, g: (0, g)),
            pl.BlockSpec((1, gw), lambda b, c, g: (0, g)),
        ],
        out_specs=pl.BlockSpec((chunk, gw), lambda b, c, g: (row(b, c, g), g)),
        scratch_shapes=[
            pltpu.VMEM((groups, gw, nstate), F32),
            pltpu.VMEM((chunk, LANES), F32),
            pltpu.VMEM((chunk, LANES), F32),
            pltpu.VMEM((LANES, chunk), F32),
            pltpu.VMEM((LANES, chunk), F32),
        ],
        compiler_params=_cparams(("parallel", "arbitrary", "arbitrary")),
        name="ssd_recurrence",
    )(conv, conv, conv, main, aux, pad(dt_bias), pad(a_log), dsk, norm_w.reshape(1, d_inner))


def _ssd_mixer(x, mod, w_in, conv_w, conv_b, dt_bias, a_log, d_skip, norm_w, w_out, ln_g, ln_b,
               *, bsz, seq, alpha):
    d = x.shape[1]
    d_inner = w_out.shape[0]
    cdim = conv_w.shape[-1]
    nheads = dt_bias.shape[0]
    nstate = (cdim - d_inner) // (2 * SSM_GROUPS)
    n_main = d_inner + cdim
    w_main = w_in[:, :n_main].astype(BF16)
    w_aux = jnp.zeros((d, LANES), BF16).at[:, :nheads].set(w_in[:, n_main:].astype(BF16))
    main = _linear_mod(x, mod, w_main, seq=seq, shift_row=0, scale_row=1, out_dtype=BF16,
                       tm=1024, tn=512, name="ssd_in_main")
    aux = _linear_mod(x, mod, w_aux, seq=seq, shift_row=0, scale_row=1, out_dtype=F32,
                      tm=1024, tn=LANES, name="ssd_in_dt")
    conv = _ssd_conv(main, conv_w.reshape(conv_w.shape[0], cdim), conv_b, seq=seq, col0=d_inner)
    y = _ssd_recurrence(main, conv, aux, dt_bias, a_log, d_skip, norm_w, bsz=bsz, seq=seq,
                        d_inner=d_inner, nstate=nstate)
    return _outproj_ln(y, w_out.astype(BF16), x, mod, ln_g, ln_b, seq=seq, gate_row=2, alpha=alpha,
                       tm=512, tk=2048, name="ssd_out_ln")


def _pack_pairs(lo, hi):
    lo_b = lax.bitcast_convert_type(lo.astype(BF16).astype(F32), U32) >> 16
    hi_b = lax.bitcast_convert_type(hi.astype(BF16).astype(F32), U32) & jnp.uint32(HI_HALF_MASK)
    return lo_b | hi_b


def _unpack_pairs(w):
    lo = lax.bitcast_convert_type(w << 16, F32)
    hi = lax.bitcast_convert_type(w & jnp.uint32(HI_HALF_MASK), F32)
    return lo, hi


def _store_packed_rows(dst_ref, v):
    rows, d = v.shape
    half = d // 2
    pr = half // LANES
    w = _pack_pairs(v[:, :half], v[:, half:])
    for c in range(pr):
        dst_ref[pl.ds(c, rows, stride=pr), :] = w[:, c * LANES:(c + 1) * LANES]


def _load_packed_rows(src_ref, rows, pr, base=0):
    return [_unpack_pairs(src_ref[pl.ds(base + c, rows, stride=pr), :]) for c in range(pr)]


def _router_kernel(x_ref, mod_ref, wr_ref, br_ref, h_ref, ti_ref, tg_ref, *, n_real):
    i = pl.program_id(0)

    @pl.when(i < n_real)
    def _():
        m = mod_ref[0]
        h = x_ref[...] * (1.0 + m[4:5, :]) + m[3:4, :]
        _store_packed_rows(h_ref, h)
        logits = _dot_hi(h, wr_ref[...]) + br_ref[...]
        lane = lax.broadcasted_iota(I32, logits.shape, 1)
        lane_f = lane.astype(F32)
        vals = []
        ti = jnp.zeros(logits.shape, I32)
        for kk in range(TOP_K):
            mx = jnp.max(logits, axis=1, keepdims=True)
            idx = jnp.min(jnp.where(logits == mx, lane_f, float(LANES)), axis=1, keepdims=True).astype(I32)
            vals.append(mx)
            ti = jnp.where(lane == kk, idx, ti)
            logits = jnp.where(lane == idx, -jnp.inf, logits)
        es = [jnp.exp(v - vals[0]) for v in vals]
        tot = es[0]
        for e in es[1:]:
            tot = tot + e
        tg = jnp.zeros(ti.shape, F32)
        for kk in range(TOP_K):
            tg = jnp.where(lane == kk, es[kk] / tot, tg)
        ti_ref[...] = ti
        tg_ref[...] = tg

    @pl.when(i >= n_real)
    def _():
        h_ref[...] = jnp.zeros_like(h_ref)
        ti_ref[...] = jnp.zeros_like(ti_ref)
        tg_ref[...] = jnp.zeros_like(tg_ref)


def _router(x, mod, w_router, b_router, *, seq, tm):
    t, d = x.shape
    ne = w_router.shape[1]
    tm = min(tm, seq)
    n_real = t // tm
    wr = jnp.zeros((d, LANES), F32).at[:, :ne].set(w_router)
    br = jnp.full((1, LANES), -1e30, F32).at[0, :ne].set(b_router)
    clamp = lambda i: jnp.minimum(i, n_real - 1)
    pr = d // (2 * LANES)
    return pl.pallas_call(
        functools.partial(_router_kernel, n_real=n_real),
        out_shape=(
            jax.ShapeDtypeStruct(((t + tm) * pr, LANES), U32),
            jax.ShapeDtypeStruct((t + tm, LANES), I32),
            jax.ShapeDtypeStruct((t + tm, LANES), F32),
        ),
        grid=(n_real + 1,),
        in_specs=[
            pl.BlockSpec((tm, d), lambda i: (clamp(i), 0)),
            pl.BlockSpec((1, 6, d), lambda i: ((clamp(i) * tm) // seq, 0, 0)),
            pl.BlockSpec((d, LANES), lambda i: (0, 0)),
            pl.BlockSpec((1, LANES), lambda i: (0, 0)),
        ],
        out_specs=(
            pl.BlockSpec((tm * pr, LANES), lambda i: (i, 0)),
            pl.BlockSpec((tm, LANES), lambda i: (i, 0)),
            pl.BlockSpec((tm, LANES), lambda i: (i, 0)),
        ),
        compiler_params=_cparams(("parallel",)),
        name="moe_router",
    )(x, mod, wr, br)


def _w1_prep_kernel(w_ref, p_ref, o_ref):
    o_ref[0] = _dot(w_ref[0].astype(BF16), p_ref[...]).astype(BF16)


def _prep_w1(w1):
    ne, d, f2 = w1.shape
    f = f2 // 2
    j = jnp.arange(f2, dtype=I32)
    src_col = jnp.where(j < f, 2 * j, 2 * (j - f) + 1)
    sel = (jnp.arange(f2, dtype=I32)[:, None] == src_col[None, :]).astype(BF16)
    tk = min(512, d)
    return pl.pallas_call(
        _w1_prep_kernel,
        out_shape=jax.ShapeDtypeStruct((ne, d, f2), BF16),
        grid=(ne, d // tk),
        in_specs=[
            pl.BlockSpec((1, tk, f2), lambda e, k: (e, k, 0)),
            pl.BlockSpec((f2, f2), lambda e, k: (0, 0)),
        ],
        out_specs=pl.BlockSpec((1, tk, f2), lambda e, k: (e, k, 0)),
        compiler_params=_cparams(("parallel", "parallel")),
        name="moe_w1_prep",
    )(w1, sel)


def _gather_rows_start(src_hbm, idx_ref, dst_ref, sem, *, count, pr):
    def body(r2, carry):
        for p in range(2):
            r = r2 * 2 + p
            src = src_hbm.at[pl.ds(pl.multiple_of(idx_ref[0, r], pr), pr)]
            dst = dst_ref.at[pl.ds(pl.multiple_of(r * pr, pr), pr)]
            pltpu.make_async_copy(src, dst, sem).start(priority=p)
        return carry

    lax.fori_loop(0, count // 2, body, 0, unroll=4)


def _gather_rows_start_static(src_hbm, idx_ref, dst_ref, sem, *, lo, hi, pr):
    for r in range(lo, hi):
        src = src_hbm.at[pl.ds(pl.multiple_of(idx_ref[0, r], pr), pr)]
        pltpu.make_async_copy(src, dst_ref.at[pl.ds(r * pr, pr)], sem).start(priority=r % 2)


def _gather_rows_wait(src_hbm, dst_ref, sem):
    pltpu.make_async_copy(src_hbm.at[pl.ds(0, dst_ref.shape[0])], dst_ref, sem).wait()


def _expert_kernel(be_ref, rows_ref, src_cur_ref, src_nxt_ref, h_hbm, w1_ref, b1_ref, w2_ref, b2_ref, y_ref,
                   xbuf, xs_ref, gsem, *, blk, nb, f):
    i = pl.program_id(0)
    slot = i % 2
    half = xs_ref.shape[1] // 2
    pr = half // LANES

    @pl.when(jnp.logical_and(i == 0, rows_ref[0] > 0))
    def _():
        _gather_rows_start(h_hbm, src_cur_ref, xbuf.at[0], gsem.at[0], count=blk, pr=pr)

    @pl.when(rows_ref[i] > 0)
    def _():
        _gather_rows_wait(h_hbm, xbuf.at[slot], gsem.at[slot])
        for c, (lo, hi) in enumerate(_load_packed_rows(xbuf.at[slot], blk, pr)):
            xs_ref[:, c * LANES:(c + 1) * LANES] = lo.astype(BF16)
            xs_ref[:, half + c * LANES:half + (c + 1) * LANES] = hi.astype(BF16)
        x = xs_ref[...]
        n_tiles = (2 * f) // MXU_COLS
        per = blk // n_tiles
        parts = []
        for n in range(n_tiles):
            cols = slice(n * MXU_COLS, (n + 1) * MXU_COLS)
            parts.append(_dot(x, w1_ref[0, :, cols]) + b1_ref[0, :, cols])
            _gather_rows_start_static(h_hbm, src_nxt_ref, xbuf.at[1 - slot], gsem.at[1 - slot],
                                      lo=n * per, hi=(n + 1) * per, pr=pr)
        hh = jnp.concatenate(parts, axis=1)
        glu = jnp.minimum(hh[:, :f], SWIGLU_LIMIT)
        lin = jnp.clip(hh[:, f:], -SWIGLU_LIMIT, SWIGLU_LIMIT)
        act = glu * _sigmoid(SWIGLU_ALPHA * glu) * (lin + 1.0)
        _store_packed_rows(y_ref, _dot(act.astype(BF16), w2_ref[0]) + b2_ref[0])

    @pl.when(rows_ref[i] == 0)
    def _():
        @pl.when(jnp.logical_and(i >= 1, rows_ref[jnp.maximum(i - 1, 0)] > 0))
        def _():
            _gather_rows_wait(h_hbm, xbuf.at[slot], gsem.at[slot])

        y_ref[...] = jnp.zeros_like(y_ref)

    @pl.when(jnp.logical_and(i == nb - 1, rows_ref[i] > 0))
    def _():
        _gather_rows_wait(h_hbm, xbuf.at[1 - slot], gsem.at[1 - slot])


def _experts(h_rows, plan, w1p, b1p, w2b, b2r, *, e0, blk):
    block_e, block_rows, src_rows = plan
    nb = block_e.shape[0]
    f2 = w1p.shape[2]
    f = f2 // 2
    d = w2b.shape[2]
    pr = d // (2 * LANES)
    wmap = lambda i, be, br: (e0 + be[i], 0, 0)
    smem_blk = lambda fn: pl.BlockSpec((None, 1, blk), fn, memory_space=pltpu.SMEM)
    grid_spec = pltpu.PrefetchScalarGridSpec(
        num_scalar_prefetch=2,
        grid=(nb,),
        in_specs=[
            smem_blk(lambda i, be, br: (i, 0, 0)),
            smem_blk(lambda i, be, br: (jnp.minimum(i + 1, nb - 1), 0, 0)),
            pl.BlockSpec(memory_space=pl.ANY),
            pl.BlockSpec((1, d, f2), wmap),
            pl.BlockSpec((1, 1, f2), wmap),
            pl.BlockSpec((1, f, d), wmap),
            pl.BlockSpec((1, 1, d), wmap),
        ],
        out_specs=pl.BlockSpec((blk * pr, LANES), lambda i, be, br: (i, 0)),
        scratch_shapes=[
            pltpu.VMEM((2, blk * pr, LANES), U32),
            pltpu.VMEM((blk, d), BF16),
            pltpu.SemaphoreType.DMA((2,)),
        ],
    )
    src3 = src_rows.reshape(nb, 1, blk)
    return pl.pallas_call(
        functools.partial(_expert_kernel, blk=blk, nb=nb, f=f),
        out_shape=jax.ShapeDtypeStruct((nb * blk * pr, LANES), U32),
        grid_spec=grid_spec,
        compiler_params=_cparams(("arbitrary",)),
        name="moe_experts",
    )(block_e, block_rows, src3, src3, h_rows, w1p, b1p, w2b, b2r)


def _combine_ln_kernel(idx_cur_ref, idx_nxt_ref, y_hbm, tg_ref, x_ref, mod_ref, g_ref, b_ref, o_ref,
                       ybuf, sem, *, alpha, nt):
    i = pl.program_id(0)
    slot = i % 2
    tm, d = x_ref.shape
    pr = d // (2 * LANES)
    count = TOP_K * tm

    @pl.when(i == 0)
    def _():
        _gather_rows_start(y_hbm, idx_cur_ref, ybuf.at[0], sem.at[0], count=count, pr=pr)

    @pl.when(i + 1 < nt)
    def _():
        _gather_rows_start(y_hbm, idx_nxt_ref, ybuf.at[1 - slot], sem.at[1 - slot], count=count, pr=pr)

    _gather_rows_wait(y_hbm, ybuf.at[slot], sem.at[slot])
    tg = tg_ref[...]
    los, his = [None] * pr, [None] * pr
    for kk in range(TOP_K):
        gk = tg[:, kk:kk + 1]
        for c, (lo, hi) in enumerate(_load_packed_rows(ybuf.at[slot], tm, pr, base=kk * tm * pr)):
            los[c] = gk * lo if kk == 0 else los[c] + gk * lo
            his[c] = gk * hi if kk == 0 else his[c] + gk * hi
    acc = jnp.concatenate(los + his, axis=1)
    gate = mod_ref[0][5:6, :]
    o_ref[...] = _deepnorm_ln(x_ref[...], acc, gate, g_ref[...], b_ref[...], alpha)


def _combine_ln(y_rows, slot_rows, tg, x, mod, ln_g, ln_b, *, seq, alpha, tm):
    t, d = x.shape
    nt = t // tm
    pr = d // (2 * LANES)
    idx3 = slot_rows.reshape(nt, 1, TOP_K * tm)
    smem_blk = lambda fn: pl.BlockSpec((None, 1, TOP_K * tm), fn, memory_space=pltpu.SMEM)
    return pl.pallas_call(
        functools.partial(_combine_ln_kernel, alpha=alpha, nt=nt),
        out_shape=jax.ShapeDtypeStruct((t, d), F32),
        grid=(nt,),
        in_specs=[
            smem_blk(lambda i: (i, 0, 0)),
            smem_blk(lambda i: (jnp.minimum(i + 1, nt - 1), 0, 0)),
            pl.BlockSpec(memory_space=pl.ANY),
            pl.BlockSpec((tm, LANES), lambda i: (i, 0)),
            pl.BlockSpec((tm, d), lambda i: (i, 0)),
            pl.BlockSpec((1, 6, d), lambda i: ((i * tm) // seq, 0, 0)),
            pl.BlockSpec((1, d), lambda i: (0, 0)),
            pl.BlockSpec((1, d), lambda i: (0, 0)),
        ],
        out_specs=pl.BlockSpec((tm, d), lambda i: (i, 0)),
        scratch_shapes=[
            pltpu.VMEM((2, TOP_K * tm * pr, LANES), U32),
            pltpu.SemaphoreType.DMA((2,)),
        ],
        compiler_params=_cparams(("arbitrary",)),
        name="moe_combine_ln",
    )(idx3, idx3, y_rows, tg, x, mod, ln_g.reshape(1, d), ln_b.reshape(1, d))


def _dispatch_plan(top_i, *, t, ne, blk, row_sublanes, tm):
    n = t * TOP_K
    n_blocks = n // blk + ne
    n_pad = n_blocks * blk
    e_flat = top_i[:t, :TOP_K].reshape(n)
    e_sorted, order = lax.sort((e_flat, jnp.arange(n, dtype=I32)), num_keys=1, is_stable=True)
    counts = jnp.sum((e_flat[:, None] == jnp.arange(ne, dtype=I32)[None, :]).astype(I32), axis=0)
    starts = jnp.cumsum(counts) - counts
    padded = (counts + blk - 1) // blk * blk
    pad_ends = jnp.cumsum(padded)
    pad_starts = pad_ends - padded
    blk_start = jnp.arange(n_blocks, dtype=I32) * blk
    block_e = jnp.minimum(jnp.sum((pad_ends[None, :] <= blk_start[:, None]).astype(I32), axis=1), ne - 1)
    block_valid = (blk_start < pad_ends[-1]).astype(I32)
    e_slot = jnp.repeat(block_e, blk)
    v_slot = jnp.repeat(block_valid, blk)
    slot = jnp.arange(n_pad, dtype=I32)
    r = slot - pad_starts[e_slot]
    valid = jnp.logical_and(r < counts[e_slot], v_slot == 1)
    p = jnp.clip(starts[e_slot] + r, 0, n - 1)
    src_row = jnp.where(valid, order[p] // TOP_K, t)
    block_rows = jnp.clip(pad_starts[block_e] + counts[block_e] - blk_start, 0, blk) * block_valid
    slot_sorted = pad_starts[e_sorted] + jnp.arange(n, dtype=I32) - starts[e_sorted]
    _, slot_of = lax.sort((order, slot_sorted.astype(I32)), num_keys=1)
    slot_rows = slot_of.reshape(t // tm, tm, TOP_K).transpose(0, 2, 1).reshape(t // tm, TOP_K * tm)
    plan = (block_e, block_rows.astype(I32), (src_row * row_sublanes).astype(I32))
    return plan, (slot_rows * row_sublanes).astype(I32)


def _moe_ffn(x, mod, w_router, b_router, w1p, b1p, w2b, b2r, ln_g, ln_b, *, seq, alpha, e0):
    t, d = x.shape
    ne = w_router.shape[1]
    tm_r = min(512, seq)
    tm_c = min(256, seq)
    h_rows, top_i, top_g = _router(x, mod, w_router, b_router, seq=seq, tm=tm_r)
    plan, slot_rows = _dispatch_plan(top_i, t=t, ne=ne, blk=MOE_BLOCK, row_sublanes=d // (2 * LANES), tm=tm_c)
    y_rows = _experts(h_rows, plan, w1p, b1p, w2b, b2r, e0=e0, blk=MOE_BLOCK)
    return _combine_ln(y_rows, slot_rows, top_g, x, mod, ln_g, ln_b, seq=seq, alpha=alpha, tm=tm_c)


def kernel(x, c, ada_w, ada_b, ln_g, ln_b, mlstm_w_in, mlstm_b_gates, mlstm_norm_w, mlstm_w_out, hgrn_w_in, hgrn_lb_param, hgrn_norm_w, hgrn_w_out, ssm_w_in, ssm_conv_w, ssm_conv_b, ssm_dt_bias, ssm_a_log, ssm_d, ssm_norm_w, ssm_w_out, moe_w_router, moe_b_router, moe_w1, moe_b1, moe_w2, moe_b2):
    bsz, seq, d = x.shape
    depth = ada_w.shape[0]
    alpha = (2 * depth) ** 0.25
    mods = _ada_mods(c, ada_w, ada_b)
    ne, f2 = moe_w1.shape[1], moe_w1.shape[3]
    w1p = _prep_w1(moe_w1.reshape(depth * ne, d, f2))
    b1p = jnp.concatenate([moe_b1[..., 0::2], moe_b1[..., 1::2]], axis=-1).reshape(depth * ne, 1, f2)
    w2b = moe_w2.astype(BF16).reshape(depth * ne, f2 // 2, d)
    b2r = moe_b2.reshape(depth * ne, 1, d)
    xf = x.reshape(bsz * seq, d)
    for i in range(depth):
        mod = mods[i]
        kind, j = i % 3, i // 3
        if kind == 0:
            xf = _mlstm_mixer(xf, mod, mlstm_w_in[j], mlstm_b_gates[j], mlstm_norm_w[j], mlstm_w_out[j],
                              ln_g[i, 0], ln_b[i, 0], bsz=bsz, seq=seq, alpha=alpha)
        elif kind == 1:
            xf = _hgrn_mixer(xf, mod, hgrn_w_in[j], hgrn_lb_param, hgrn_norm_w[j], hgrn_w_out[j],
                             ln_g[i, 0], ln_b[i, 0], bsz=bsz, seq=seq, alpha=alpha, layer=i)
        else:
            xf = _ssd_mixer(xf, mod, ssm_w_in[j], ssm_conv_w[j], ssm_conv_b[j], ssm_dt_bias[j], ssm_a_log[j],
                            ssm_d[j], ssm_norm_w[j], ssm_w_out[j], ln_g[i, 0], ln_b[i, 0],
                            bsz=bsz, seq=seq, alpha=alpha)
        xf = _moe_ffn(xf, mod, moe_w_router[i], moe_b_router[i], w1p, b1p, w2b, b2r,
                      ln_g[i, 1], ln_b[i, 1], seq=seq, alpha=alpha, e0=i * ne)
    return xf.reshape(bsz, seq, d)
```

```python
import functools

import jax
import jax.numpy as jnp
from jax import lax
from jax.experimental import pallas as pl
from jax.experimental.pallas import tpu as pltpu

F32 = jnp.float32
BF16 = jnp.bfloat16
I32 = jnp.int32
U32 = jnp.uint32
HI_HALF_MASK = 0xFFFF0000

MLSTM_GATE_CAP = 15.0
HGRN_HEAD_DIM = 128
SSM_GROUPS = 8
SSM_HEADDIM = 64
TOP_K = 4
SWIGLU_ALPHA = 1.702
SWIGLU_LIMIT = 7.0
LN_EPS = 1e-5
RMS_EPS = 1e-6

LANES = 128
SUBLANES = 8
VMEM_LIMIT_BYTES = 56 * 1024 * 1024

MLSTM_CHUNK = 128
HGRN_CHUNK = 64
HGRN_SUB = 16
HGRN_ROWS_PER_STEP = 512
HGRN_HEADS_PER_STEP = 8
SSD_CHUNK = 128
MOE_BLOCK = 512

_HI = lax.Precision.HIGHEST


def _cparams(sem):
    return pltpu.CompilerParams(dimension_semantics=sem, vmem_limit_bytes=VMEM_LIMIT_BYTES)


def _dot(a, b):
    return jnp.dot(a, b, preferred_element_type=F32)


def _dot_nt(a, b):
    return lax.dot_general(a, b, (((1,), (1,)), ((), ())), preferred_element_type=F32)


def _dot_tn(a, b):
    return lax.dot_general(a, b, (((0,), (0,)), ((), ())), preferred_element_type=F32)


def _dot_hi(a, b):
    return jnp.dot(a, b, preferred_element_type=F32, precision=_HI)


def _sigmoid(x):
    return 1.0 / (1.0 + jnp.exp(-x))


def _silu(x):
    return x * _sigmoid(x)


def _log_sigmoid(x):
    return jnp.minimum(x, 0.0) - jnp.log1p(jnp.exp(-jnp.abs(x)))


def _softplus(x):
    return jnp.maximum(x, 0.0) + jnp.log1p(jnp.exp(-jnp.abs(x)))


def _lower_tri(n):
    r = lax.broadcasted_iota(I32, (n, n), 0)
    c = lax.broadcasted_iota(I32, (n, n), 1)
    return r >= c


def _ada_kernel(c_ref, w_ref, b_ref, o_ref):
    cond = _silu(c_ref[...])
    o_ref[0] = _dot(cond.astype(BF16), w_ref[0].astype(BF16)) + b_ref[0]


def _ada_mods(c, ada_w, ada_b):
    depth, d, n6 = ada_w.shape
    bsz = c.shape[0]
    rows = -(-bsz // SUBLANES) * SUBLANES
    c_pad = jnp.zeros((rows, d), F32).at[:bsz].set(c)
    tn = 1536 if n6 % 1536 == 0 else n6
    out = pl.pallas_call(
        _ada_kernel,
        out_shape=jax.ShapeDtypeStruct((depth, rows, n6), F32),
        grid=(depth, n6 // tn),
        in_specs=[
            pl.BlockSpec((rows, d), lambda l, j: (0, 0)),
            pl.BlockSpec((1, d, tn), lambda l, j: (l, 0, j)),
            pl.BlockSpec((1, 1, tn), lambda l, j: (l, 0, j)),
        ],
        out_specs=pl.BlockSpec((1, rows, tn), lambda l, j: (l, 0, j)),
        compiler_params=_cparams(("parallel", "parallel")),
        name="ada_mods",
    )(c_pad, ada_w, ada_b.reshape(depth, 1, n6))
    return out[:, :bsz].reshape(depth, bsz, 6, d)


def _linear_mod_kernel(x_ref, mod_ref, w_ref, o_ref, h_ref, *, shift_row, scale_row):
    @pl.when(pl.program_id(1) == 0)
    def _():
        m = mod_ref[0]
        h = x_ref[...] * (1.0 + m[scale_row:scale_row + 1, :]) + m[shift_row:shift_row + 1, :]
        h_ref[...] = h.astype(BF16)

    o_ref[...] = _dot(h_ref[...], w_ref[...]).astype(o_ref.dtype)


def _linear_mod(x, mod, w, *, seq, shift_row, scale_row, out_dtype, tm, tn, name):
    t, d = x.shape
    n = w.shape[1]
    tm = min(tm, seq)
    tn = min(tn, n)
    assert t % tm == 0 and seq % tm == 0 and n % tn == 0
    return pl.pallas_call(
        functools.partial(_linear_mod_kernel, shift_row=shift_row, scale_row=scale_row),
        out_shape=jax.ShapeDtypeStruct((t, n), out_dtype),
        grid=(t // tm, n // tn),
        in_specs=[
            pl.BlockSpec((tm, d), lambda i, j: (i, 0)),
            pl.BlockSpec((1, 6, d), lambda i, j: ((i * tm) // seq, 0, 0)),
            pl.BlockSpec((d, tn), lambda i, j: (0, j)),
        ],
        out_specs=pl.BlockSpec((tm, tn), lambda i, j: (i, j)),
        scratch_shapes=[pltpu.VMEM((tm, d), BF16)],
        compiler_params=_cparams(("parallel", "arbitrary")),
        name=name,
    )(x, mod, w)


def _deepnorm_ln(x, y, gate, g, b, alpha):
    v = alpha * x + (1.0 + gate) * y
    mu = jnp.mean(v, axis=-1, keepdims=True)
    dv = v - mu
    var = jnp.mean(dv * dv, axis=-1, keepdims=True)
    return dv * lax.rsqrt(var + LN_EPS) * g + b


def _outproj_ln_kernel(y_ref, w_ref, x_ref, mod_ref, g_ref, b_ref, o_ref, acc_ref, *, gate_row, nk, alpha):
    k = pl.program_id(1)
    part = _dot(y_ref[...], w_ref[...])

    def finish(acc):
        gate = mod_ref[0][gate_row:gate_row + 1, :]
        o_ref[...] = _deepnorm_ln(x_ref[...], acc, gate, g_ref[...], b_ref[...], alpha)

    if nk == 1:
        finish(part)
    else:
        @pl.when(k == 0)
        def _():
            acc_ref[...] = part

        @pl.when(jnp.logical_and(k > 0, k < nk - 1))
        def _():
            acc_ref[...] += part

        @pl.when(k == nk - 1)
        def _():
            finish(acc_ref[...] + part)


def _outproj_ln(y, w, x, mod, ln_g, ln_b, *, seq, gate_row, alpha, tm, tk, name):
    t, kdim = y.shape
    d = w.shape[1]
    tm = min(tm, seq)
    tk = min(tk, kdim)
    assert t % tm == 0 and seq % tm == 0 and kdim % tk == 0
    nk = kdim // tk
    return pl.pallas_call(
        functools.partial(_outproj_ln_kernel, gate_row=gate_row, nk=nk, alpha=alpha),
        out_shape=jax.ShapeDtypeStruct((t, d), F32),
        grid=(t // tm, nk),
        in_specs=[
            pl.BlockSpec((tm, tk), lambda i, k: (i, k)),
            pl.BlockSpec((tk, d), lambda i, k: (k, 0)),
            pl.BlockSpec((tm, d), lambda i, k: (i, 0)),
            pl.BlockSpec((1, 6, d), lambda i, k: ((i * tm) // seq, 0, 0)),
            pl.BlockSpec((1, d), lambda i, k: (0, 0)),
            pl.BlockSpec((1, d), lambda i, k: (0, 0)),
        ],
        out_specs=pl.BlockSpec((tm, d), lambda i, k: (i, 0)),
        scratch_shapes=[pltpu.VMEM((tm, d), F32)],
        compiler_params=_cparams(("parallel", "arbitrary")),
        name=name,
    )(y, w, x, mod, ln_g.reshape(1, d), ln_b.reshape(1, d))


def _mlstm_kernel(q_ref, k_ref, v_ref, o_ref, aux_ref, bias_ref, nw_ref, y_ref,
                  c_ref, n_ref, m_ref, *, heads, dqk, dv, chunk):
    @pl.when(pl.program_id(1) == 0)
    def _():
        c_ref[...] = jnp.zeros_like(c_ref)
        n_ref[...] = jnp.zeros_like(n_ref)
        m_ref[...] = jnp.zeros_like(m_ref)

    scale = dqk ** -0.5
    g = MLSTM_GATE_CAP * jnp.tanh((aux_ref[...] + bias_ref[...]) / MLSTM_GATE_CAP)
    lane = lax.broadcasted_iota(I32, g.shape, 1)
    logsig = _log_sigmoid(g)
    is_f = jnp.logical_and(lane >= heads, lane < 2 * heads)
    tri = _lower_tri(chunk)
    bcum = _dot_hi(tri.astype(F32), jnp.where(is_f, logsig, 0.0))
    g_t = g.T
    bcum_t = bcum.T

    for h in range(heads):
        ig_col = g[:, h:h + 1]
        ig_row = g_t[h:h + 1, :]
        b_col = bcum[:, heads + h:heads + h + 1]
        b_row = bcum_t[heads + h:heads + h + 1, :]
        m_prev = m_ref[h][0:1, 0:1]

        log_d = jnp.where(tri, b_col - b_row + ig_row, -jnp.inf)
        m_inter = b_col + m_prev
        m_t = jnp.maximum(jnp.max(log_d, axis=1, keepdims=True), m_inter)
        dmat = jnp.exp(log_d - m_t)

        q = q_ref[:, h * dqk:(h + 1) * dqk]
        k = k_ref[:, h * dqk:(h + 1) * dqk]
        v = v_ref[:, h * dv:(h + 1) * dv]
        scores = _dot_nt(q, k) * scale * dmat
        inter = jnp.exp(m_inter - m_t)
        c_st = c_ref[h]
        n_st = n_ref[h][0:1, :]
        num = _dot(scores.astype(BF16), v) + inter * (_dot_nt(q, c_st.astype(BF16)) * scale)
        qn = jnp.sum(q.astype(F32) * n_st, axis=1, keepdims=True) * scale
        den = jnp.sum(scores, axis=1, keepdims=True) + inter * qn
        hout = num * (1.0 / jnp.maximum(jnp.abs(den), jnp.exp(-m_t)))

        b_last = b_col[chunk - 1:chunk, :]
        log_w = b_last - b_col + ig_col
        m_new = jnp.maximum(b_last + m_prev, jnp.max(log_w, axis=0, keepdims=True))
        w = jnp.exp(log_w - m_new)
        decay = jnp.exp(b_last + m_prev - m_new)
        vw = (v.astype(F32) * w).astype(BF16)
        c_ref[h] = decay * c_st + _dot_tn(vw, k)
        n_new = decay * n_st + jnp.sum(k.astype(F32) * w, axis=0, keepdims=True)
        n_ref[h] = jnp.broadcast_to(n_new, n_ref.shape[1:])
        m_ref[h] = jnp.broadcast_to(m_new, m_ref.shape[1:])

        ms = jnp.mean(hout * hout, axis=1, keepdims=True)
        hn = hout * lax.rsqrt(ms + RMS_EPS) * nw_ref[:, h * dv:(h + 1) * dv]
        og = o_ref[:, h * dv:(h + 1) * dv].astype(F32)
        y_ref[:, h * dv:(h + 1) * dv] = (hn * _sigmoid(og)).astype(y_ref.dtype)


def _mlstm_recurrence(main, aux, bias, norm_w, *, bsz, seq, heads, dqk, dv):
    t = main.shape[0]
    nqk, nv = heads * dqk, heads * dv
    chunk = min(MLSTM_CHUNK, seq)
    nc = seq // chunk
    assert nv == 2 * nqk and seq % chunk == 0
    row = lambda b, c: b * nc + c
    return pl.pallas_call(
        functools.partial(_mlstm_kernel, heads=heads, dqk=dqk, dv=dv, chunk=chunk),
        out_shape=jax.ShapeDtypeStruct((t, nv), BF16),
        grid=(bsz, nc),
        in_specs=[
            pl.BlockSpec((chunk, nqk), lambda b, c: (row(b, c), 0)),
            pl.BlockSpec((chunk, nqk), lambda b, c: (row(b, c), 1)),
            pl.BlockSpec((chunk, nv), lambda b, c: (row(b, c), 1)),
            pl.BlockSpec((chunk, nv), lambda b, c: (row(b, c), 2)),
            pl.BlockSpec((chunk, LANES), lambda b, c: (row(b, c), 0)),
            pl.BlockSpec((1, LANES), lambda b, c: (0, 0)),
            pl.BlockSpec((1, nv), lambda b, c: (0, 0)),
        ],
        out_specs=pl.BlockSpec((chunk, nv), lambda b, c: (row(b, c), 0)),
        scratch_shapes=[
            pltpu.VMEM((heads, dv, dqk), F32),
            pltpu.VMEM((heads, SUBLANES, dqk), F32),
            pltpu.VMEM((heads, SUBLANES, LANES), F32),
        ],
        compiler_params=_cparams(("parallel", "arbitrary")),
        name="mlstm_recurrence",
    )(main, main, main, main, aux, bias, norm_w.reshape(1, nv))


def _mlstm_mixer(x, mod, w_in, b_gates, norm_w, w_out, ln_g, ln_b, *, bsz, seq, alpha):
    d = x.shape[1]
    heads = b_gates.shape[0] // 2
    nv = w_out.shape[0]
    nqk = (w_in.shape[1] - 2 * nv - 2 * heads) // 2
    n_main = 2 * nqk + 2 * nv
    w_main = w_in[:, :n_main].astype(BF16)
    w_aux = jnp.zeros((d, LANES), BF16).at[:, :2 * heads].set(w_in[:, n_main:].astype(BF16))
    bias = jnp.zeros((1, LANES), F32).at[0, :2 * heads].set(b_gates)
    main = _linear_mod(x, mod, w_main, seq=seq, shift_row=0, scale_row=1, out_dtype=BF16,
                       tm=1024, tn=512, name="mlstm_in_main")
    aux = _linear_mod(x, mod, w_aux, seq=seq, shift_row=0, scale_row=1, out_dtype=F32,
                      tm=1024, tn=LANES, name="mlstm_in_gates")
    y = _mlstm_recurrence(main, aux, bias, norm_w, bsz=bsz, seq=seq, heads=heads,
                          dqk=nqk // heads, dv=nv // heads)
    return _outproj_ln(y, w_out.astype(BF16), x, mod, ln_g, ln_b, seq=seq, gate_row=2, alpha=alpha,
                       tm=512, tk=2048, name="mlstm_out_ln")


def _hgrn_kernel(main_ref, lbp_ref, nw_ref, y_ref, st_ref, q_ref, gc_ref, g_ref, i_ref, o_ref,
                 *, layer, chunk, sub, nchunks, hps):
    dk = HGRN_HEAD_DIM

    @pl.when(pl.program_id(2) == 0)
    def _():
        st_ref[...] = jnp.zeros_like(st_ref)

    p = lbp_ref[...]
    e = jnp.exp(p - jnp.max(p, axis=0, keepdims=True))
    sm = e / jnp.sum(e, axis=0, keepdims=True)
    lb_all = jnp.zeros((1, hps * dk), F32)
    for r in range(1, layer + 1):
        lb_all = lb_all + sm[r:r + 1, :]
    tri = _lower_tri(chunk).astype(F32)
    nsub = chunk // sub
    half = sub // 2
    half_row = lax.broadcasted_iota(I32, (half, dk), 0)
    ones_sq = jnp.ones((dk, dk), BF16)
    nw = nw_ref[...]

    def prologue(r0, hh):
        lb = lb_all[:, hh * dk:(hh + 1) * dk]
        log_lb = jnp.log(lb)
        log_1m = jnp.log1p(-lb)
        c0 = hh * 4 * dk
        blk = main_ref[pl.ds(r0, chunk), c0:c0 + 3 * dk]
        qraw = blk[:, 0:dk].astype(F32)
        fraw = blk[:, dk:2 * dk].astype(F32)
        iv_b = blk[:, 2 * dk:3 * dk]
        q = _silu(qraw)
        lsig = _log_sigmoid(fraw)
        b2 = log_1m + lsig
        logf = jnp.maximum(log_lb, b2) + jnp.log1p(jnp.exp(-jnp.abs(log_lb - b2)))
        gcum = _dot_hi(tri, logf)
        gk = gcum - (log_1m + lsig - fraw)
        q_ref[hh] = q
        gc_ref[hh] = gcum
        g_ref[hh] = gk
        i_ref[hh] = iv_b.astype(F32)
        st = st_ref[hh]
        o_ref[hh] = _dot_nt((q * jnp.exp(gcum)).astype(BF16), st.astype(BF16))
        g_last = gcum[chunk - 1:chunk]
        st_ref[hh] = st * jnp.exp(g_last) + _dot_tn(iv_b, jnp.exp(g_last - gk).astype(BF16))

    def sub_block(r0, hh, bi):
        lo = bi * sub
        g_i = gc_ref[hh, lo:lo + sub, :]
        q_i = q_ref[hh, lo:lo + sub, :]
        acc = o_ref[hh, lo:lo + sub, :]
        if bi > 0:
            ref_row = gc_ref[hh, lo - 1:lo, :]
            qt = (q_i * jnp.exp(g_i - ref_row)).astype(BF16)
            kt = jnp.exp(ref_row - g_ref[hh, 0:lo, :]).astype(BF16)
            c0 = hh * 4 * dk
            iv_b = main_ref[pl.ds(r0, lo), c0 + 2 * dk:c0 + 3 * dk]
            acc = acc + _dot(_dot_nt(qt, kt).astype(BF16), iv_b)
        g_top, g_bot = g_i[:half], g_i[half:]
        q_top, q_bot = q_i[:half], q_i[half:]
        pws = []
        for s in range(sub):
            gs = g_ref[hh, lo + s:lo + s + 1, :]
            if s < half:
                pws.append(q_top * jnp.exp(jnp.where(half_row >= s, g_top - gs, -jnp.inf)))
                pws.append(q_bot * jnp.exp(g_bot - gs))
            else:
                pws.append(q_bot * jnp.exp(jnp.where(half_row >= s - half, g_bot - gs, -jnp.inf)))
        rs = _dot(jnp.concatenate(pws, axis=0).astype(BF16), ones_sq)
        top, bot = acc[:half], acc[half:]
        p = 0
        for s in range(sub):
            i_s = i_ref[hh, lo + s:lo + s + 1, :]
            if s < half:
                top = top + rs[p * half:(p + 1) * half] * i_s
                p += 1
            bot = bot + rs[p * half:(p + 1) * half] * i_s
            p += 1
        o_ref[hh, lo:lo + half, :] = top
        o_ref[hh, lo + half:lo + sub, :] = bot

    def epilogue(r0, hh):
        o = o_ref[hh]
        c0 = hh * 4 * dk
        graw = main_ref[pl.ds(r0, chunk), c0 + 3 * dk:c0 + 4 * dk].astype(F32)
        ms = jnp.mean(o * o, axis=1, keepdims=True)
        y = o * lax.rsqrt(ms + RMS_EPS) * nw * _silu(graw)
        y_ref[pl.ds(r0, chunk), hh * dk:(hh + 1) * dk] = y.astype(y_ref.dtype)

    def body(ci, carry):
        r0 = pl.multiple_of(ci * chunk, chunk)
        for hh in range(hps):
            prologue(r0, hh)
        for bi in range(nsub):
            for hh in range(hps):
                sub_block(r0, hh, bi)
        for hh in range(hps):
            epilogue(r0, hh)
        return carry

    lax.fori_loop(0, nchunks, body, 0)


def _hgrn_recurrence(main, lb_param, norm_w, *, bsz, seq, heads, layer):
    t = main.shape[0]
    dk = HGRN_HEAD_DIM
    depth = lb_param.shape[0]
    rows = min(HGRN_ROWS_PER_STEP, seq)
    chunk = min(HGRN_CHUNK, rows)
    hps = HGRN_HEADS_PER_STEP
    assert seq % rows == 0 and rows % chunk == 0 and chunk % HGRN_SUB == 0 and heads % hps == 0
    nr = seq // rows
    return pl.pallas_call(
        functools.partial(_hgrn_kernel, layer=layer, chunk=chunk, sub=HGRN_SUB, nchunks=rows // chunk, hps=hps),
        out_shape=jax.ShapeDtypeStruct((t, heads * dk), BF16),
        grid=(bsz, heads // hps, nr),
        in_specs=[
            pl.BlockSpec((rows, hps * 4 * dk), lambda b, h, c: (b * nr + c, h)),
            pl.BlockSpec((depth, hps * dk), lambda b, h, c: (0, h)),
            pl.BlockSpec((1, dk), lambda b, h, c: (0, 0)),
        ],
        out_specs=pl.BlockSpec((rows, hps * dk), lambda b, h, c: (b * nr + c, h)),
        scratch_shapes=[pltpu.VMEM((hps, dk, dk), F32)] + [pltpu.VMEM((hps, chunk, dk), F32)] * 5,
        compiler_params=_cparams(("parallel", "parallel", "arbitrary")),
        name="hgrn_recurrence",
    )(main, lb_param, norm_w.reshape(1, dk))


def _hgrn_mixer(x, mod, w_in, lb_param, norm_w, w_out, ln_g, ln_b, *, bsz, seq, alpha, layer):
    d = x.shape[1]
    dk = HGRN_HEAD_DIM
    heads = d // dk
    w_perm = w_in.reshape(d, 4, heads, dk).transpose(0, 2, 1, 3).reshape(d, 4 * d).astype(BF16)
    main = _linear_mod(x, mod, w_perm, seq=seq, shift_row=0, scale_row=1, out_dtype=BF16,
                       tm=1024, tn=512, name="hgrn_in")
    y = _hgrn_recurrence(main, lb_param, norm_w, bsz=bsz, seq=seq, heads=heads, layer=layer)
    return _outproj_ln(y, w_out.astype(BF16), x, mod, ln_g, ln_b, seq=seq, gate_row=2, alpha=alpha,
                       tm=512, tk=2048, name="hgrn_out_ln")


def _conv_kernel(x_ref, halo_ref, w_ref, b_ref, o_ref, *, taps, steps_per_seq):
    first = (pl.program_id(0) % steps_per_seq) == 0
    x = x_ref[...].astype(F32)
    halo = jnp.where(first, 0.0, halo_ref[...].astype(F32))
    xc = jnp.concatenate([halo, x], axis=0)
    hr, tm = halo.shape[0], x.shape[0]
    acc = jnp.broadcast_to(b_ref[...], x.shape)
    for kk in range(taps):
        shift = taps - 1 - kk
        xs = xc if shift == 0 else pltpu.roll(xc, shift, 0)
        acc = acc + w_ref[kk:kk + 1, :] * xs[hr:hr + tm]
    o_ref[...] = _silu(acc).astype(o_ref.dtype)


def _ssd_conv(main, conv_w, conv_b, *, seq, col0):
    t = main.shape[0]
    taps, cdim = conv_w.shape
    tm = min(512, seq)
    tc = 512
    halo = 2 * SUBLANES
    assert seq % tm == 0 and cdim % tc == 0 and col0 % tc == 0 and taps - 1 <= halo and tm % halo == 0
    cb0 = col0 // tc
    hb = tm // halo
    return pl.pallas_call(
        functools.partial(_conv_kernel, taps=taps, steps_per_seq=seq // tm),
        out_shape=jax.ShapeDtypeStruct((t, cdim), BF16),
        grid=(t // tm, cdim // tc),
        in_specs=[
            pl.BlockSpec((tm, tc), lambda i, j: (i, cb0 + j)),
            pl.BlockSpec((halo, tc), lambda i, j: (jnp.maximum(i * hb - 1, 0), cb0 + j)),
            pl.BlockSpec((taps, tc), lambda i, j: (0, j)),
            pl.BlockSpec((1, tc), lambda i, j: (0, j)),
        ],
        out_specs=pl.BlockSpec((tm, tc), lambda i, j: (i, j)),
        compiler_params=_cparams(("parallel", "parallel")),
        name="ssd_conv",
    )(main, main, conv_w, conv_b.reshape(1, cdim))


def _ssd_kernel(xs_ref, b_ref, c_ref, z_ref, aux_ref, dtb_ref, alog_ref, dsk_ref, nw_ref, y_ref,
                state_ref, dt_ref, cum_ref, dt_t_ref, cum_t_ref, *, chunk, hpg, hd):
    cidx = pl.program_id(1)
    g = pl.program_id(2)

    @pl.when(jnp.logical_and(cidx == 0, g == 0))
    def _():
        state_ref[...] = jnp.zeros_like(state_ref)

    tri = _lower_tri(chunk)

    @pl.when(g == 0)
    def _():
        dt = _softplus(aux_ref[...] + dtb_ref[...])
        la = dt * (-jnp.exp(alog_ref[...]))
        cum = _dot_hi(tri.astype(F32), la)
        dt_ref[...] = dt
        cum_ref[...] = cum
        dt_t_ref[...] = dt.T
        cum_t_ref[...] = cum.T

    dt_nat = dt_ref[...]
    cum_nat = cum_ref[...]
    g8 = pl.multiple_of(g * hpg, hpg)
    cum_t = cum_t_ref[pl.ds(g8, hpg), :]
    dt_t = dt_t_ref[pl.ds(g8, hpg), :]
    lane = lax.broadcasted_iota(I32, cum_nat.shape, 1)

    xs = xs_ref[...].astype(F32)
    x_t = xs.T
    bm = b_ref[...]
    cm = c_ref[...]
    cb_t = _dot_nt(bm, cm)
    st = state_ref[g]
    inter_t = _dot_nt(st.astype(BF16), cm)
    last = cum_t[:, chunk - 1:chunk]
    w_t = jnp.exp(last - cum_t) * dt_t
    ecum_t = jnp.exp(cum_t)
    s_le_t = jnp.logical_not(tri) | (lax.broadcasted_iota(I32, (chunk, chunk), 0)
                                     == lax.broadcasted_iota(I32, (chunk, chunk), 1))

    y_parts = []
    xw_parts = []
    st_parts = []
    for j in range(hpg):
        col = g8 + j
        sel = lane == col
        cum_col = jnp.sum(jnp.where(sel, cum_nat, 0.0), axis=1, keepdims=True)
        dt_col = jnp.sum(jnp.where(sel, dt_nat, 0.0), axis=1, keepdims=True)
        cum_row = cum_t[j:j + 1, :]
        dec_t = jnp.exp(jnp.where(s_le_t, cum_row - cum_col, -jnp.inf))
        m_t = (cb_t * dec_t * dt_col).astype(BF16)
        x_j = x_t[j * hd:(j + 1) * hd, :]
        y_j = _dot(x_j.astype(BF16), m_t) + inter_t[j * hd:(j + 1) * hd, :] * ecum_t[j:j + 1, :]
        y_parts.append(y_j)
        xw_parts.append((x_j * w_t[j:j + 1, :]).astype(BF16))
        st_parts.append(st[j * hd:(j + 1) * hd, :] * jnp.exp(last[j:j + 1, :]))
    y_t = jnp.concatenate(y_parts, axis=0)
    xw_t = jnp.concatenate(xw_parts, axis=0)
    state_ref[g] = jnp.concatenate(st_parts, axis=0) + _dot(xw_t, bm)

    y = y_t.T + dsk_ref[...] * xs
    y = y * _silu(z_ref[...].astype(F32))
    ms = jnp.mean(y * y, axis=1, keepdims=True)
    y_ref[...] = (y * lax.rsqrt(ms + RMS_EPS) * nw_ref[...]).astype(y_ref.dtype)


def _ssd_recurrence(main, conv, aux, dt_bias, a_log, d_skip, norm_w, *, bsz, seq, d_inner, nstate):
    t = main.shape[0]
    groups = SSM_GROUPS
    gw = d_inner // groups
    hpg = gw // SSM_HEADDIM
    nheads = groups * hpg
    chunk = min(SSD_CHUNK, seq)
    assert seq % chunk == 0 and nstate == LANES and chunk == LANES and nheads <= LANES
    nc = seq // chunk
    row = lambda b, c, g: b * nc + c
    pad = lambda v: jnp.zeros((1, LANES), F32).at[0, :nheads].set(v)
    dsk = jnp.repeat(d_skip, SSM_HEADDIM).reshape(1, d_inner)
    b0 = d_inner // nstate
    return pl.pallas_call(
        functools.partial(_ssd_kernel, chunk=chunk, hpg=hpg, hd=SSM_HEADDIM),
        out_shape=jax.ShapeDtypeStruct((t, d_inner), BF16),
        grid=(bsz, nc, groups),
        in_specs=[
            pl.BlockSpec((chunk, gw), lambda b, c, g: (row(b, c, g), g)),
            pl.BlockSpec((chunk, nstate), lambda b, c, g: (row(b, c, g), b0 + g)),
            pl.BlockSpec((chunk, nstate), lambda b, c, g: (row(b, c, g), b0 + groups + g)),
            pl.BlockSpec((chunk, gw), lambda b, c, g: (row(b, c, g), g)),
            pl.BlockSpec((chunk, LANES), lambda b, c, g: (row(b, c, g), 0)),
            pl.BlockSpec((1, LANES), lambda b, c, g: (0, 0)),
            pl.BlockSpec((1, LANES), lambda b, c, g: (0, 0)),
            pl.BlockSpec((1, gw), lambda b, c, g: (0, g)),
            pl.BlockSpec((1, gw), lambda b, c, g: (0, g)),
        ],
        out_specs=pl.BlockSpec((chunk, gw), lambda b, c, g: (row(b, c, g), g)),
        scratch_shapes=[
            pltpu.VMEM((groups, gw, nstate), F32),
            pltpu.VMEM((chunk, LANES), F32),
            pltpu.VMEM((chunk, LANES), F32),
            pltpu.VMEM((LANES, chunk), F32),
            pltpu.VMEM((LANES, chunk), F32),
        ],
        compiler_params=_cparams(("parallel", "arbitrary", "arbitrary")),
        name="ssd_recurrence",
    )(conv, conv, conv, main, aux, pad(dt_bias), pad(a_log), dsk, norm_w.reshape(1, d_inner))


def _ssd_mixer(x, mod, w_in, conv_w, conv_b, dt_bias, a_log, d_skip, norm_w, w_out, ln_g, ln_b,
               *, bsz, seq, alpha):
    d = x.shape[1]
    d_inner = w_out.shape[0]
    cdim = conv_w.shape[-1]
    nheads = dt_bias.shape[0]
    nstate = (cdim - d_inner) // (2 * SSM_GROUPS)
    n_main = d_inner + cdim
    w_main = w_in[:, :n_main].astype(BF16)
    w_aux = jnp.zeros((d, LANES), BF16).at[:, :nheads].set(w_in[:, n_main:].astype(BF16))
    main = _linear_mod(x, mod, w_main, seq=seq, shift_row=0, scale_row=1, out_dtype=BF16,
                       tm=1024, tn=512, name="ssd_in_main")
    aux = _linear_mod(x, mod, w_aux, seq=seq, shift_row=0, scale_row=1, out_dtype=F32,
                      tm=1024, tn=LANES, name="ssd_in_dt")
    conv = _ssd_conv(main, conv_w.reshape(conv_w.shape[0], cdim), conv_b, seq=seq, col0=d_inner)
    y = _ssd_recurrence(main, conv, aux, dt_bias, a_log, d_skip, norm_w, bsz=bsz, seq=seq,
                        d_inner=d_inner, nstate=nstate)
    return _outproj_ln(y, w_out.astype(BF16), x, mod, ln_g, ln_b, seq=seq, gate_row=2, alpha=alpha,
                       tm=512, tk=2048, name="ssd_out_ln")


def _pack_pairs(lo, hi):
    lo_b = lax.bitcast_convert_type(lo.astype(BF16).astype(F32), U32) >> 16
    hi_b = lax.bitcast_convert_type(hi.astype(BF16).astype(F32), U32) & jnp.uint32(HI_HALF_MASK)
    return lo_b | hi_b


def _unpack_pairs(w):
    lo = lax.bitcast_convert_type(w << 16, F32)
    hi = lax.bitcast_convert_type(w & jnp.uint32(HI_HALF_MASK), F32)
    return lo, hi


def _store_packed_rows(dst_ref, v):
    rows, d = v.shape
    half = d // 2
    pr = half // LANES
    w = _pack_pairs(v[:, :half], v[:, half:])
    for c in range(pr):
        dst_ref[pl.ds(c, rows, stride=pr), :] = w[:, c * LANES:(c + 1) * LANES]


def _load_packed_rows(src_ref, rows, pr, base=0):
    return [_unpack_pairs(src_ref[pl.ds(base + c, rows, stride=pr), :]) for c in range(pr)]


def _router_kernel(x_ref, mod_ref, wr_ref, br_ref, h_ref, ti_ref, tg_ref, *, n_real):
    i = pl.program_id(0)

    @pl.when(i < n_real)
    def _():
        m = mod_ref[0]
        h = x_ref[...] * (1.0 + m[4:5, :]) + m[3:4, :]
        _store_packed_rows(h_ref, h)
        logits = _dot_hi(h, wr_ref[...]) + br_ref[...]
        lane = lax.broadcasted_iota(I32, logits.shape, 1)
        lane_f = lane.astype(F32)
        vals = []
        ti = jnp.zeros(logits.shape, I32)
        for kk in range(TOP_K):
            mx = jnp.max(logits, axis=1, keepdims=True)
            idx = jnp.min(jnp.where(logits == mx, lane_f, float(LANES)), axis=1, keepdims=True).astype(I32)
            vals.append(mx)
            ti = jnp.where(lane == kk, idx, ti)
            logits = jnp.where(lane == idx, -jnp.inf, logits)
        es = [jnp.exp(v - vals[0]) for v in vals]
        tot = es[0]
        for e in es[1:]:
            tot = tot + e
        tg = jnp.zeros(ti.shape, F32)
        for kk in range(TOP_K):
            tg = jnp.where(lane == kk, es[kk] / tot, tg)
        ti_ref[...] = ti
        tg_ref[...] = tg

    @pl.when(i >= n_real)
    def _():
        h_ref[...] = jnp.zeros_like(h_ref)
        ti_ref[...] = jnp.zeros_like(ti_ref)
        tg_ref[...] = jnp.zeros_like(tg_ref)


def _router(x, mod, w_router, b_router, *, seq, tm):
    t, d = x.shape
    ne = w_router.shape[1]
    tm = min(tm, seq)
    n_real = t // tm
    wr = jnp.zeros((d, LANES), F32).at[:, :ne].set(w_router)
    br = jnp.full((1, LANES), -1e30, F32).at[0, :ne].set(b_router)
    clamp = lambda i: jnp.minimum(i, n_real - 1)
    pr = d // (2 * LANES)
    return pl.pallas_call(
        functools.partial(_router_kernel, n_real=n_real),
        out_shape=(
            jax.ShapeDtypeStruct(((t + tm) * pr, LANES), U32),
            jax.ShapeDtypeStruct((t + tm, LANES), I32),
            jax.ShapeDtypeStruct((t + tm, LANES), F32),
        ),
        grid=(n_real + 1,),
        in_specs=[
            pl.BlockSpec((tm, d), lambda i: (clamp(i), 0)),
            pl.BlockSpec((1, 6, d), lambda i: ((clamp(i) * tm) // seq, 0, 0)),
            pl.BlockSpec((d, LANES), lambda i: (0, 0)),
            pl.BlockSpec((1, LANES), lambda i: (0, 0)),
        ],
        out_specs=(
            pl.BlockSpec((tm * pr, LANES), lambda i: (i, 0)),
            pl.BlockSpec((tm, LANES), lambda i: (i, 0)),
            pl.BlockSpec((tm, LANES), lambda i: (i, 0)),
        ),
        compiler_params=_cparams(("parallel",)),
        name="moe_router",
    )(x, mod, wr, br)


def _w1_prep_kernel(w_ref, p_ref, o_ref):
    o_ref[0] = _dot(w_ref[0].astype(BF16), p_ref[...]).astype(BF16)


def _prep_w1(w1):
    ne, d, f2 = w1.shape
    f = f2 // 2
    j = jnp.arange(f2, dtype=I32)
    src_col = jnp.where(j < f, 2 * j, 2 * (j - f) + 1)
    sel = (jnp.arange(f2, dtype=I32)[:, None] == src_col[None, :]).astype(BF16)
    tk = min(512, d)
    return pl.pallas_call(
        _w1_prep_kernel,
        out_shape=jax.ShapeDtypeStruct((ne, d, f2), BF16),
        grid=(ne, d // tk),
        in_specs=[
            pl.BlockSpec((1, tk, f2), lambda e, k: (e, k, 0)),
            pl.BlockSpec((f2, f2), lambda e, k: (0, 0)),
        ],
        out_specs=pl.BlockSpec((1, tk, f2), lambda e, k: (e, k, 0)),
        compiler_params=_cparams(("parallel", "parallel")),
        name="moe_w1_prep",
    )(w1, sel)


def _gather_rows_start(src_hbm, idx_ref, dst_ref, sem, *, count, pr):
    def body(r2, carry):
        for p in range(2):
            r = r2 * 2 + p
            src = src_hbm.at[pl.ds(pl.multiple_of(idx_ref[0, r], pr), pr)]
            dst = dst_ref.at[pl.ds(pl.multiple_of(r * pr, pr), pr)]
            pltpu.make_async_copy(src, dst, sem).start(priority=p)
        return carry

    lax.fori_loop(0, count // 2, body, 0, unroll=4)


def _gather_rows_wait(src_hbm, dst_ref, sem):
    pltpu.make_async_copy(src_hbm.at[pl.ds(0, dst_ref.shape[0])], dst_ref, sem).wait()


def _expert_kernel(be_ref, rows_ref, src_cur_ref, src_nxt_ref, h_hbm, w1_ref, b1_ref, w2_ref, b2_ref, y_ref,
                   xbuf, xs_ref, gsem, *, blk, nb, f):
    i = pl.program_id(0)
    slot = i % 2
    half = xs_ref.shape[1] // 2
    pr = half // LANES

    @pl.when(jnp.logical_and(i == 0, rows_ref[0] > 0))
    def _():
        _gather_rows_start(h_hbm, src_cur_ref, xbuf.at[0], gsem.at[0], count=blk, pr=pr)

    nxt = jnp.minimum(i + 1, nb - 1)

    @pl.when(jnp.logical_and(i + 1 < nb, rows_ref[nxt] > 0))
    def _():
        _gather_rows_start(h_hbm, src_nxt_ref, xbuf.at[1 - slot], gsem.at[1 - slot], count=blk, pr=pr)

    @pl.when(rows_ref[i] > 0)
    def _():
        _gather_rows_wait(h_hbm, xbuf.at[slot], gsem.at[slot])
        for c, (lo, hi) in enumerate(_load_packed_rows(xbuf.at[slot], blk, pr)):
            xs_ref[:, c * LANES:(c + 1) * LANES] = lo.astype(BF16)
            xs_ref[:, half + c * LANES:half + (c + 1) * LANES] = hi.astype(BF16)
        hh = _dot(xs_ref[...], w1_ref[0]) + b1_ref[0]
        glu = jnp.minimum(hh[:, :f], SWIGLU_LIMIT)
        lin = jnp.clip(hh[:, f:], -SWIGLU_LIMIT, SWIGLU_LIMIT)
        act = glu * _sigmoid(SWIGLU_ALPHA * glu) * (lin + 1.0)
        _store_packed_rows(y_ref, _dot(act.astype(BF16), w2_ref[0]) + b2_ref[0])

    @pl.when(rows_ref[i] == 0)
    def _():
        y_ref[...] = jnp.zeros_like(y_ref)


def _experts(h_rows, plan, w1p, b1p, w2b, b2r, *, e0, blk):
    block_e, block_rows, src_rows = plan
    nb = block_e.shape[0]
    f2 = w1p.shape[2]
    f = f2 // 2
    d = w2b.shape[2]
    pr = d // (2 * LANES)
    wmap = lambda i, be, br: (e0 + be[i], 0, 0)
    smem_blk = lambda fn: pl.BlockSpec((None, 1, blk), fn, memory_space=pltpu.SMEM)
    grid_spec = pltpu.PrefetchScalarGridSpec(
        num_scalar_prefetch=2,
        grid=(nb,),
        in_specs=[
            smem_blk(lambda i, be, br: (i, 0, 0)),
            smem_blk(lambda i, be, br: (jnp.minimum(i + 1, nb - 1), 0, 0)),
            pl.BlockSpec(memory_space=pl.ANY),
            pl.BlockSpec((1, d, f2), wmap),
            pl.BlockSpec((1, 1, f2), wmap),
            pl.BlockSpec((1, f, d), wmap),
            pl.BlockSpec((1, 1, d), wmap),
        ],
        out_specs=pl.BlockSpec((blk * pr, LANES), lambda i, be, br: (i, 0)),
        scratch_shapes=[
            pltpu.VMEM((2, blk * pr, LANES), U32),
            pltpu.VMEM((blk, d), BF16),
            pltpu.SemaphoreType.DMA((2,)),
        ],
    )
    src3 = src_rows.reshape(nb, 1, blk)
    return pl.pallas_call(
        functools.partial(_expert_kernel, blk=blk, nb=nb, f=f),
        out_shape=jax.ShapeDtypeStruct((nb * blk * pr, LANES), U32),
        grid_spec=grid_spec,
        compiler_params=_cparams(("arbitrary",)),
        name="moe_experts",
    )(block_e, block_rows, src3, src3, h_rows, w1p, b1p, w2b, b2r)


def _combine_ln_kernel(idx_cur_ref, idx_nxt_ref, y_hbm, tg_ref, x_ref, mod_ref, g_ref, b_ref, o_ref,
                       ybuf, sem, *, alpha, nt):
    i = pl.program_id(0)
    slot = i % 2
    tm, d = x_ref.shape
    pr = d // (2 * LANES)
    count = TOP_K * tm

    @pl.when(i == 0)
    def _():
        _gather_rows_start(y_hbm, idx_cur_ref, ybuf.at[0], sem.at[0], count=count, pr=pr)

    @pl.when(i + 1 < nt)
    def _():
        _gather_rows_start(y_hbm, idx_nxt_ref, ybuf.at[1 - slot], sem.at[1 - slot], count=count, pr=pr)

    _gather_rows_wait(y_hbm, ybuf.at[slot], sem.at[slot])
    tg = tg_ref[...]
    los, his = [None] * pr, [None] * pr
    for kk in range(TOP_K):
        gk = tg[:, kk:kk + 1]
        for c, (lo, hi) in enumerate(_load_packed_rows(ybuf.at[slot], tm, pr, base=kk * tm * pr)):
            los[c] = gk * lo if kk == 0 else los[c] + gk * lo
            his[c] = gk * hi if kk == 0 else his[c] + gk * hi
    acc = jnp.concatenate(los + his, axis=1)
    gate = mod_ref[0][5:6, :]
    o_ref[...] = _deepnorm_ln(x_ref[...], acc, gate, g_ref[...], b_ref[...], alpha)


def _combine_ln(y_rows, slot_rows, tg, x, mod, ln_g, ln_b, *, seq, alpha, tm):
    t, d = x.shape
    nt = t // tm
    pr = d // (2 * LANES)
    idx3 = slot_rows.reshape(nt, 1, TOP_K * tm)
    smem_blk = lambda fn: pl.BlockSpec((None, 1, TOP_K * tm), fn, memory_space=pltpu.SMEM)
    return pl.pallas_call(
        functools.partial(_combine_ln_kernel, alpha=alpha, nt=nt),
        out_shape=jax.ShapeDtypeStruct((t, d), F32),
        grid=(nt,),
        in_specs=[
            smem_blk(lambda i: (i, 0, 0)),
            smem_blk(lambda i: (jnp.minimum(i + 1, nt - 1), 0, 0)),
            pl.BlockSpec(memory_space=pl.ANY),
            pl.BlockSpec((tm, LANES), lambda i: (i, 0)),
            pl.BlockSpec((tm, d), lambda i: (i, 0)),
            pl.BlockSpec((1, 6, d), lambda i: ((i * tm) // seq, 0, 0)),
            pl.BlockSpec((1, d), lambda i: (0, 0)),
            pl.BlockSpec((1, d), lambda i: (0, 0)),
        ],
        out_specs=pl.BlockSpec((tm, d), lambda i: (i, 0)),
        scratch_shapes=[
            pltpu.VMEM((2, TOP_K * tm * pr, LANES), U32),
            pltpu.SemaphoreType.DMA((2,)),
        ],
        compiler_params=_cparams(("arbitrary",)),
        name="moe_combine_ln",
    )(idx3, idx3, y_rows, tg, x, mod, ln_g.reshape(1, d), ln_b.reshape(1, d))


def _dispatch_plan(top_i, *, t, ne, blk, row_sublanes, tm):
    n = t * TOP_K
    n_blocks = n // blk + ne
    n_pad = n_blocks * blk
    e_flat = top_i[:t, :TOP_K].reshape(n)
    e_sorted, order = lax.sort((e_flat, jnp.arange(n, dtype=I32)), num_keys=1, is_stable=True)
    counts = jnp.sum((e_flat[:, None] == jnp.arange(ne, dtype=I32)[None, :]).astype(I32), axis=0)
    starts = jnp.cumsum(counts) - counts
    padded = (counts + blk - 1) // blk * blk
    pad_ends = jnp.cumsum(padded)
    pad_starts = pad_ends - padded
    blk_start = jnp.arange(n_blocks, dtype=I32) * blk
    block_e = jnp.minimum(jnp.sum((pad_ends[None, :] <= blk_start[:, None]).astype(I32), axis=1), ne - 1)
    block_valid = (blk_start < pad_ends[-1]).astype(I32)
    e_slot = jnp.repeat(block_e, blk)
    v_slot = jnp.repeat(block_valid, blk)
    slot = jnp.arange(n_pad, dtype=I32)
    r = slot - pad_starts[e_slot]
    valid = jnp.logical_and(r < counts[e_slot], v_slot == 1)
    p = jnp.clip(starts[e_slot] + r, 0, n - 1)
    src_row = jnp.where(valid, order[p] // TOP_K, t)
    block_rows = jnp.clip(pad_starts[block_e] + counts[block_e] - blk_start, 0, blk) * block_valid
    slot_sorted = pad_starts[e_sorted] + jnp.arange(n, dtype=I32) - starts[e_sorted]
    _, slot_of = lax.sort((order, slot_sorted.astype(I32)), num_keys=1)
    slot_rows = slot_of.reshape(t // tm, tm, TOP_K).transpose(0, 2, 1).reshape(t // tm, TOP_K * tm)
    plan = (block_e, block_rows.astype(I32), (src_row * row_sublanes).astype(I32))
    return plan, (slot_rows * row_sublanes).astype(I32)


def _moe_ffn(x, mod, w_router, b_router, w1p, b1p, w2b, b2r, ln_g, ln_b, *, seq, alpha, e0):
    t, d = x.shape
    ne = w_router.shape[1]
    tm_r = min(512, seq)
    tm_c = min(256, seq)
    h_rows, top_i, top_g = _router(x, mod, w_router, b_router, seq=seq, tm=tm_r)
    plan, slot_rows = _dispatch_plan(top_i, t=t, ne=ne, blk=MOE_BLOCK, row_sublanes=d // (2 * LANES), tm=tm_c)
    y_rows = _experts(h_rows, plan, w1p, b1p, w2b, b2r, e0=e0, blk=MOE_BLOCK)
    return _combine_ln(y_rows, slot_rows, top_g, x, mod, ln_g, ln_b, seq=seq, alpha=alpha, tm=tm_c)


def kernel(x, c, ada_w, ada_b, ln_g, ln_b, mlstm_w_in, mlstm_b_gates, mlstm_norm_w, mlstm_w_out, hgrn_w_in, hgrn_lb_param, hgrn_norm_w, hgrn_w_out, ssm_w_in, ssm_conv_w, ssm_conv_b, ssm_dt_bias, ssm_a_log, ssm_d, ssm_norm_w, ssm_w_out, moe_w_router, moe_b_router, moe_w1, moe_b1, moe_w2, moe_b2):
    bsz, seq, d = x.shape
    depth = ada_w.shape[0]
    alpha = (2 * depth) ** 0.25
    mods = _ada_mods(c, ada_w, ada_b)
    ne, f2 = moe_w1.shape[1], moe_w1.shape[3]
    w1p = _prep_w1(moe_w1.reshape(depth * ne, d, f2))
    b1p = jnp.concatenate([moe_b1[..., 0::2], moe_b1[..., 1::2]], axis=-1).reshape(depth * ne, 1, f2)
    w2b = moe_w2.astype(BF16).reshape(depth * ne, f2 // 2, d)
    b2r = moe_b2.reshape(depth * ne, 1, d)
    xf = x.reshape(bsz * seq, d)
    for i in range(depth):
        mod = mods[i]
        kind, j = i % 3, i // 3
        if kind == 0:
            xf = _mlstm_mixer(xf, mod, mlstm_w_in[j], mlstm_b_gates[j], mlstm_norm_w[j], mlstm_w_out[j],
                              ln_g[i, 0], ln_b[i, 0], bsz=bsz, seq=seq, alpha=alpha)
        elif kind == 1:
            xf = _hgrn_mixer(xf, mod, hgrn_w_in[j], hgrn_lb_param, hgrn_norm_w[j], hgrn_w_out[j],
                             ln_g[i, 0], ln_b[i, 0], bsz=bsz, seq=seq, alpha=alpha, layer=i)
        else:
            xf = _ssd_mixer(xf, mod, ssm_w_in[j], ssm_conv_w[j], ssm_conv_b[j], ssm_dt_bias[j], ssm_a_log[j],
                            ssm_d[j], ssm_norm_w[j], ssm_w_out[j], ln_g[i, 0], ln_b[i, 0],
                            bsz=bsz, seq=seq, alpha=alpha)
        xf = _moe_ffn(xf, mod, moe_w_router[i], moe_b_router[i], w1p, b1p, w2b, b2r,
                      ln_g[i, 1], ln_b[i, 1], seq=seq, alpha=alpha, e0=i * ne)
    return xf.reshape(bsz, seq, d)
```

```python
import functools

import jax
import jax.numpy as jnp
from jax import lax
from jax.experimental import pallas as pl
from jax.experimental.pallas import tpu as pltpu

F32 = jnp.float32
BF16 = jnp.bfloat16
I32 = jnp.int32
U32 = jnp.uint32
HI_HALF_MASK = 0xFFFF0000

MLSTM_GATE_CAP = 15.0
HGRN_HEAD_DIM = 128
SSM_GROUPS = 8
SSM_HEADDIM = 64
TOP_K = 4
SWIGLU_ALPHA = 1.702
SWIGLU_LIMIT = 7.0
LN_EPS = 1e-5
RMS_EPS = 1e-6

LANES = 128
SUBLANES = 8
VMEM_LIMIT_BYTES = 56 * 1024 * 1024

MLSTM_CHUNK = 128
HGRN_CHUNK = 64
HGRN_SUB = 16
HGRN_ROWS_PER_STEP = 512
HGRN_HEADS_PER_STEP = 8
SSD_CHUNK = 128
SSD_GROUPS_PER_STEP = 2
MOE_BLOCK = 512

_HI = lax.Precision.HIGHEST


def _cparams(sem):
    return pltpu.CompilerParams(dimension_semantics=sem, vmem_limit_bytes=VMEM_LIMIT_BYTES)


def _dot(a, b):
    return jnp.dot(a, b, preferred_element_type=F32)


def _dot_nt(a, b):
    return lax.dot_general(a, b, (((1,), (1,)), ((), ())), preferred_element_type=F32)


def _dot_tn(a, b):
    return lax.dot_general(a, b, (((0,), (0,)), ((), ())), preferred_element_type=F32)


def _dot_hi(a, b):
    return jnp.dot(a, b, preferred_element_type=F32, precision=_HI)


def _sigmoid(x):
    return 1.0 / (1.0 + jnp.exp(-x))


def _silu(x):
    return x * _sigmoid(x)


def _log_sigmoid(x):
    return jnp.minimum(x, 0.0) - jnp.log1p(jnp.exp(-jnp.abs(x)))


def _softplus(x):
    return jnp.maximum(x, 0.0) + jnp.log1p(jnp.exp(-jnp.abs(x)))


def _lower_tri(n):
    r = lax.broadcasted_iota(I32, (n, n), 0)
    c = lax.broadcasted_iota(I32, (n, n), 1)
    return r >= c


def _ada_kernel(c_ref, w_ref, b_ref, o_ref):
    cond = _silu(c_ref[...])
    o_ref[0] = _dot(cond.astype(BF16), w_ref[0].astype(BF16)) + b_ref[0]


def _ada_mods(c, ada_w, ada_b):
    depth, d, n6 = ada_w.shape
    bsz = c.shape[0]
    rows = -(-bsz // SUBLANES) * SUBLANES
    c_pad = jnp.zeros((rows, d), F32).at[:bsz].set(c)
    tn = 1536 if n6 % 1536 == 0 else n6
    out = pl.pallas_call(
        _ada_kernel,
        out_shape=jax.ShapeDtypeStruct((depth, rows, n6), F32),
        grid=(depth, n6 // tn),
        in_specs=[
            pl.BlockSpec((rows, d), lambda l, j: (0, 0)),
            pl.BlockSpec((1, d, tn), lambda l, j: (l, 0, j)),
            pl.BlockSpec((1, 1, tn), lambda l, j: (l, 0, j)),
        ],
        out_specs=pl.BlockSpec((1, rows, tn), lambda l, j: (l, 0, j)),
        compiler_params=_cparams(("parallel", "parallel")),
        name="ada_mods",
    )(c_pad, ada_w, ada_b.reshape(depth, 1, n6))
    return out[:, :bsz].reshape(depth, bsz, 6, d)


def _linear_mod_kernel(x_ref, mod_ref, w_ref, o_ref, h_ref, *, shift_row, scale_row):
    @pl.when(pl.program_id(1) == 0)
    def _():
        m = mod_ref[0]
        h = x_ref[...] * (1.0 + m[scale_row:scale_row + 1, :]) + m[shift_row:shift_row + 1, :]
        h_ref[...] = h.astype(BF16)

    o_ref[...] = _dot(h_ref[...], w_ref[...]).astype(o_ref.dtype)


def _linear_mod(x, mod, w, *, seq, shift_row, scale_row, out_dtype, tm, tn, name):
    t, d = x.shape
    n = w.shape[1]
    tm = min(tm, seq)
    tn = min(tn, n)
    assert t % tm == 0 and seq % tm == 0 and n % tn == 0
    return pl.pallas_call(
        functools.partial(_linear_mod_kernel, shift_row=shift_row, scale_row=scale_row),
        out_shape=jax.ShapeDtypeStruct((t, n), out_dtype),
        grid=(t // tm, n // tn),
        in_specs=[
            pl.BlockSpec((tm, d), lambda i, j: (i, 0)),
            pl.BlockSpec((1, 6, d), lambda i, j: ((i * tm) // seq, 0, 0)),
            pl.BlockSpec((d, tn), lambda i, j: (0, j)),
        ],
        out_specs=pl.BlockSpec((tm, tn), lambda i, j: (i, j)),
        scratch_shapes=[pltpu.VMEM((tm, d), BF16)],
        compiler_params=_cparams(("parallel", "arbitrary")),
        name=name,
    )(x, mod, w)


def _deepnorm_ln(x, y, gate, g, b, alpha):
    v = alpha * x + (1.0 + gate) * y
    mu = jnp.mean(v, axis=-1, keepdims=True)
    dv = v - mu
    var = jnp.mean(dv * dv, axis=-1, keepdims=True)
    return dv * lax.rsqrt(var + LN_EPS) * g + b


def _outproj_ln_kernel(y_ref, w_ref, x_ref, mod_ref, g_ref, b_ref, o_ref, acc_ref, *, gate_row, nk, alpha):
    k = pl.program_id(1)
    part = _dot(y_ref[...], w_ref[...])

    def finish(acc):
        gate = mod_ref[0][gate_row:gate_row + 1, :]
        o_ref[...] = _deepnorm_ln(x_ref[...], acc, gate, g_ref[...], b_ref[...], alpha)

    if nk == 1:
        finish(part)
    else:
        @pl.when(k == 0)
        def _():
            acc_ref[...] = part

        @pl.when(jnp.logical_and(k > 0, k < nk - 1))
        def _():
            acc_ref[...] += part

        @pl.when(k == nk - 1)
        def _():
            finish(acc_ref[...] + part)


def _outproj_ln(y, w, x, mod, ln_g, ln_b, *, seq, gate_row, alpha, tm, tk, name):
    t, kdim = y.shape
    d = w.shape[1]
    tm = min(tm, seq)
    tk = min(tk, kdim)
    assert t % tm == 0 and seq % tm == 0 and kdim % tk == 0
    nk = kdim // tk
    return pl.pallas_call(
        functools.partial(_outproj_ln_kernel, gate_row=gate_row, nk=nk, alpha=alpha),
        out_shape=jax.ShapeDtypeStruct((t, d), F32),
        grid=(t // tm, nk),
        in_specs=[
            pl.BlockSpec((tm, tk), lambda i, k: (i, k)),
            pl.BlockSpec((tk, d), lambda i, k: (k, 0)),
            pl.BlockSpec((tm, d), lambda i, k: (i, 0)),
            pl.BlockSpec((1, 6, d), lambda i, k: ((i * tm) // seq, 0, 0)),
            pl.BlockSpec((1, d), lambda i, k: (0, 0)),
            pl.BlockSpec((1, d), lambda i, k: (0, 0)),
        ],
        out_specs=pl.BlockSpec((tm, d), lambda i, k: (i, 0)),
        scratch_shapes=[pltpu.VMEM((tm, d), F32)],
        compiler_params=_cparams(("parallel", "arbitrary")),
        name=name,
    )(y, w, x, mod, ln_g.reshape(1, d), ln_b.reshape(1, d))


def _mlstm_kernel(q_ref, k_ref, v_ref, o_ref, aux_ref, bias_ref, nw_ref, y_ref,
                  c_ref, n_ref, m_ref, *, heads, dqk, dv, chunk):
    @pl.when(pl.program_id(1) == 0)
    def _():
        c_ref[...] = jnp.zeros_like(c_ref)
        n_ref[...] = jnp.zeros_like(n_ref)
        m_ref[...] = jnp.zeros_like(m_ref)

    scale = dqk ** -0.5
    g = MLSTM_GATE_CAP * jnp.tanh((aux_ref[...] + bias_ref[...]) / MLSTM_GATE_CAP)
    lane = lax.broadcasted_iota(I32, g.shape, 1)
    logsig = _log_sigmoid(g)
    is_f = jnp.logical_and(lane >= heads, lane < 2 * heads)
    tri = _lower_tri(chunk)
    bcum = _dot_hi(tri.astype(F32), jnp.where(is_f, logsig, 0.0))
    g_t = g.T
    bcum_t = bcum.T

    for h in range(heads):
        ig_col = g[:, h:h + 1]
        ig_row = g_t[h:h + 1, :]
        b_col = bcum[:, heads + h:heads + h + 1]
        b_row = bcum_t[heads + h:heads + h + 1, :]
        m_prev = m_ref[h][0:1, 0:1]

        log_d = jnp.where(tri, b_col - b_row + ig_row, -jnp.inf)
        m_inter = b_col + m_prev
        m_t = jnp.maximum(jnp.max(log_d, axis=1, keepdims=True), m_inter)
        dmat = jnp.exp(log_d - m_t)

        q = q_ref[:, h * dqk:(h + 1) * dqk]
        k = k_ref[:, h * dqk:(h + 1) * dqk]
        v = v_ref[:, h * dv:(h + 1) * dv]
        scores = _dot_nt(q, k) * scale * dmat
        inter = jnp.exp(m_inter - m_t)
        c_st = c_ref[h]
        n_st = n_ref[h][0:1, :]
        num = _dot(scores.astype(BF16), v) + inter * (_dot_nt(q, c_st.astype(BF16)) * scale)
        qn = jnp.sum(q.astype(F32) * n_st, axis=1, keepdims=True) * scale
        den = jnp.sum(scores, axis=1, keepdims=True) + inter * qn
        hout = num * (1.0 / jnp.maximum(jnp.abs(den), jnp.exp(-m_t)))

        b_last = b_col[chunk - 1:chunk, :]
        log_w = b_last - b_col + ig_col
        m_new = jnp.maximum(b_last + m_prev, jnp.max(log_w, axis=0, keepdims=True))
        w = jnp.exp(log_w - m_new)
        decay = jnp.exp(b_last + m_prev - m_new)
        vw = (v.astype(F32) * w).astype(BF16)
        c_ref[h] = decay * c_st + _dot_tn(vw, k)
        n_new = decay * n_st + jnp.sum(k.astype(F32) * w, axis=0, keepdims=True)
        n_ref[h] = jnp.broadcast_to(n_new, n_ref.shape[1:])
        m_ref[h] = jnp.broadcast_to(m_new, m_ref.shape[1:])

        ms = jnp.mean(hout * hout, axis=1, keepdims=True)
        hn = hout * lax.rsqrt(ms + RMS_EPS) * nw_ref[:, h * dv:(h + 1) * dv]
        og = o_ref[:, h * dv:(h + 1) * dv].astype(F32)
        y_ref[:, h * dv:(h + 1) * dv] = (hn * _sigmoid(og)).astype(y_ref.dtype)


def _mlstm_recurrence(main, aux, bias, norm_w, *, bsz, seq, heads, dqk, dv):
    t = main.shape[0]
    nqk, nv = heads * dqk, heads * dv
    chunk = min(MLSTM_CHUNK, seq)
    nc = seq // chunk
    assert nv == 2 * nqk and seq % chunk == 0
    row = lambda b, c: b * nc + c
    return pl.pallas_call(
        functools.partial(_mlstm_kernel, heads=heads, dqk=dqk, dv=dv, chunk=chunk),
        out_shape=jax.ShapeDtypeStruct((t, nv), BF16),
        grid=(bsz, nc),
        in_specs=[
            pl.BlockSpec((chunk, nqk), lambda b, c: (row(b, c), 0)),
            pl.BlockSpec((chunk, nqk), lambda b, c: (row(b, c), 1)),
            pl.BlockSpec((chunk, nv), lambda b, c: (row(b, c), 1)),
            pl.BlockSpec((chunk, nv), lambda b, c: (row(b, c), 2)),
            pl.BlockSpec((chunk, LANES), lambda b, c: (row(b, c), 0)),
            pl.BlockSpec((1, LANES), lambda b, c: (0, 0)),
            pl.BlockSpec((1, nv), lambda b, c: (0, 0)),
        ],
        out_specs=pl.BlockSpec((chunk, nv), lambda b, c: (row(b, c), 0)),
        scratch_shapes=[
            pltpu.VMEM((heads, dv, dqk), F32),
            pltpu.VMEM((heads, SUBLANES, dqk), F32),
            pltpu.VMEM((heads, SUBLANES, LANES), F32),
        ],
        compiler_params=_cparams(("parallel", "arbitrary")),
        name="mlstm_recurrence",
    )(main, main, main, main, aux, bias, norm_w.reshape(1, nv))


def _mlstm_mixer(x, mod, w_in, b_gates, norm_w, w_out, ln_g, ln_b, *, bsz, seq, alpha):
    d = x.shape[1]
    heads = b_gates.shape[0] // 2
    nv = w_out.shape[0]
    nqk = (w_in.shape[1] - 2 * nv - 2 * heads) // 2
    n_main = 2 * nqk + 2 * nv
    w_main = w_in[:, :n_main].astype(BF16)
    w_aux = jnp.zeros((d, LANES), BF16).at[:, :2 * heads].set(w_in[:, n_main:].astype(BF16))
    bias = jnp.zeros((1, LANES), F32).at[0, :2 * heads].set(b_gates)
    main = _linear_mod(x, mod, w_main, seq=seq, shift_row=0, scale_row=1, out_dtype=BF16,
                       tm=1024, tn=512, name="mlstm_in_main")
    aux = _linear_mod(x, mod, w_aux, seq=seq, shift_row=0, scale_row=1, out_dtype=F32,
                      tm=1024, tn=LANES, name="mlstm_in_gates")
    y = _mlstm_recurrence(main, aux, bias, norm_w, bsz=bsz, seq=seq, heads=heads,
                          dqk=nqk // heads, dv=nv // heads)
    return _outproj_ln(y, w_out.astype(BF16), x, mod, ln_g, ln_b, seq=seq, gate_row=2, alpha=alpha,
                       tm=512, tk=2048, name="mlstm_out_ln")


def _hgrn_kernel(main_ref, lbp_ref, nw_ref, y_ref, st_ref, q_ref, gc_ref, g_ref, i_ref, o_ref,
                 *, layer, chunk, sub, nchunks, hps):
    dk = HGRN_HEAD_DIM

    @pl.when(pl.program_id(2) == 0)
    def _():
        st_ref[...] = jnp.zeros_like(st_ref)

    p = lbp_ref[...]
    e = jnp.exp(p - jnp.max(p, axis=0, keepdims=True))
    sm = e / jnp.sum(e, axis=0, keepdims=True)
    lb_all = jnp.zeros((1, hps * dk), F32)
    for r in range(1, layer + 1):
        lb_all = lb_all + sm[r:r + 1, :]
    tri = _lower_tri(chunk).astype(F32)
    nsub = chunk // sub
    half = sub // 2
    half_row = lax.broadcasted_iota(I32, (half, dk), 0)
    ones_sq = jnp.ones((dk, dk), BF16)
    nw = nw_ref[...]

    def prologue(r0, hh):
        lb = lb_all[:, hh * dk:(hh + 1) * dk]
        log_lb = jnp.log(lb)
        log_1m = jnp.log1p(-lb)
        c0 = hh * 4 * dk
        blk = main_ref[pl.ds(r0, chunk), c0:c0 + 3 * dk]
        qraw = blk[:, 0:dk].astype(F32)
        fraw = blk[:, dk:2 * dk].astype(F32)
        iv_b = blk[:, 2 * dk:3 * dk]
        q = _silu(qraw)
        lsig = _log_sigmoid(fraw)
        b2 = log_1m + lsig
        logf = jnp.maximum(log_lb, b2) + jnp.log1p(jnp.exp(-jnp.abs(log_lb - b2)))
        gcum = _dot_hi(tri, logf)
        gk = gcum - (log_1m + lsig - fraw)
        q_ref[hh] = q
        gc_ref[hh] = gcum
        g_ref[hh] = gk
        i_ref[hh] = iv_b.astype(F32)
        st = st_ref[hh]
        o_ref[hh] = _dot_nt((q * jnp.exp(gcum)).astype(BF16), st.astype(BF16))
        g_last = gcum[chunk - 1:chunk]
        st_ref[hh] = st * jnp.exp(g_last) + _dot_tn(iv_b, jnp.exp(g_last - gk).astype(BF16))

    def sub_block(r0, hh, bi):
        lo = bi * sub
        g_i = gc_ref[hh, lo:lo + sub, :]
        q_i = q_ref[hh, lo:lo + sub, :]
        acc = o_ref[hh, lo:lo + sub, :]
        if bi > 0:
            ref_row = gc_ref[hh, lo - 1:lo, :]
            qt = (q_i * jnp.exp(g_i - ref_row)).astype(BF16)
            kt = jnp.exp(ref_row - g_ref[hh, 0:lo, :]).astype(BF16)
            c0 = hh * 4 * dk
            iv_b = main_ref[pl.ds(r0, lo), c0 + 2 * dk:c0 + 3 * dk]
            acc = acc + _dot(_dot_nt(qt, kt).astype(BF16), iv_b)
        g_top, g_bot = g_i[:half], g_i[half:]
        q_top, q_bot = q_i[:half], q_i[half:]
        pws = []
        for s in range(sub):
            gs = g_ref[hh, lo + s:lo + s + 1, :]
            if s < half:
                pws.append(q_top * jnp.exp(jnp.where(half_row >= s, g_top - gs, -jnp.inf)))
                pws.append(q_bot * jnp.exp(g_bot - gs))
            else:
                pws.append(q_bot * jnp.exp(jnp.where(half_row >= s - half, g_bot - gs, -jnp.inf)))
        rs = _dot(jnp.concatenate(pws, axis=0).astype(BF16), ones_sq)
        top, bot = acc[:half], acc[half:]
        p = 0
        for s in range(sub):
            i_s = i_ref[hh, lo + s:lo + s + 1, :]
            if s < half:
                top = top + rs[p * half:(p + 1) * half] * i_s
                p += 1
            bot = bot + rs[p * half:(p + 1) * half] * i_s
            p += 1
        o_ref[hh, lo:lo + half, :] = top
        o_ref[hh, lo + half:lo + sub, :] = bot

    def epilogue(r0, hh):
        o = o_ref[hh]
        c0 = hh * 4 * dk
        graw = main_ref[pl.ds(r0, chunk), c0 + 3 * dk:c0 + 4 * dk].astype(F32)
        ms = jnp.mean(o * o, axis=1, keepdims=True)
        y = o * lax.rsqrt(ms + RMS_EPS) * nw * _silu(graw)
        y_ref[pl.ds(r0, chunk), hh * dk:(hh + 1) * dk] = y.astype(y_ref.dtype)

    def body(ci, carry):
        r0 = pl.multiple_of(ci * chunk, chunk)
        for hh in range(hps):
            prologue(r0, hh)
        for bi in range(nsub):
            for hh in range(hps):
                sub_block(r0, hh, bi)
        for hh in range(hps):
            epilogue(r0, hh)
        return carry

    lax.fori_loop(0, nchunks, body, 0)


def _hgrn_recurrence(main, lb_param, norm_w, *, bsz, seq, heads, layer):
    t = main.shape[0]
    dk = HGRN_HEAD_DIM
    depth = lb_param.shape[0]
    rows = min(HGRN_ROWS_PER_STEP, seq)
    chunk = min(HGRN_CHUNK, rows)
    hps = HGRN_HEADS_PER_STEP
    assert seq % rows == 0 and rows % chunk == 0 and chunk % HGRN_SUB == 0 and heads % hps == 0
    nr = seq // rows
    return pl.pallas_call(
        functools.partial(_hgrn_kernel, layer=layer, chunk=chunk, sub=HGRN_SUB, nchunks=rows // chunk, hps=hps),
        out_shape=jax.ShapeDtypeStruct((t, heads * dk), BF16),
        grid=(bsz, heads // hps, nr),
        in_specs=[
            pl.BlockSpec((rows, hps * 4 * dk), lambda b, h, c: (b * nr + c, h)),
            pl.BlockSpec((depth, hps * dk), lambda b, h, c: (0, h)),
            pl.BlockSpec((1, dk), lambda b, h, c: (0, 0)),
        ],
        out_specs=pl.BlockSpec((rows, hps * dk), lambda b, h, c: (b * nr + c, h)),
        scratch_shapes=[pltpu.VMEM((hps, dk, dk), F32)] + [pltpu.VMEM((hps, chunk, dk), F32)] * 5,
        compiler_params=_cparams(("parallel", "parallel", "arbitrary")),
        name="hgrn_recurrence",
    )(main, lb_param, norm_w.reshape(1, dk))


def _hgrn_mixer(x, mod, w_in, lb_param, norm_w, w_out, ln_g, ln_b, *, bsz, seq, alpha, layer):
    d = x.shape[1]
    dk = HGRN_HEAD_DIM
    heads = d // dk
    w_perm = w_in.reshape(d, 4, heads, dk).transpose(0, 2, 1, 3).reshape(d, 4 * d).astype(BF16)
    main = _linear_mod(x, mod, w_perm, seq=seq, shift_row=0, scale_row=1, out_dtype=BF16,
                       tm=1024, tn=512, name="hgrn_in")
    y = _hgrn_recurrence(main, lb_param, norm_w, bsz=bsz, seq=seq, heads=heads, layer=layer)
    return _outproj_ln(y, w_out.astype(BF16), x, mod, ln_g, ln_b, seq=seq, gate_row=2, alpha=alpha,
                       tm=512, tk=2048, name="hgrn_out_ln")


def _conv_kernel(x_ref, halo_ref, w_ref, b_ref, o_ref, *, taps, steps_per_seq):
    first = (pl.program_id(0) % steps_per_seq) == 0
    x = x_ref[...].astype(F32)
    halo = jnp.where(first, 0.0, halo_ref[...].astype(F32))
    xc = jnp.concatenate([halo, x], axis=0)
    hr, tm = halo.shape[0], x.shape[0]
    acc = jnp.broadcast_to(b_ref[...], x.shape)
    for kk in range(taps):
        shift = taps - 1 - kk
        xs = xc if shift == 0 else pltpu.roll(xc, shift, 0)
        acc = acc + w_ref[kk:kk + 1, :] * xs[hr:hr + tm]
    o_ref[...] = _silu(acc).astype(o_ref.dtype)


def _ssd_conv(main, conv_w, conv_b, *, seq, col0):
    t = main.shape[0]
    taps, cdim = conv_w.shape
    tm = min(512, seq)
    tc = 512
    halo = 2 * SUBLANES
    assert seq % tm == 0 and cdim % tc == 0 and col0 % tc == 0 and taps - 1 <= halo and tm % halo == 0
    cb0 = col0 // tc
    hb = tm // halo
    return pl.pallas_call(
        functools.partial(_conv_kernel, taps=taps, steps_per_seq=seq // tm),
        out_shape=jax.ShapeDtypeStruct((t, cdim), BF16),
        grid=(t // tm, cdim // tc),
        in_specs=[
            pl.BlockSpec((tm, tc), lambda i, j: (i, cb0 + j)),
            pl.BlockSpec((halo, tc), lambda i, j: (jnp.maximum(i * hb - 1, 0), cb0 + j)),
            pl.BlockSpec((taps, tc), lambda i, j: (0, j)),
            pl.BlockSpec((1, tc), lambda i, j: (0, j)),
        ],
        out_specs=pl.BlockSpec((tm, tc), lambda i, j: (i, j)),
        compiler_params=_cparams(("parallel", "parallel")),
        name="ssd_conv",
    )(main, main, conv_w, conv_b.reshape(1, cdim))


def _ssd_kernel(xs_ref, b_ref, c_ref, z_ref, aux_ref, dtb_ref, alog_ref, dsk_ref, nw_ref, y_ref,
                state_ref, acol_ref, dt_t_ref, cum_t_ref, *, chunk, hpg, hd, gps):
    cidx = pl.program_id(1)
    g = pl.program_id(2)

    @pl.when(jnp.logical_and(cidx == 0, g == 0))
    def _():
        state_ref[...] = jnp.zeros_like(state_ref)

    tri = _lower_tri(chunk)

    @pl.when(g == 0)
    def _():
        dt = _softplus(aux_ref[...] + dtb_ref[...])
        la = dt * (-jnp.exp(alog_ref[...]))
        cum = _dot_hi(tri.astype(F32), la)
        acol_ref[...] = jnp.log(dt) - cum
        dt_t_ref[...] = dt.T
        cum_t_ref[...] = cum.T

    acol_nat = acol_ref[...]
    lane = lax.broadcasted_iota(I32, acol_nat.shape, 1)
    s_le_t = jnp.logical_not(tri) | (lax.broadcasted_iota(I32, (chunk, chunk), 0)
                                     == lax.broadcasted_iota(I32, (chunk, chunk), 1))
    gw = hpg * hd
    nstate = b_ref.shape[1] // gps

    for gg in range(gps):
        gi = g * gps + gg
        g8 = pl.multiple_of(gi * hpg, hpg)
        cum_t = cum_t_ref[pl.ds(g8, hpg), :]
        dt_t = dt_t_ref[pl.ds(g8, hpg), :]
        xs = xs_ref[:, gg * gw:(gg + 1) * gw].astype(F32)
        x_t = xs.T
        bm = b_ref[:, gg * nstate:(gg + 1) * nstate]
        cm = c_ref[:, gg * nstate:(gg + 1) * nstate]
        cb_t = _dot_nt(bm, cm)
        st = state_ref[gi]
        inter_t = _dot_nt(st.astype(BF16), cm)
        last = cum_t[:, chunk - 1:chunk]
        w_t = jnp.exp(last - cum_t) * dt_t
        ecum_t = jnp.exp(cum_t)

        y_parts = []
        xw_parts = []
        st_parts = []
        for j in range(hpg):
            col = g8 + j
            a_col = jnp.sum(jnp.where(lane == col, acol_nat, 0.0), axis=1, keepdims=True)
            cum_row = cum_t[j:j + 1, :]
            m_t = (cb_t * jnp.exp(jnp.where(s_le_t, cum_row + a_col, -jnp.inf))).astype(BF16)
            x_j = x_t[j * hd:(j + 1) * hd, :]
            y_j = _dot(x_j.astype(BF16), m_t) + inter_t[j * hd:(j + 1) * hd, :] * ecum_t[j:j + 1, :]
            y_parts.append(y_j)
            xw_parts.append((x_j * w_t[j:j + 1, :]).astype(BF16))
            st_parts.append(st[j * hd:(j + 1) * hd, :] * jnp.exp(last[j:j + 1, :]))
        y_t = jnp.concatenate(y_parts, axis=0)
        xw_t = jnp.concatenate(xw_parts, axis=0)
        state_ref[gi] = jnp.concatenate(st_parts, axis=0) + _dot(xw_t, bm)

        y = y_t.T + dsk_ref[:, gg * gw:(gg + 1) * gw] * xs
        y = y * _silu(z_ref[:, gg * gw:(gg + 1) * gw].astype(F32))
        ms = jnp.mean(y * y, axis=1, keepdims=True)
        y_ref[:, gg * gw:(gg + 1) * gw] = (y * lax.rsqrt(ms + RMS_EPS) * nw_ref[:, gg * gw:(gg + 1) * gw]
                                           ).astype(y_ref.dtype)


def _ssd_recurrence(main, conv, aux, dt_bias, a_log, d_skip, norm_w, *, bsz, seq, d_inner, nstate):
    t = main.shape[0]
    groups = SSM_GROUPS
    gw = d_inner // groups
    hpg = gw // SSM_HEADDIM
    nheads = groups * hpg
    chunk = min(SSD_CHUNK, seq)
    assert seq % chunk == 0 and nstate == LANES and chunk == LANES and nheads <= LANES
    nc = seq // chunk
    row = lambda b, c, g: b * nc + c
    pad = lambda v: jnp.zeros((1, LANES), F32).at[0, :nheads].set(v)
    dsk = jnp.repeat(d_skip, SSM_HEADDIM).reshape(1, d_inner)
    gps = SSD_GROUPS_PER_STEP
    assert groups % gps == 0 and d_inner % (gps * nstate) == 0
    b0 = d_inner // (gps * nstate)
    c0 = b0 + groups // gps
    return pl.pallas_call(
        functools.partial(_ssd_kernel, chunk=chunk, hpg=hpg, hd=SSM_HEADDIM, gps=gps),
        out_shape=jax.ShapeDtypeStruct((t, d_inner), BF16),
        grid=(bsz, nc, groups // gps),
        in_specs=[
            pl.BlockSpec((chunk, gps * gw), lambda b, c, g: (row(b, c, g), g)),
            pl.BlockSpec((chunk, gps * nstate), lambda b, c, g: (row(b, c, g), b0 + g)),
            pl.BlockSpec((chunk, gps * nstate), lambda b, c, g: (row(b, c, g), c0 + g)),
            pl.BlockSpec((chunk, gps * gw), lambda b, c, g: (row(b, c, g), g)),
            pl.BlockSpec((chunk, LANES), lambda b, c, g: (row(b, c, g), 0)),
            pl.BlockSpec((1, LANES), lambda b, c, g: (0, 0)),
            pl.BlockSpec((1, LANES), lambda b, c, g: (0, 0)),
            pl.BlockSpec((1, gps * gw), lambda b, c, g: (0, g)),
            pl.BlockSpec((1, gps * gw), lambda b, c, g: (0, g)),
        ],
        out_specs=pl.BlockSpec((chunk, gps * gw), lambda b, c, g: (row(b, c, g), g)),
        scratch_shapes=[
            pltpu.VMEM((groups, gw, nstate), F32),
            pltpu.VMEM((chunk, LANES), F32),
            pltpu.VMEM((LANES, chunk), F32),
            pltpu.VMEM((LANES, chunk), F32),
        ],
        compiler_params=_cparams(("parallel", "arbitrary", "arbitrary")),
        name="ssd_recurrence",
    )(conv, conv, conv, main, aux, pad(dt_bias), pad(a_log), dsk, norm_w.reshape(1, d_inner))


def _ssd_mixer(x, mod, w_in, conv_w, conv_b, dt_bias, a_log, d_skip, norm_w, w_out, ln_g, ln_b,
               *, bsz, seq, alpha):
    d = x.shape[1]
    d_inner = w_out.shape[0]
    cdim = conv_w.shape[-1]
    nheads = dt_bias.shape[0]
    nstate = (cdim - d_inner) // (2 * SSM_GROUPS)
    n_main = d_inner + cdim
    w_main = w_in[:, :n_main].astype(BF16)
    w_aux = jnp.zeros((d, LANES), BF16).at[:, :nheads].set(w_in[:, n_main:].astype(BF16))
    main = _linear_mod(x, mod, w_main, seq=seq, shift_row=0, scale_row=1, out_dtype=BF16,
                       tm=1024, tn=512, name="ssd_in_main")
    aux = _linear_mod(x, mod, w_aux, seq=seq, shift_row=0, scale_row=1, out_dtype=F32,
                      tm=1024, tn=LANES, name="ssd_in_dt")
    conv = _ssd_conv(main, conv_w.reshape(conv_w.shape[0], cdim), conv_b, seq=seq, col0=d_inner)
    y = _ssd_recurrence(main, conv, aux, dt_bias, a_log, d_skip, norm_w, bsz=bsz, seq=seq,
                        d_inner=d_inner, nstate=nstate)
    return _outproj_ln(y, w_out.astype(BF16), x, mod, ln_g, ln_b, seq=seq, gate_row=2, alpha=alpha,
                       tm=512, tk=2048, name="ssd_out_ln")


def _pack_pairs(lo, hi):
    lo_b = lax.bitcast_convert_type(lo.astype(BF16).astype(F32), U32) >> 16
    hi_b = lax.bitcast_convert_type(hi.astype(BF16).astype(F32), U32) & jnp.uint32(HI_HALF_MASK)
    return lo_b | hi_b


def _unpack_pairs(w):
    lo = lax.bitcast_convert_type(w << 16, F32)
    hi = lax.bitcast_convert_type(w & jnp.uint32(HI_HALF_MASK), F32)
    return lo, hi


def _store_packed_rows(dst_ref, v):
    rows, d = v.shape
    half = d // 2
    pr = half // LANES
    w = _pack_pairs(v[:, :half], v[:, half:])
    for c in range(pr):
        dst_ref[pl.ds(c, rows, stride=pr), :] = w[:, c * LANES:(c + 1) * LANES]


def _load_packed_rows(src_ref, rows, pr, base=0):
    return [_unpack_pairs(src_ref[pl.ds(base + c, rows, stride=pr), :]) for c in range(pr)]


def _router_kernel(x_ref, mod_ref, wr_ref, br_ref, h_ref, ti_ref, tg_ref, *, n_real):
    i = pl.program_id(0)

    @pl.when(i < n_real)
    def _():
        m = mod_ref[0]
        h = x_ref[...] * (1.0 + m[4:5, :]) + m[3:4, :]
        _store_packed_rows(h_ref, h)
        logits = _dot_hi(h, wr_ref[...]) + br_ref[...]
        lane = lax.broadcasted_iota(I32, logits.shape, 1)
        lane_f = lane.astype(F32)
        vals = []
        ti = jnp.zeros(logits.shape, I32)
        for kk in range(TOP_K):
            mx = jnp.max(logits, axis=1, keepdims=True)
            idx = jnp.min(jnp.where(logits == mx, lane_f, float(LANES)), axis=1, keepdims=True).astype(I32)
            vals.append(mx)
            ti = jnp.where(lane == kk, idx, ti)
            logits = jnp.where(lane == idx, -jnp.inf, logits)
        es = [jnp.exp(v - vals[0]) for v in vals]
        tot = es[0]
        for e in es[1:]:
            tot = tot + e
        tg = jnp.zeros(ti.shape, F32)
        for kk in range(TOP_K):
            tg = jnp.where(lane == kk, es[kk] / tot, tg)
        ti_ref[...] = ti
        tg_ref[...] = tg

    @pl.when(i >= n_real)
    def _():
        h_ref[...] = jnp.zeros_like(h_ref)
        ti_ref[...] = jnp.zeros_like(ti_ref)
        tg_ref[...] = jnp.zeros_like(tg_ref)


def _router(x, mod, w_router, b_router, *, seq, tm):
    t, d = x.shape
    ne = w_router.shape[1]
    tm = min(tm, seq)
    n_real = t // tm
    wr = jnp.zeros((d, LANES), F32).at[:, :ne].set(w_router)
    br = jnp.full((1, LANES), -1e30, F32).at[0, :ne].set(b_router)
    clamp = lambda i: jnp.minimum(i, n_real - 1)
    pr = d // (2 * LANES)
    return pl.pallas_call(
        functools.partial(_router_kernel, n_real=n_real),
        out_shape=(
            jax.ShapeDtypeStruct(((t + tm) * pr, LANES), U32),
            jax.ShapeDtypeStruct((t + tm, LANES), I32),
            jax.ShapeDtypeStruct((t + tm, LANES), F32),
        ),
        grid=(n_real + 1,),
        in_specs=[
            pl.BlockSpec((tm, d), lambda i: (clamp(i), 0)),
            pl.BlockSpec((1, 6, d), lambda i: ((clamp(i) * tm) // seq, 0, 0)),
            pl.BlockSpec((d, LANES), lambda i: (0, 0)),
            pl.BlockSpec((1, LANES), lambda i: (0, 0)),
        ],
        out_specs=(
            pl.BlockSpec((tm * pr, LANES), lambda i: (i, 0)),
            pl.BlockSpec((tm, LANES), lambda i: (i, 0)),
            pl.BlockSpec((tm, LANES), lambda i: (i, 0)),
        ),
        compiler_params=_cparams(("parallel",)),
        name="moe_router",
    )(x, mod, wr, br)


def _w1_prep_kernel(w_ref, p_ref, o_ref):
    o_ref[0] = _dot(w_ref[0].astype(BF16), p_ref[...]).astype(BF16)


def _prep_w1(w1):
    ne, d, f2 = w1.shape
    f = f2 // 2
    j = jnp.arange(f2, dtype=I32)
    src_col = jnp.where(j < f, 2 * j, 2 * (j - f) + 1)
    sel = (jnp.arange(f2, dtype=I32)[:, None] == src_col[None, :]).astype(BF16)
    tk = min(512, d)
    return pl.pallas_call(
        _w1_prep_kernel,
        out_shape=jax.ShapeDtypeStruct((ne, d, f2), BF16),
        grid=(ne, d // tk),
        in_specs=[
            pl.BlockSpec((1, tk, f2), lambda e, k: (e, k, 0)),
            pl.BlockSpec((f2, f2), lambda e, k: (0, 0)),
        ],
        out_specs=pl.BlockSpec((1, tk, f2), lambda e, k: (e, k, 0)),
        compiler_params=_cparams(("parallel", "parallel")),
        name="moe_w1_prep",
    )(w1, sel)


def _gather_rows_start(src_hbm, idx_ref, dst_ref, sem, *, count, pr):
    def body(r2, carry):
        for p in range(2):
            r = r2 * 2 + p
            src = src_hbm.at[pl.ds(pl.multiple_of(idx_ref[0, r], pr), pr)]
            dst = dst_ref.at[pl.ds(pl.multiple_of(r * pr, pr), pr)]
            pltpu.make_async_copy(src, dst, sem).start(priority=p)
        return carry

    lax.fori_loop(0, count // 2, body, 0, unroll=4)


def _gather_rows_wait(src_hbm, dst_ref, sem):
    pltpu.make_async_copy(src_hbm.at[pl.ds(0, dst_ref.shape[0])], dst_ref, sem).wait()


def _expert_kernel(be_ref, rows_ref, src_cur_ref, src_nxt_ref, h_hbm, w1_ref, b1_ref, w2_ref, b2_ref, y_ref,
                   xbuf, xs_ref, gsem, *, blk, nb, f):
    i = pl.program_id(0)
    slot = i % 2
    half = xs_ref.shape[1] // 2
    pr = half // LANES

    @pl.when(jnp.logical_and(i == 0, rows_ref[0] > 0))
    def _():
        _gather_rows_start(h_hbm, src_cur_ref, xbuf.at[0], gsem.at[0], count=blk, pr=pr)

    nxt = jnp.minimum(i + 1, nb - 1)

    @pl.when(jnp.logical_and(i + 1 < nb, rows_ref[nxt] > 0))
    def _():
        _gather_rows_start(h_hbm, src_nxt_ref, xbuf.at[1 - slot], gsem.at[1 - slot], count=blk, pr=pr)

    @pl.when(rows_ref[i] > 0)
    def _():
        _gather_rows_wait(h_hbm, xbuf.at[slot], gsem.at[slot])
        for c, (lo, hi) in enumerate(_load_packed_rows(xbuf.at[slot], blk, pr)):
            xs_ref[:, c * LANES:(c + 1) * LANES] = lo.astype(BF16)
            xs_ref[:, half + c * LANES:half + (c + 1) * LANES] = hi.astype(BF16)
        hh = _dot(xs_ref[...], w1_ref[0]) + b1_ref[0]
        glu = jnp.minimum(hh[:, :f], SWIGLU_LIMIT)
        lin = jnp.clip(hh[:, f:], -SWIGLU_LIMIT, SWIGLU_LIMIT)
        act = glu * _sigmoid(SWIGLU_ALPHA * glu) * (lin + 1.0)
        _store_packed_rows(y_ref, _dot(act.astype(BF16), w2_ref[0]) + b2_ref[0])

    @pl.when(rows_ref[i] == 0)
    def _():
        y_ref[...] = jnp.zeros_like(y_ref)


def _experts(h_rows, plan, w1p, b1p, w2b, b2r, *, e0, blk):
    block_e, block_rows, src_rows = plan
    nb = block_e.shape[0]
    f2 = w1p.shape[2]
    f = f2 // 2
    d = w2b.shape[2]
    pr = d // (2 * LANES)
    wmap = lambda i, be, br: (e0 + be[i], 0, 0)
    smem_blk = lambda fn: pl.BlockSpec((None, 1, blk), fn, memory_space=pltpu.SMEM)
    grid_spec = pltpu.PrefetchScalarGridSpec(
        num_scalar_prefetch=2,
        grid=(nb,),
        in_specs=[
            smem_blk(lambda i, be, br: (i, 0, 0)),
            smem_blk(lambda i, be, br: (jnp.minimum(i + 1, nb - 1), 0, 0)),
            pl.BlockSpec(memory_space=pl.ANY),
            pl.BlockSpec((1, d, f2), wmap),
            pl.BlockSpec((1, 1, f2), wmap),
            pl.BlockSpec((1, f, d), wmap),
            pl.BlockSpec((1, 1, d), wmap),
        ],
        out_specs=pl.BlockSpec((blk * pr, LANES), lambda i, be, br: (i, 0)),
        scratch_shapes=[
            pltpu.VMEM((2, blk * pr, LANES), U32),
            pltpu.VMEM((blk, d), BF16),
            pltpu.SemaphoreType.DMA((2,)),
        ],
    )
    src3 = src_rows.reshape(nb, 1, blk)
    return pl.pallas_call(
        functools.partial(_expert_kernel, blk=blk, nb=nb, f=f),
        out_shape=jax.ShapeDtypeStruct((nb * blk * pr, LANES), U32),
        grid_spec=grid_spec,
        compiler_params=_cparams(("arbitrary",)),
        name="moe_experts",
    )(block_e, block_rows, src3, src3, h_rows, w1p, b1p, w2b, b2r)


def _combine_ln_kernel(idx_cur_ref, idx_nxt_ref, y_hbm, tg_ref, x_ref, mod_ref, g_ref, b_ref, o_ref,
                       ybuf, sem, *, alpha, nt):
    i = pl.program_id(0)
    slot = i % 2
    tm, d = x_ref.shape
    pr = d // (2 * LANES)
    count = TOP_K * tm

    @pl.when(i == 0)
    def _():
        _gather_rows_start(y_hbm, idx_cur_ref, ybuf.at[0], sem.at[0], count=count, pr=pr)

    @pl.when(i + 1 < nt)
    def _():
        _gather_rows_start(y_hbm, idx_nxt_ref, ybuf.at[1 - slot], sem.at[1 - slot], count=count, pr=pr)

    _gather_rows_wait(y_hbm, ybuf.at[slot], sem.at[slot])
    tg = tg_ref[...]
    los, his = [None] * pr, [None] * pr
    for kk in range(TOP_K):
        gk = tg[:, kk:kk + 1]
        for c, (lo, hi) in enumerate(_load_packed_rows(ybuf.at[slot], tm, pr, base=kk * tm * pr)):
            los[c] = gk * lo if kk == 0 else los[c] + gk * lo
            his[c] = gk * hi if kk == 0 else his[c] + gk * hi
    acc = jnp.concatenate(los + his, axis=1)
    gate = mod_ref[0][5:6, :]
    o_ref[...] = _deepnorm_ln(x_ref[...], acc, gate, g_ref[...], b_ref[...], alpha)


def _combine_ln(y_rows, slot_rows, tg, x, mod, ln_g, ln_b, *, seq, alpha, tm):
    t, d = x.shape
    nt = t // tm
    pr = d // (2 * LANES)
    idx3 = slot_rows.reshape(nt, 1, TOP_K * tm)
    smem_blk = lambda fn: pl.BlockSpec((None, 1, TOP_K * tm), fn, memory_space=pltpu.SMEM)
    return pl.pallas_call(
        functools.partial(_combine_ln_kernel, alpha=alpha, nt=nt),
        out_shape=jax.ShapeDtypeStruct((t, d), F32),
        grid=(nt,),
        in_specs=[
            smem_blk(lambda i: (i, 0, 0)),
            smem_blk(lambda i: (jnp.minimum(i + 1, nt - 1), 0, 0)),
            pl.BlockSpec(memory_space=pl.ANY),
            pl.BlockSpec((tm, LANES), lambda i: (i, 0)),
            pl.BlockSpec((tm, d), lambda i: (i, 0)),
            pl.BlockSpec((1, 6, d), lambda i: ((i * tm) // seq, 0, 0)),
            pl.BlockSpec((1, d), lambda i: (0, 0)),
            pl.BlockSpec((1, d), lambda i: (0, 0)),
        ],
        out_specs=pl.BlockSpec((tm, d), lambda i: (i, 0)),
        scratch_shapes=[
            pltpu.VMEM((2, TOP_K * tm * pr, LANES), U32),
            pltpu.SemaphoreType.DMA((2,)),
        ],
        compiler_params=_cparams(("arbitrary",)),
        name="moe_combine_ln",
    )(idx3, idx3, y_rows, tg, x, mod, ln_g.reshape(1, d), ln_b.reshape(1, d))


def _dispatch_plan(top_i, *, t, ne, blk, row_sublanes, tm):
    n = t * TOP_K
    n_blocks = n // blk + ne
    n_pad = n_blocks * blk
    e_flat = top_i[:t, :TOP_K].reshape(n)
    e_sorted, order = lax.sort((e_flat, jnp.arange(n, dtype=I32)), num_keys=1, is_stable=True)
    counts = jnp.sum((e_flat[:, None] == jnp.arange(ne, dtype=I32)[None, :]).astype(I32), axis=0)
    starts = jnp.cumsum(counts) - counts
    padded = (counts + blk - 1) // blk * blk
    pad_ends = jnp.cumsum(padded)
    pad_starts = pad_ends - padded
    blk_start = jnp.arange(n_blocks, dtype=I32) * blk
    block_e = jnp.minimum(jnp.sum((pad_ends[None, :] <= blk_start[:, None]).astype(I32), axis=1), ne - 1)
    block_valid = (blk_start < pad_ends[-1]).astype(I32)
    e_slot = jnp.repeat(block_e, blk)
    v_slot = jnp.repeat(block_valid, blk)
    slot = jnp.arange(n_pad, dtype=I32)
    r = slot - pad_starts[e_slot]
    valid = jnp.logical_and(r < counts[e_slot], v_slot == 1)
    p = jnp.clip(starts[e_slot] + r, 0, n - 1)
    src_row = jnp.where(valid, order[p] // TOP_K, t)
    block_rows = jnp.clip(pad_starts[block_e] + counts[block_e] - blk_start, 0, blk) * block_valid
    slot_sorted = pad_starts[e_sorted] + jnp.arange(n, dtype=I32) - starts[e_sorted]
    _, slot_of = lax.sort((order, slot_sorted.astype(I32)), num_keys=1)
    slot_rows = slot_of.reshape(t // tm, tm, TOP_K).transpose(0, 2, 1).reshape(t // tm, TOP_K * tm)
    plan = (block_e, block_rows.astype(I32), (src_row * row_sublanes).astype(I32))
    return plan, (slot_rows * row_sublanes).astype(I32)


def _moe_ffn(x, mod, w_router, b_router, w1p, b1p, w2b, b2r, ln_g, ln_b, *, seq, alpha, e0):
    t, d = x.shape
    ne = w_router.shape[1]
    tm_r = min(512, seq)
    tm_c = min(256, seq)
    h_rows, top_i, top_g = _router(x, mod, w_router, b_router, seq=seq, tm=tm_r)
    plan, slot_rows = _dispatch_plan(top_i, t=t, ne=ne, blk=MOE_BLOCK, row_sublanes=d // (2 * LANES), tm=tm_c)
    y_rows = _experts(h_rows, plan, w1p, b1p, w2b, b2r, e0=e0, blk=MOE_BLOCK)
    return _combine_ln(y_rows, slot_rows, top_g, x, mod, ln_g, ln_b, seq=seq, alpha=alpha, tm=tm_c)


def kernel(x, c, ada_w, ada_b, ln_g, ln_b, mlstm_w_in, mlstm_b_gates, mlstm_norm_w, mlstm_w_out, hgrn_w_in, hgrn_lb_param, hgrn_norm_w, hgrn_w_out, ssm_w_in, ssm_conv_w, ssm_conv_b, ssm_dt_bias, ssm_a_log, ssm_d, ssm_norm_w, ssm_w_out, moe_w_router, moe_b_router, moe_w1, moe_b1, moe_w2, moe_b2):
    bsz, seq, d = x.shape
    depth = ada_w.shape[0]
    alpha = (2 * depth) ** 0.25
    mods = _ada_mods(c, ada_w, ada_b)
    ne, f2 = moe_w1.shape[1], moe_w1.shape[3]
    w1p = _prep_w1(moe_w1.reshape(depth * ne, d, f2))
    b1p = jnp.concatenate([moe_b1[..., 0::2], moe_b1[..., 1::2]], axis=-1).reshape(depth * ne, 1, f2)
    w2b = moe_w2.astype(BF16).reshape(depth * ne, f2 // 2, d)
    b2r = moe_b2.reshape(depth * ne, 1, d)
    xf = x.reshape(bsz * seq, d)
    for i in range(depth):
        mod = mods[i]
        kind, j = i % 3, i // 3
        if kind == 0:
            xf = _mlstm_mixer(xf, mod, mlstm_w_in[j], mlstm_b_gates[j], mlstm_norm_w[j], mlstm_w_out[j],
                              ln_g[i, 0], ln_b[i, 0], bsz=bsz, seq=seq, alpha=alpha)
        elif kind == 1:
            xf = _hgrn_mixer(xf, mod, hgrn_w_in[j], hgrn_lb_param, hgrn_norm_w[j], hgrn_w_out[j],
                             ln_g[i, 0], ln_b[i, 0], bsz=bsz, seq=seq, alpha=alpha, layer=i)
        else:
            xf = _ssd_mixer(xf, mod, ssm_w_in[j], ssm_conv_w[j], ssm_conv_b[j], ssm_dt_bias[j], ssm_a_log[j],
                            ssm_d[j], ssm_norm_w[j], ssm_w_out[j], ln_g[i, 0], ln_b[i, 0],
                            bsz=bsz, seq=seq, alpha=alpha)
        xf = _moe_ffn(xf, mod, moe_w_router[i], moe_b_router[i], w1p, b1p, w2b, b2r,
                      ln_g[i, 1], ln_b[i, 1], seq=seq, alpha=alpha, e0=i * ne)
    return xf.reshape(bsz, seq, d)
```

```python
import functools

import jax
import jax.numpy as jnp
from jax import lax
from jax.experimental import pallas as pl
from jax.experimental.pallas import tpu as pltpu

F32 = jnp.float32
BF16 = jnp.bfloat16
I32 = jnp.int32
U32 = jnp.uint32
HI_HALF_MASK = 0xFFFF0000

MLSTM_GATE_CAP = 15.0
HGRN_HEAD_DIM = 128
SSM_GROUPS = 8
SSM_HEADDIM = 64
TOP_K = 4
SWIGLU_ALPHA = 1.702
SWIGLU_LIMIT = 7.0
LN_EPS = 1e-5
RMS_EPS = 1e-6

LANES = 128
SUBLANES = 8
VMEM_LIMIT_BYTES = 56 * 1024 * 1024

MLSTM_CHUNK = 128
HGRN_CHUNK = 64
HGRN_SUB = 16
HGRN_ROWS_PER_STEP = 512
HGRN_HEADS_PER_STEP = 8
SSD_CHUNK = 128
SSD_GROUPS_PER_STEP = 2
MOE_BLOCK = 512

_HI = lax.Precision.HIGHEST


def _cparams(sem):
    return pltpu.CompilerParams(dimension_semantics=sem, vmem_limit_bytes=VMEM_LIMIT_BYTES)


def _dot(a, b):
    return jnp.dot(a, b, preferred_element_type=F32)


def _dot_nt(a, b):
    return lax.dot_general(a, b, (((1,), (1,)), ((), ())), preferred_element_type=F32)


def _dot_tn(a, b):
    return lax.dot_general(a, b, (((0,), (0,)), ((), ())), preferred_element_type=F32)


def _dot_hi(a, b):
    return jnp.dot(a, b, preferred_element_type=F32, precision=_HI)


def _sigmoid(x):
    return 1.0 / (1.0 + jnp.exp(-x))


def _silu(x):
    return x * _sigmoid(x)


def _log_sigmoid(x):
    return jnp.minimum(x, 0.0) - jnp.log1p(jnp.exp(-jnp.abs(x)))


def _softplus(x):
    return jnp.maximum(x, 0.0) + jnp.log1p(jnp.exp(-jnp.abs(x)))


def _lower_tri(n):
    r = lax.broadcasted_iota(I32, (n, n), 0)
    c = lax.broadcasted_iota(I32, (n, n), 1)
    return r >= c


def _ada_kernel(c_ref, w_ref, b_ref, o_ref):
    cond = _silu(c_ref[...])
    o_ref[0] = _dot(cond.astype(BF16), w_ref[0].astype(BF16)) + b_ref[0]


def _ada_mods(c, ada_w, ada_b):
    depth, d, n6 = ada_w.shape
    bsz = c.shape[0]
    rows = -(-bsz // SUBLANES) * SUBLANES
    c_pad = jnp.zeros((rows, d), F32).at[:bsz].set(c)
    tn = 1536 if n6 % 1536 == 0 else n6
    out = pl.pallas_call(
        _ada_kernel,
        out_shape=jax.ShapeDtypeStruct((depth, rows, n6), F32),
        grid=(depth, n6 // tn),
        in_specs=[
            pl.BlockSpec((rows, d), lambda l, j: (0, 0)),
            pl.BlockSpec((1, d, tn), lambda l, j: (l, 0, j)),
            pl.BlockSpec((1, 1, tn), lambda l, j: (l, 0, j)),
        ],
        out_specs=pl.BlockSpec((1, rows, tn), lambda l, j: (l, 0, j)),
        compiler_params=_cparams(("parallel", "parallel")),
        name="ada_mods",
    )(c_pad, ada_w, ada_b.reshape(depth, 1, n6))
    return out[:, :bsz].reshape(depth, bsz, 6, d)


def _linear_mod_kernel(x_ref, mod_ref, w_ref, o_ref, h_ref, *, shift_row, scale_row):
    @pl.when(pl.program_id(1) == 0)
    def _():
        m = mod_ref[0]
        h = x_ref[...] * (1.0 + m[scale_row:scale_row + 1, :]) + m[shift_row:shift_row + 1, :]
        h_ref[...] = h.astype(BF16)

    o_ref[...] = _dot(h_ref[...], w_ref[...]).astype(o_ref.dtype)


def _linear_mod(x, mod, w, *, seq, shift_row, scale_row, out_dtype, tm, tn, name):
    t, d = x.shape
    n = w.shape[1]
    tm = min(tm, seq)
    tn = min(tn, n)
    assert t % tm == 0 and seq % tm == 0 and n % tn == 0
    return pl.pallas_call(
        functools.partial(_linear_mod_kernel, shift_row=shift_row, scale_row=scale_row),
        out_shape=jax.ShapeDtypeStruct((t, n), out_dtype),
        grid=(t // tm, n // tn),
        in_specs=[
            pl.BlockSpec((tm, d), lambda i, j: (i, 0)),
            pl.BlockSpec((1, 6, d), lambda i, j: ((i * tm) // seq, 0, 0)),
            pl.BlockSpec((d, tn), lambda i, j: (0, j)),
        ],
        out_specs=pl.BlockSpec((tm, tn), lambda i, j: (i, j)),
        scratch_shapes=[pltpu.VMEM((tm, d), BF16)],
        compiler_params=_cparams(("parallel", "arbitrary")),
        name=name,
    )(x, mod, w)


def _deepnorm_ln(x, y, gate, g, b, alpha):
    v = alpha * x + (1.0 + gate) * y
    mu = jnp.mean(v, axis=-1, keepdims=True)
    dv = v - mu
    var = jnp.mean(dv * dv, axis=-1, keepdims=True)
    return dv * lax.rsqrt(var + LN_EPS) * g + b


def _outproj_ln_kernel(y_ref, w_ref, x_ref, mod_ref, g_ref, b_ref, o_ref, acc_ref, *, gate_row, nk, alpha):
    k = pl.program_id(1)
    part = _dot(y_ref[...], w_ref[...])

    def finish(acc):
        gate = mod_ref[0][gate_row:gate_row + 1, :]
        o_ref[...] = _deepnorm_ln(x_ref[...], acc, gate, g_ref[...], b_ref[...], alpha)

    if nk == 1:
        finish(part)
    else:
        @pl.when(k == 0)
        def _():
            acc_ref[...] = part

        @pl.when(jnp.logical_and(k > 0, k < nk - 1))
        def _():
            acc_ref[...] += part

        @pl.when(k == nk - 1)
        def _():
            finish(acc_ref[...] + part)


def _outproj_ln(y, w, x, mod, ln_g, ln_b, *, seq, gate_row, alpha, tm, tk, name):
    t, kdim = y.shape
    d = w.shape[1]
    tm = min(tm, seq)
    tk = min(tk, kdim)
    assert t % tm == 0 and seq % tm == 0 and kdim % tk == 0
    nk = kdim // tk
    return pl.pallas_call(
        functools.partial(_outproj_ln_kernel, gate_row=gate_row, nk=nk, alpha=alpha),
        out_shape=jax.ShapeDtypeStruct((t, d), F32),
        grid=(t // tm, nk),
        in_specs=[
            pl.BlockSpec((tm, tk), lambda i, k: (i, k)),
            pl.BlockSpec((tk, d), lambda i, k: (k, 0)),
            pl.BlockSpec((tm, d), lambda i, k: (i, 0)),
            pl.BlockSpec((1, 6, d), lambda i, k: ((i * tm) // seq, 0, 0)),
            pl.BlockSpec((1, d), lambda i, k: (0, 0)),
            pl.BlockSpec((1, d), lambda i, k: (0, 0)),
        ],
        out_specs=pl.BlockSpec((tm, d), lambda i, k: (i, 0)),
        scratch_shapes=[pltpu.VMEM((tm, d), F32)],
        compiler_params=_cparams(("parallel", "arbitrary")),
        name=name,
    )(y, w, x, mod, ln_g.reshape(1, d), ln_b.reshape(1, d))


def _mlstm_kernel(q_ref, k_ref, v_ref, o_ref, aux_ref, bias_ref, nw_ref, y_ref,
                  c_ref, n_ref, m_ref, *, heads, dqk, dv, chunk):
    @pl.when(pl.program_id(1) == 0)
    def _():
        c_ref[...] = jnp.zeros_like(c_ref)
        n_ref[...] = jnp.zeros_like(n_ref)
        m_ref[...] = jnp.zeros_like(m_ref)

    scale = dqk ** -0.5
    g = MLSTM_GATE_CAP * jnp.tanh((aux_ref[...] + bias_ref[...]) / MLSTM_GATE_CAP)
    lane = lax.broadcasted_iota(I32, g.shape, 1)
    logsig = _log_sigmoid(g)
    is_f = jnp.logical_and(lane >= heads, lane < 2 * heads)
    tri = _lower_tri(chunk)
    bcum = _dot_hi(tri.astype(F32), jnp.where(is_f, logsig, 0.0))
    g_t = g.T
    bcum_t = bcum.T

    for h in range(heads):
        ig_col = g[:, h:h + 1]
        ig_row = g_t[h:h + 1, :]
        b_col = bcum[:, heads + h:heads + h + 1]
        b_row = bcum_t[heads + h:heads + h + 1, :]
        m_prev = m_ref[h][0:1, 0:1]

        log_d = jnp.where(tri, b_col - b_row + ig_row, -jnp.inf)
        m_inter = b_col + m_prev
        m_t = jnp.maximum(jnp.max(log_d, axis=1, keepdims=True), m_inter)
        dmat = jnp.exp(log_d - m_t)

        q = q_ref[:, h * dqk:(h + 1) * dqk]
        k = k_ref[:, h * dqk:(h + 1) * dqk]
        v = v_ref[:, h * dv:(h + 1) * dv]
        scores = _dot_nt(q, k) * scale * dmat
        inter = jnp.exp(m_inter - m_t)
        c_st = c_ref[h]
        n_st = n_ref[h][0:1, :]
        num = _dot(scores.astype(BF16), v) + inter * (_dot_nt(q, c_st.astype(BF16)) * scale)
        qn = jnp.sum(q.astype(F32) * n_st, axis=1, keepdims=True) * scale
        den = jnp.sum(scores, axis=1, keepdims=True) + inter * qn
        hout = num * (1.0 / jnp.maximum(jnp.abs(den), jnp.exp(-m_t)))

        b_last = b_col[chunk - 1:chunk, :]
        log_w = b_last - b_col + ig_col
        m_new = jnp.maximum(b_last + m_prev, jnp.max(log_w, axis=0, keepdims=True))
        w = jnp.exp(log_w - m_new)
        decay = jnp.exp(b_last + m_prev - m_new)
        vw = (v.astype(F32) * w).astype(BF16)
        c_ref[h] = decay * c_st + _dot_tn(vw, k)
        n_new = decay * n_st + jnp.sum(k.astype(F32) * w, axis=0, keepdims=True)
        n_ref[h] = jnp.broadcast_to(n_new, n_ref.shape[1:])
        m_ref[h] = jnp.broadcast_to(m_new, m_ref.shape[1:])

        ms = jnp.mean(hout * hout, axis=1, keepdims=True)
        hn = hout * lax.rsqrt(ms + RMS_EPS) * nw_ref[:, h * dv:(h + 1) * dv]
        og = o_ref[:, h * dv:(h + 1) * dv].astype(F32)
        y_ref[:, h * dv:(h + 1) * dv] = (hn * _sigmoid(og)).astype(y_ref.dtype)


def _mlstm_recurrence(main, aux, bias, norm_w, *, bsz, seq, heads, dqk, dv):
    t = main.shape[0]
    nqk, nv = heads * dqk, heads * dv
    chunk = min(MLSTM_CHUNK, seq)
    nc = seq // chunk
    assert nv == 2 * nqk and seq % chunk == 0
    row = lambda b, c: b * nc + c
    return pl.pallas_call(
        functools.partial(_mlstm_kernel, heads=heads, dqk=dqk, dv=dv, chunk=chunk),
        out_shape=jax.ShapeDtypeStruct((t, nv), BF16),
        grid=(bsz, nc),
        in_specs=[
            pl.BlockSpec((chunk, nqk), lambda b, c: (row(b, c), 0)),
            pl.BlockSpec((chunk, nqk), lambda b, c: (row(b, c), 1)),
            pl.BlockSpec((chunk, nv), lambda b, c: (row(b, c), 1)),
            pl.BlockSpec((chunk, nv), lambda b, c: (row(b, c), 2)),
            pl.BlockSpec((chunk, LANES), lambda b, c: (row(b, c), 0)),
            pl.BlockSpec((1, LANES), lambda b, c: (0, 0)),
            pl.BlockSpec((1, nv), lambda b, c: (0, 0)),
        ],
        out_specs=pl.BlockSpec((chunk, nv), lambda b, c: (row(b, c), 0)),
        scratch_shapes=[
            pltpu.VMEM((heads, dv, dqk), F32),
            pltpu.VMEM((heads, SUBLANES, dqk), F32),
            pltpu.VMEM((heads, SUBLANES, LANES), F32),
        ],
        compiler_params=_cparams(("parallel", "arbitrary")),
        name="mlstm_recurrence",
    )(main, main, main, main, aux, bias, norm_w.reshape(1, nv))


def _mlstm_mixer(x, mod, w_in, b_gates, norm_w, w_out, ln_g, ln_b, *, bsz, seq, alpha):
    d = x.shape[1]
    heads = b_gates.shape[0] // 2
    nv = w_out.shape[0]
    nqk = (w_in.shape[1] - 2 * nv - 2 * heads) // 2
    n_main = 2 * nqk + 2 * nv
    w_main = w_in[:, :n_main].astype(BF16)
    w_aux = jnp.zeros((d, LANES), BF16).at[:, :2 * heads].set(w_in[:, n_main:].astype(BF16))
    bias = jnp.zeros((1, LANES), F32).at[0, :2 * heads].set(b_gates)
    main = _linear_mod(x, mod, w_main, seq=seq, shift_row=0, scale_row=1, out_dtype=BF16,
                       tm=1024, tn=1024, name="mlstm_in_main")
    aux = _linear_mod(x, mod, w_aux, seq=seq, shift_row=0, scale_row=1, out_dtype=F32,
                      tm=1024, tn=LANES, name="mlstm_in_gates")
    y = _mlstm_recurrence(main, aux, bias, norm_w, bsz=bsz, seq=seq, heads=heads,
                          dqk=nqk // heads, dv=nv // heads)
    return _outproj_ln(y, w_out.astype(BF16), x, mod, ln_g, ln_b, seq=seq, gate_row=2, alpha=alpha,
                       tm=512, tk=2048, name="mlstm_out_ln")


def _hgrn_kernel(main_ref, lbp_ref, nw_ref, y_ref, st_ref, q_ref, gc_ref, g_ref, i_ref, o_ref,
                 *, layer, chunk, sub, nchunks, hps):
    dk = HGRN_HEAD_DIM

    @pl.when(pl.program_id(2) == 0)
    def _():
        st_ref[...] = jnp.zeros_like(st_ref)

    p = lbp_ref[...]
    e = jnp.exp(p - jnp.max(p, axis=0, keepdims=True))
    sm = e / jnp.sum(e, axis=0, keepdims=True)
    lb_all = jnp.zeros((1, hps * dk), F32)
    for r in range(1, layer + 1):
        lb_all = lb_all + sm[r:r + 1, :]
    tri = _lower_tri(chunk).astype(F32)
    nsub = chunk // sub
    half = sub // 2
    half_row = lax.broadcasted_iota(I32, (half, dk), 0)
    ones_sq = jnp.ones((dk, dk), BF16)
    nw = nw_ref[...]

    def prologue(r0, hh):
        lb = lb_all[:, hh * dk:(hh + 1) * dk]
        log_lb = jnp.log(lb)
        log_1m = jnp.log1p(-lb)
        c0 = hh * 4 * dk
        blk = main_ref[pl.ds(r0, chunk), c0:c0 + 3 * dk]
        qraw = blk[:, 0:dk].astype(F32)
        fraw = blk[:, dk:2 * dk].astype(F32)
        iv_b = blk[:, 2 * dk:3 * dk]
        q = _silu(qraw)
        lsig = _log_sigmoid(fraw)
        b2 = log_1m + lsig
        logf = jnp.maximum(log_lb, b2) + jnp.log1p(jnp.exp(-jnp.abs(log_lb - b2)))
        gcum = _dot_hi(tri, logf)
        gk = gcum - (log_1m + lsig - fraw)
        q_ref[hh] = q
        gc_ref[hh] = gcum
        g_ref[hh] = gk
        i_ref[hh] = iv_b.astype(F32)
        st = st_ref[hh]
        o_ref[hh] = _dot_nt((q * jnp.exp(gcum)).astype(BF16), st.astype(BF16))
        g_last = gcum[chunk - 1:chunk]
        st_ref[hh] = st * jnp.exp(g_last) + _dot_tn(iv_b, jnp.exp(g_last - gk).astype(BF16))

    def sub_block(r0, hh, bi):
        lo = bi * sub
        g_i = gc_ref[hh, lo:lo + sub, :]
        q_i = q_ref[hh, lo:lo + sub, :]
        acc = o_ref[hh, lo:lo + sub, :]
        if bi > 0:
            ref_row = gc_ref[hh, lo - 1:lo, :]
            qt = (q_i * jnp.exp(g_i - ref_row)).astype(BF16)
            kt = jnp.exp(ref_row - g_ref[hh, 0:lo, :]).astype(BF16)
            c0 = hh * 4 * dk
            iv_b = main_ref[pl.ds(r0, lo), c0 + 2 * dk:c0 + 3 * dk]
            acc = acc + _dot(_dot_nt(qt, kt).astype(BF16), iv_b)
        g_top, g_bot = g_i[:half], g_i[half:]
        q_top, q_bot = q_i[:half], q_i[half:]
        pws = []
        for s in range(sub):
            gs = g_ref[hh, lo + s:lo + s + 1, :]
            if s < half:
                pws.append(q_top * jnp.exp(jnp.where(half_row >= s, g_top - gs, -jnp.inf)))
                pws.append(q_bot * jnp.exp(g_bot - gs))
            else:
                pws.append(q_bot * jnp.exp(jnp.where(half_row >= s - half, g_bot - gs, -jnp.inf)))
        rs = _dot(jnp.concatenate(pws, axis=0).astype(BF16), ones_sq)
        top, bot = acc[:half], acc[half:]
        p = 0
        for s in range(sub):
            i_s = i_ref[hh, lo + s:lo + s + 1, :]
            if s < half:
                top = top + rs[p * half:(p + 1) * half] * i_s
                p += 1
            bot = bot + rs[p * half:(p + 1) * half] * i_s
            p += 1
        o_ref[hh, lo:lo + half, :] = top
        o_ref[hh, lo + half:lo + sub, :] = bot

    def epilogue(r0, hh):
        o = o_ref[hh]
        c0 = hh * 4 * dk
        graw = main_ref[pl.ds(r0, chunk), c0 + 3 * dk:c0 + 4 * dk].astype(F32)
        ms = jnp.mean(o * o, axis=1, keepdims=True)
        y = o * lax.rsqrt(ms + RMS_EPS) * nw * _silu(graw)
        y_ref[pl.ds(r0, chunk), hh * dk:(hh + 1) * dk] = y.astype(y_ref.dtype)

    def body(ci, carry):
        r0 = pl.multiple_of(ci * chunk, chunk)
        for hh in range(hps):
            prologue(r0, hh)
        for bi in range(nsub):
            for hh in range(hps):
                sub_block(r0, hh, bi)
        for hh in range(hps):
            epilogue(r0, hh)
        return carry

    lax.fori_loop(0, nchunks, body, 0)


def _hgrn_recurrence(main, lb_param, norm_w, *, bsz, seq, heads, layer):
    t = main.shape[0]
    dk = HGRN_HEAD_DIM
    depth = lb_param.shape[0]
    rows = min(HGRN_ROWS_PER_STEP, seq)
    chunk = min(HGRN_CHUNK, rows)
    hps = HGRN_HEADS_PER_STEP
    assert seq % rows == 0 and rows % chunk == 0 and chunk % HGRN_SUB == 0 and heads % hps == 0
    nr = seq // rows
    return pl.pallas_call(
        functools.partial(_hgrn_kernel, layer=layer, chunk=chunk, sub=HGRN_SUB, nchunks=rows // chunk, hps=hps),
        out_shape=jax.ShapeDtypeStruct((t, heads * dk), BF16),
        grid=(bsz, heads // hps, nr),
        in_specs=[
            pl.BlockSpec((rows, hps * 4 * dk), lambda b, h, c: (b * nr + c, h)),
            pl.BlockSpec((depth, hps * dk), lambda b, h, c: (0, h)),
            pl.BlockSpec((1, dk), lambda b, h, c: (0, 0)),
        ],
        out_specs=pl.BlockSpec((rows, hps * dk), lambda b, h, c: (b * nr + c, h)),
        scratch_shapes=[pltpu.VMEM((hps, dk, dk), F32)] + [pltpu.VMEM((hps, chunk, dk), F32)] * 5,
        compiler_params=_cparams(("parallel", "parallel", "arbitrary")),
        name="hgrn_recurrence",
    )(main, lb_param, norm_w.reshape(1, dk))


def _hgrn_mixer(x, mod, w_in, lb_param, norm_w, w_out, ln_g, ln_b, *, bsz, seq, alpha, layer):
    d = x.shape[1]
    dk = HGRN_HEAD_DIM
    heads = d // dk
    w_perm = w_in.reshape(d, 4, heads, dk).transpose(0, 2, 1, 3).reshape(d, 4 * d).astype(BF16)
    main = _linear_mod(x, mod, w_perm, seq=seq, shift_row=0, scale_row=1, out_dtype=BF16,
                       tm=1024, tn=1024, name="hgrn_in")
    y = _hgrn_recurrence(main, lb_param, norm_w, bsz=bsz, seq=seq, heads=heads, layer=layer)
    return _outproj_ln(y, w_out.astype(BF16), x, mod, ln_g, ln_b, seq=seq, gate_row=2, alpha=alpha,
                       tm=512, tk=2048, name="hgrn_out_ln")


def _conv_kernel(x_ref, halo_ref, w_ref, b_ref, o_ref, *, taps, steps_per_seq):
    first = (pl.program_id(0) % steps_per_seq) == 0
    x = x_ref[...].astype(F32)
    halo = jnp.where(first, 0.0, halo_ref[...].astype(F32))
    xc = jnp.concatenate([halo, x], axis=0)
    hr, tm = halo.shape[0], x.shape[0]
    acc = jnp.broadcast_to(b_ref[...], x.shape)
    for kk in range(taps):
        shift = taps - 1 - kk
        xs = xc if shift == 0 else pltpu.roll(xc, shift, 0)
        acc = acc + w_ref[kk:kk + 1, :] * xs[hr:hr + tm]
    o_ref[...] = _silu(acc).astype(o_ref.dtype)


def _ssd_conv(main, conv_w, conv_b, *, seq, col0):
    t = main.shape[0]
    taps, cdim = conv_w.shape
    tm = min(512, seq)
    tc = 512
    halo = 2 * SUBLANES
    assert seq % tm == 0 and cdim % tc == 0 and col0 % tc == 0 and taps - 1 <= halo and tm % halo == 0
    cb0 = col0 // tc
    hb = tm // halo
    return pl.pallas_call(
        functools.partial(_conv_kernel, taps=taps, steps_per_seq=seq // tm),
        out_shape=jax.ShapeDtypeStruct((t, cdim), BF16),
        grid=(t // tm, cdim // tc),
        in_specs=[
            pl.BlockSpec((tm, tc), lambda i, j: (i, cb0 + j)),
            pl.BlockSpec((halo, tc), lambda i, j: (jnp.maximum(i * hb - 1, 0), cb0 + j)),
            pl.BlockSpec((taps, tc), lambda i, j: (0, j)),
            pl.BlockSpec((1, tc), lambda i, j: (0, j)),
        ],
        out_specs=pl.BlockSpec((tm, tc), lambda i, j: (i, j)),
        compiler_params=_cparams(("parallel", "parallel")),
        name="ssd_conv",
    )(main, main, conv_w, conv_b.reshape(1, cdim))


def _ssd_kernel(xs_ref, b_ref, c_ref, z_ref, aux_ref, dtb_ref, alog_ref, dsk_ref, nw_ref, y_ref,
                state_ref, acol_ref, dt_t_ref, cum_t_ref, *, chunk, hpg, hd, gps):
    cidx = pl.program_id(1)
    g = pl.program_id(2)

    @pl.when(jnp.logical_and(cidx == 0, g == 0))
    def _():
        state_ref[...] = jnp.zeros_like(state_ref)

    tri = _lower_tri(chunk)

    @pl.when(g == 0)
    def _():
        dt = _softplus(aux_ref[...] + dtb_ref[...])
        la = dt * (-jnp.exp(alog_ref[...]))
        cum = _dot_hi(tri.astype(F32), la)
        acol_ref[...] = jnp.log(dt) - cum
        dt_t_ref[...] = dt.T
        cum_t_ref[...] = cum.T

    acol_nat = acol_ref[...]
    lane = lax.broadcasted_iota(I32, acol_nat.shape, 1)
    s_le_t = jnp.logical_not(tri) | (lax.broadcasted_iota(I32, (chunk, chunk), 0)
                                     == lax.broadcasted_iota(I32, (chunk, chunk), 1))
    gw = hpg * hd
    nstate = b_ref.shape[1] // gps

    for gg in range(gps):
        gi = g * gps + gg
        g8 = pl.multiple_of(gi * hpg, hpg)
        cum_t = cum_t_ref[pl.ds(g8, hpg), :]
        dt_t = dt_t_ref[pl.ds(g8, hpg), :]
        xs = xs_ref[:, gg * gw:(gg + 1) * gw].astype(F32)
        x_t = xs.T
        bm = b_ref[:, gg * nstate:(gg + 1) * nstate]
        cm = c_ref[:, gg * nstate:(gg + 1) * nstate]
        cb_t = _dot_nt(bm, cm)
        st = state_ref[gi]
        inter_t = _dot_nt(st.astype(BF16), cm)
        last = cum_t[:, chunk - 1:chunk]
        w_t = jnp.exp(last - cum_t) * dt_t
        ecum_t = jnp.exp(cum_t)

        y_parts = []
        xw_parts = []
        st_parts = []
        for j in range(hpg):
            col = g8 + j
            a_col = jnp.sum(jnp.where(lane == col, acol_nat, 0.0), axis=1, keepdims=True)
            cum_row = cum_t[j:j + 1, :]
            m_t = (cb_t * jnp.exp(jnp.where(s_le_t, cum_row + a_col, -jnp.inf))).astype(BF16)
            x_j = x_t[j * hd:(j + 1) * hd, :]
            y_j = _dot(x_j.astype(BF16), m_t) + inter_t[j * hd:(j + 1) * hd, :] * ecum_t[j:j + 1, :]
            y_parts.append(y_j)
            xw_parts.append((x_j * w_t[j:j + 1, :]).astype(BF16))
            st_parts.append(st[j * hd:(j + 1) * hd, :] * jnp.exp(last[j:j + 1, :]))
        y_t = jnp.concatenate(y_parts, axis=0)
        xw_t = jnp.concatenate(xw_parts, axis=0)
        state_ref[gi] = jnp.concatenate(st_parts, axis=0) + _dot(xw_t, bm)

        y = y_t.T + dsk_ref[:, gg * gw:(gg + 1) * gw] * xs
        y = y * _silu(z_ref[:, gg * gw:(gg + 1) * gw].astype(F32))
        ms = jnp.mean(y * y, axis=1, keepdims=True)
        y_ref[:, gg * gw:(gg + 1) * gw] = (y * lax.rsqrt(ms + RMS_EPS) * nw_ref[:, gg * gw:(gg + 1) * gw]
                                           ).astype(y_ref.dtype)


def _ssd_recurrence(main, conv, aux, dt_bias, a_log, d_skip, norm_w, *, bsz, seq, d_inner, nstate):
    t = main.shape[0]
    groups = SSM_GROUPS
    gw = d_inner // groups
    hpg = gw // SSM_HEADDIM
    nheads = groups * hpg
    chunk = min(SSD_CHUNK, seq)
    assert seq % chunk == 0 and nstate == LANES and chunk == LANES and nheads <= LANES
    nc = seq // chunk
    row = lambda b, c, g: b * nc + c
    pad = lambda v: jnp.zeros((1, LANES), F32).at[0, :nheads].set(v)
    dsk = jnp.repeat(d_skip, SSM_HEADDIM).reshape(1, d_inner)
    gps = SSD_GROUPS_PER_STEP
    assert groups % gps == 0 and d_inner % (gps * nstate) == 0
    b0 = d_inner // (gps * nstate)
    c0 = b0 + groups // gps
    return pl.pallas_call(
        functools.partial(_ssd_kernel, chunk=chunk, hpg=hpg, hd=SSM_HEADDIM, gps=gps),
        out_shape=jax.ShapeDtypeStruct((t, d_inner), BF16),
        grid=(bsz, nc, groups // gps),
        in_specs=[
            pl.BlockSpec((chunk, gps * gw), lambda b, c, g: (row(b, c, g), g)),
            pl.BlockSpec((chunk, gps * nstate), lambda b, c, g: (row(b, c, g), b0 + g)),
            pl.BlockSpec((chunk, gps * nstate), lambda b, c, g: (row(b, c, g), c0 + g)),
            pl.BlockSpec((chunk, gps * gw), lambda b, c, g: (row(b, c, g), g)),
            pl.BlockSpec((chunk, LANES), lambda b, c, g: (row(b, c, g), 0)),
            pl.BlockSpec((1, LANES), lambda b, c, g: (0, 0)),
            pl.BlockSpec((1, LANES), lambda b, c, g: (0, 0)),
            pl.BlockSpec((1, gps * gw), lambda b, c, g: (0, g)),
            pl.BlockSpec((1, gps * gw), lambda b, c, g: (0, g)),
        ],
        out_specs=pl.BlockSpec((chunk, gps * gw), lambda b, c, g: (row(b, c, g), g)),
        scratch_shapes=[
            pltpu.VMEM((groups, gw, nstate), F32),
            pltpu.VMEM((chunk, LANES), F32),
            pltpu.VMEM((LANES, chunk), F32),
            pltpu.VMEM((LANES, chunk), F32),
        ],
        compiler_params=_cparams(("parallel", "arbitrary", "arbitrary")),
        name="ssd_recurrence",
    )(conv, conv, conv, main, aux, pad(dt_bias), pad(a_log), dsk, norm_w.reshape(1, d_inner))


def _ssd_mixer(x, mod, w_in, conv_w, conv_b, dt_bias, a_log, d_skip, norm_w, w_out, ln_g, ln_b,
               *, bsz, seq, alpha):
    d = x.shape[1]
    d_inner = w_out.shape[0]
    cdim = conv_w.shape[-1]
    nheads = dt_bias.shape[0]
    nstate = (cdim - d_inner) // (2 * SSM_GROUPS)
    n_main = d_inner + cdim
    w_main = w_in[:, :n_main].astype(BF16)
    w_aux = jnp.zeros((d, LANES), BF16).at[:, :nheads].set(w_in[:, n_main:].astype(BF16))
    main = _linear_mod(x, mod, w_main, seq=seq, shift_row=0, scale_row=1, out_dtype=BF16,
                       tm=1024, tn=1024, name="ssd_in_main")
    aux = _linear_mod(x, mod, w_aux, seq=seq, shift_row=0, scale_row=1, out_dtype=F32,
                      tm=1024, tn=LANES, name="ssd_in_dt")
    conv = _ssd_conv(main, conv_w.reshape(conv_w.shape[0], cdim), conv_b, seq=seq, col0=d_inner)
    y = _ssd_recurrence(main, conv, aux, dt_bias, a_log, d_skip, norm_w, bsz=bsz, seq=seq,
                        d_inner=d_inner, nstate=nstate)
    return _outproj_ln(y, w_out.astype(BF16), x, mod, ln_g, ln_b, seq=seq, gate_row=2, alpha=alpha,
                       tm=512, tk=2048, name="ssd_out_ln")


def _pack_pairs(lo, hi):
    lo_b = lax.bitcast_convert_type(lo.astype(BF16).astype(F32), U32) >> 16
    hi_b = lax.bitcast_convert_type(hi.astype(BF16).astype(F32), U32) & jnp.uint32(HI_HALF_MASK)
    return lo_b | hi_b


def _unpack_pairs(w):
    lo = lax.bitcast_convert_type(w << 16, F32)
    hi = lax.bitcast_convert_type(w & jnp.uint32(HI_HALF_MASK), F32)
    return lo, hi


def _store_packed_rows(dst_ref, v):
    rows, d = v.shape
    half = d // 2
    pr = half // LANES
    w = _pack_pairs(v[:, :half], v[:, half:])
    for c in range(pr):
        dst_ref[pl.ds(c, rows, stride=pr), :] = w[:, c * LANES:(c + 1) * LANES]


def _load_packed_rows(src_ref, rows, pr, base=0):
    return [_unpack_pairs(src_ref[pl.ds(base + c, rows, stride=pr), :]) for c in range(pr)]


def _router_kernel(x_ref, mod_ref, wr_ref, br_ref, h_ref, ti_ref, tg_ref, *, n_real):
    i = pl.program_id(0)

    @pl.when(i < n_real)
    def _():
        m = mod_ref[0]
        h = x_ref[...] * (1.0 + m[4:5, :]) + m[3:4, :]
        _store_packed_rows(h_ref, h)
        logits = _dot_hi(h, wr_ref[...]) + br_ref[...]
        lane = lax.broadcasted_iota(I32, logits.shape, 1)
        lane_f = lane.astype(F32)
        vals = []
        ti = jnp.zeros(logits.shape, I32)
        for kk in range(TOP_K):
            mx = jnp.max(logits, axis=1, keepdims=True)
            idx = jnp.min(jnp.where(logits == mx, lane_f, float(LANES)), axis=1, keepdims=True).astype(I32)
            vals.append(mx)
            ti = jnp.where(lane == kk, idx, ti)
            logits = jnp.where(lane == idx, -jnp.inf, logits)
        es = [jnp.exp(v - vals[0]) for v in vals]
        tot = es[0]
        for e in es[1:]:
            tot = tot + e
        tg = jnp.zeros(ti.shape, F32)
        for kk in range(TOP_K):
            tg = jnp.where(lane == kk, es[kk] / tot, tg)
        ti_ref[...] = ti
        tg_ref[...] = tg

    @pl.when(i >= n_real)
    def _():
        h_ref[...] = jnp.zeros_like(h_ref)
        ti_ref[...] = jnp.zeros_like(ti_ref)
        tg_ref[...] = jnp.zeros_like(tg_ref)


def _router(x, mod, w_router, b_router, *, seq, tm):
    t, d = x.shape
    ne = w_router.shape[1]
    tm = min(tm, seq)
    n_real = t // tm
    wr = jnp.zeros((d, LANES), F32).at[:, :ne].set(w_router)
    br = jnp.full((1, LANES), -1e30, F32).at[0, :ne].set(b_router)
    clamp = lambda i: jnp.minimum(i, n_real - 1)
    pr = d // (2 * LANES)
    return pl.pallas_call(
        functools.partial(_router_kernel, n_real=n_real),
        out_shape=(
            jax.ShapeDtypeStruct(((t + tm) * pr, LANES), U32),
            jax.ShapeDtypeStruct((t + tm, LANES), I32),
            jax.ShapeDtypeStruct((t + tm, LANES), F32),
        ),
        grid=(n_real + 1,),
        in_specs=[
            pl.BlockSpec((tm, d), lambda i: (clamp(i), 0)),
            pl.BlockSpec((1, 6, d), lambda i: ((clamp(i) * tm) // seq, 0, 0)),
            pl.BlockSpec((d, LANES), lambda i: (0, 0)),
            pl.BlockSpec((1, LANES), lambda i: (0, 0)),
        ],
        out_specs=(
            pl.BlockSpec((tm * pr, LANES), lambda i: (i, 0)),
            pl.BlockSpec((tm, LANES), lambda i: (i, 0)),
            pl.BlockSpec((tm, LANES), lambda i: (i, 0)),
        ),
        compiler_params=_cparams(("parallel",)),
        name="moe_router",
    )(x, mod, wr, br)


def _w1_prep_kernel(w_ref, p_ref, o_ref):
    o_ref[0] = _dot(w_ref[0].astype(BF16), p_ref[...]).astype(BF16)


def _prep_w1(w1):
    ne, d, f2 = w1.shape
    f = f2 // 2
    j = jnp.arange(f2, dtype=I32)
    src_col = jnp.where(j < f, 2 * j, 2 * (j - f) + 1)
    sel = (jnp.arange(f2, dtype=I32)[:, None] == src_col[None, :]).astype(BF16)
    tk = min(512, d)
    return pl.pallas_call(
        _w1_prep_kernel,
        out_shape=jax.ShapeDtypeStruct((ne, d, f2), BF16),
        grid=(ne, d // tk),
        in_specs=[
            pl.BlockSpec((1, tk, f2), lambda e, k: (e, k, 0)),
            pl.BlockSpec((f2, f2), lambda e, k: (0, 0)),
        ],
        out_specs=pl.BlockSpec((1, tk, f2), lambda e, k: (e, k, 0)),
        compiler_params=_cparams(("parallel", "parallel")),
        name="moe_w1_prep",
    )(w1, sel)


def _gather_rows_start(src_hbm, idx_ref, dst_ref, sem, *, count, pr):
    def body(r2, carry):
        for p in range(2):
            r = r2 * 2 + p
            src = src_hbm.at[pl.ds(pl.multiple_of(idx_ref[0, r], pr), pr)]
            dst = dst_ref.at[pl.ds(pl.multiple_of(r * pr, pr), pr)]
            pltpu.make_async_copy(src, dst, sem).start(priority=p)
        return carry

    lax.fori_loop(0, count // 2, body, 0, unroll=4)


def _gather_rows_wait(src_hbm, dst_ref, sem):
    pltpu.make_async_copy(src_hbm.at[pl.ds(0, dst_ref.shape[0])], dst_ref, sem).wait()


def _expert_kernel(be_ref, rows_ref, src_cur_ref, src_nxt_ref, h_hbm, w1_ref, b1_ref, w2_ref, b2_ref, y_ref,
                   xbuf, xs_ref, gsem, *, blk, nb, f):
    i = pl.program_id(0)
    slot = i % 2
    half = xs_ref.shape[1] // 2
    pr = half // LANES

    @pl.when(jnp.logical_and(i == 0, rows_ref[0] > 0))
    def _():
        _gather_rows_start(h_hbm, src_cur_ref, xbuf.at[0], gsem.at[0], count=blk, pr=pr)

    nxt = jnp.minimum(i + 1, nb - 1)

    @pl.when(jnp.logical_and(i + 1 < nb, rows_ref[nxt] > 0))
    def _():
        _gather_rows_start(h_hbm, src_nxt_ref, xbuf.at[1 - slot], gsem.at[1 - slot], count=blk, pr=pr)

    @pl.when(rows_ref[i] > 0)
    def _():
        _gather_rows_wait(h_hbm, xbuf.at[slot], gsem.at[slot])
        for c, (lo, hi) in enumerate(_load_packed_rows(xbuf.at[slot], blk, pr)):
            xs_ref[:, c * LANES:(c + 1) * LANES] = lo.astype(BF16)
            xs_ref[:, half + c * LANES:half + (c + 1) * LANES] = hi.astype(BF16)
        hh = _dot(xs_ref[...], w1_ref[0]) + b1_ref[0]
        glu = jnp.minimum(hh[:, :f], SWIGLU_LIMIT)
        lin = jnp.clip(hh[:, f:], -SWIGLU_LIMIT, SWIGLU_LIMIT)
        act = glu * _sigmoid(SWIGLU_ALPHA * glu) * (lin + 1.0)
        _store_packed_rows(y_ref, _dot(act.astype(BF16), w2_ref[0]) + b2_ref[0])

    @pl.when(rows_ref[i] == 0)
    def _():
        y_ref[...] = jnp.zeros_like(y_ref)


def _experts(h_rows, plan, w1p, b1p, w2b, b2r, *, e0, blk):
    block_e, block_rows, src_rows = plan
    nb = block_e.shape[0]
    f2 = w1p.shape[2]
    f = f2 // 2
    d = w2b.shape[2]
    pr = d // (2 * LANES)
    wmap = lambda i, be, br: (e0 + be[i], 0, 0)
    smem_blk = lambda fn: pl.BlockSpec((None, 1, blk), fn, memory_space=pltpu.SMEM)
    grid_spec = pltpu.PrefetchScalarGridSpec(
        num_scalar_prefetch=2,
        grid=(nb,),
        in_specs=[
            smem_blk(lambda i, be, br: (i, 0, 0)),
            smem_blk(lambda i, be, br: (jnp.minimum(i + 1, nb - 1), 0, 0)),
            pl.BlockSpec(memory_space=pl.ANY),
            pl.BlockSpec((1, d, f2), wmap),
            pl.BlockSpec((1, 1, f2), wmap),
            pl.BlockSpec((1, f, d), wmap),
            pl.BlockSpec((1, 1, d), wmap),
        ],
        out_specs=pl.BlockSpec((blk * pr, LANES), lambda i, be, br: (i, 0)),
        scratch_shapes=[
            pltpu.VMEM((2, blk * pr, LANES), U32),
            pltpu.VMEM((blk, d), BF16),
            pltpu.SemaphoreType.DMA((2,)),
        ],
    )
    src3 = src_rows.reshape(nb, 1, blk)
    return pl.pallas_call(
        functools.partial(_expert_kernel, blk=blk, nb=nb, f=f),
        out_shape=jax.ShapeDtypeStruct((nb * blk * pr, LANES), U32),
        grid_spec=grid_spec,
        compiler_params=_cparams(("arbitrary",)),
        name="moe_experts",
    )(block_e, block_rows, src3, src3, h_rows, w1p, b1p, w2b, b2r)


def _combine_ln_kernel(idx_cur_ref, idx_nxt_ref, y_hbm, tg_ref, x_ref, mod_ref, g_ref, b_ref, o_ref,
                       ybuf, sem, *, alpha, nt):
    i = pl.program_id(0)
    slot = i % 2
    tm, d = x_ref.shape
    pr = d // (2 * LANES)
    count = TOP_K * tm

    @pl.when(i == 0)
    def _():
        _gather_rows_start(y_hbm, idx_cur_ref, ybuf.at[0], sem.at[0], count=count, pr=pr)

    @pl.when(i + 1 < nt)
    def _():
        _gather_rows_start(y_hbm, idx_nxt_ref, ybuf.at[1 - slot], sem.at[1 - slot], count=count, pr=pr)

    _gather_rows_wait(y_hbm, ybuf.at[slot], sem.at[slot])
    tg = tg_ref[...]
    los, his = [None] * pr, [None] * pr
    for kk in range(TOP_K):
        gk = tg[:, kk:kk + 1]
        for c, (lo, hi) in enumerate(_load_packed_rows(ybuf.at[slot], tm, pr, base=kk * tm * pr)):
            los[c] = gk * lo if kk == 0 else los[c] + gk * lo
            his[c] = gk * hi if kk == 0 else his[c] + gk * hi
    acc = jnp.concatenate(los + his, axis=1)
    gate = mod_ref[0][5:6, :]
    o_ref[...] = _deepnorm_ln(x_ref[...], acc, gate, g_ref[...], b_ref[...], alpha)


def _combine_ln(y_rows, slot_rows, tg, x, mod, ln_g, ln_b, *, seq, alpha, tm):
    t, d = x.shape
    nt = t // tm
    pr = d // (2 * LANES)
    idx3 = slot_rows.reshape(nt, 1, TOP_K * tm)
    smem_blk = lambda fn: pl.BlockSpec((None, 1, TOP_K * tm), fn, memory_space=pltpu.SMEM)
    return pl.pallas_call(
        functools.partial(_combine_ln_kernel, alpha=alpha, nt=nt),
        out_shape=jax.ShapeDtypeStruct((t, d), F32),
        grid=(nt,),
        in_specs=[
            smem_blk(lambda i: (i, 0, 0)),
            smem_blk(lambda i: (jnp.minimum(i + 1, nt - 1), 0, 0)),
            pl.BlockSpec(memory_space=pl.ANY),
            pl.BlockSpec((tm, LANES), lambda i: (i, 0)),
            pl.BlockSpec((tm, d), lambda i: (i, 0)),
            pl.BlockSpec((1, 6, d), lambda i: ((i * tm) // seq, 0, 0)),
            pl.BlockSpec((1, d), lambda i: (0, 0)),
            pl.BlockSpec((1, d), lambda i: (0, 0)),
        ],
        out_specs=pl.BlockSpec((tm, d), lambda i: (i, 0)),
        scratch_shapes=[
            pltpu.VMEM((2, TOP_K * tm * pr, LANES), U32),
            pltpu.SemaphoreType.DMA((2,)),
        ],
        compiler_params=_cparams(("arbitrary",)),
        name="moe_combine_ln",
    )(idx3, idx3, y_rows, tg, x, mod, ln_g.reshape(1, d), ln_b.reshape(1, d))


def _dispatch_plan(top_i, *, t, ne, blk, row_sublanes, tm):
    n = t * TOP_K
    n_blocks = n // blk + ne
    n_pad = n_blocks * blk
    e_flat = top_i[:t, :TOP_K].reshape(n)
    e_sorted, order = lax.sort((e_flat, jnp.arange(n, dtype=I32)), num_keys=1, is_stable=True)
    counts = jnp.sum((e_flat[:, None] == jnp.arange(ne, dtype=I32)[None, :]).astype(I32), axis=0)
    starts = jnp.cumsum(counts) - counts
    padded = (counts + blk - 1) // blk * blk
    pad_ends = jnp.cumsum(padded)
    pad_starts = pad_ends - padded
    blk_start = jnp.arange(n_blocks, dtype=I32) * blk
    block_e = jnp.minimum(jnp.sum((pad_ends[None, :] <= blk_start[:, None]).astype(I32), axis=1), ne - 1)
    block_valid = (blk_start < pad_ends[-1]).astype(I32)
    e_slot = jnp.repeat(block_e, blk)
    v_slot = jnp.repeat(block_valid, blk)
    slot = jnp.arange(n_pad, dtype=I32)
    r = slot - pad_starts[e_slot]
    valid = jnp.logical_and(r < counts[e_slot], v_slot == 1)
    p = jnp.clip(starts[e_slot] + r, 0, n - 1)
    src_row = jnp.where(valid, order[p] // TOP_K, t)
    block_rows = jnp.clip(pad_starts[block_e] + counts[block_e] - blk_start, 0, blk) * block_valid
    slot_sorted = pad_starts[e_sorted] + jnp.arange(n, dtype=I32) - starts[e_sorted]
    _, slot_of = lax.sort((order, slot_sorted.astype(I32)), num_keys=1)
    slot_rows = slot_of.reshape(t // tm, tm, TOP_K).transpose(0, 2, 1).reshape(t // tm, TOP_K * tm)
    plan = (block_e, block_rows.astype(I32), (src_row * row_sublanes).astype(I32))
    return plan, (slot_rows * row_sublanes).astype(I32)


def _moe_ffn(x, mod, w_router, b_router, w1p, b1p, w2b, b2r, ln_g, ln_b, *, seq, alpha, e0):
    t, d = x.shape
    ne = w_router.shape[1]
    tm_r = min(512, seq)
    tm_c = min(256, seq)
    h_rows, top_i, top_g = _router(x, mod, w_router, b_router, seq=seq, tm=tm_r)
    plan, slot_rows = _dispatch_plan(top_i, t=t, ne=ne, blk=MOE_BLOCK, row_sublanes=d // (2 * LANES), tm=tm_c)
    y_rows = _experts(h_rows, plan, w1p, b1p, w2b, b2r, e0=e0, blk=MOE_BLOCK)
    return _combine_ln(y_rows, slot_rows, top_g, x, mod, ln_g, ln_b, seq=seq, alpha=alpha, tm=tm_c)


def kernel(x, c, ada_w, ada_b, ln_g, ln_b, mlstm_w_in, mlstm_b_gates, mlstm_norm_w, mlstm_w_out, hgrn_w_in, hgrn_lb_param, hgrn_norm_w, hgrn_w_out, ssm_w_in, ssm_conv_w, ssm_conv_b, ssm_dt_bias, ssm_a_log, ssm_d, ssm_norm_w, ssm_w_out, moe_w_router, moe_b_router, moe_w1, moe_b1, moe_w2, moe_b2):
    bsz, seq, d = x.shape
    depth = ada_w.shape[0]
    alpha = (2 * depth) ** 0.25
    mods = _ada_mods(c, ada_w, ada_b)
    ne, f2 = moe_w1.shape[1], moe_w1.shape[3]
    w1p = _prep_w1(moe_w1.reshape(depth * ne, d, f2))
    b1p = jnp.concatenate([moe_b1[..., 0::2], moe_b1[..., 1::2]], axis=-1).reshape(depth * ne, 1, f2)
    w2b = moe_w2.astype(BF16).reshape(depth * ne, f2 // 2, d)
    b2r = moe_b2.reshape(depth * ne, 1, d)
    xf = x.reshape(bsz * seq, d)
    for i in range(depth):
        mod = mods[i]
        kind, j = i % 3, i // 3
        if kind == 0:
            xf = _mlstm_mixer(xf, mod, mlstm_w_in[j], mlstm_b_gates[j], mlstm_norm_w[j], mlstm_w_out[j],
                              ln_g[i, 0], ln_b[i, 0], bsz=bsz, seq=seq, alpha=alpha)
        elif kind == 1:
            xf = _hgrn_mixer(xf, mod, hgrn_w_in[j], hgrn_lb_param, hgrn_norm_w[j], hgrn_w_out[j],
                             ln_g[i, 0], ln_b[i, 0], bsz=bsz, seq=seq, alpha=alpha, layer=i)
        else:
            xf = _ssd_mixer(xf, mod, ssm_w_in[j], ssm_conv_w[j], ssm_conv_b[j], ssm_dt_bias[j], ssm_a_log[j],
                            ssm_d[j], ssm_norm_w[j], ssm_w_out[j], ln_g[i, 0], ln_b[i, 0],
                            bsz=bsz, seq=seq, alpha=alpha)
        xf = _moe_ffn(xf, mod, moe_w_router[i], moe_b_router[i], w1p, b1p, w2b, b2r,
                      ln_g[i, 1], ln_b[i, 1], seq=seq, alpha=alpha, e0=i * ne)
    return xf.reshape(bsz, seq, d)
```

```python
import functools

import jax
import jax.numpy as jnp
from jax import lax
from jax.experimental import pallas as pl
from jax.experimental.pallas import tpu as pltpu

F32 = jnp.float32
BF16 = jnp.bfloat16
I32 = jnp.int32
U32 = jnp.uint32
HI_HALF_MASK = 0xFFFF0000

MLSTM_GATE_CAP = 15.0
HGRN_HEAD_DIM = 128
SSM_GROUPS = 8
SSM_HEADDIM = 64
TOP_K = 4
SWIGLU_ALPHA = 1.702
SWIGLU_LIMIT = 7.0
LN_EPS = 1e-5
RMS_EPS = 1e-6

LANES = 128
SUBLANES = 8
VMEM_LIMIT_BYTES = 56 * 1024 * 1024

MLSTM_CHUNK = 128
HGRN_CHUNK = 64
HGRN_SUB = 16
HGRN_ROWS_PER_STEP = 512
HGRN_HEADS_PER_STEP = 16
SSD_CHUNK = 128
SSD_GROUPS_PER_STEP = 2
MOE_BLOCK = 512

_HI = lax.Precision.HIGHEST


def _cparams(sem):
    return pltpu.CompilerParams(dimension_semantics=sem, vmem_limit_bytes=VMEM_LIMIT_BYTES)


def _dot(a, b):
    return jnp.dot(a, b, preferred_element_type=F32)


def _dot_nt(a, b):
    return lax.dot_general(a, b, (((1,), (1,)), ((), ())), preferred_element_type=F32)


def _dot_tn(a, b):
    return lax.dot_general(a, b, (((0,), (0,)), ((), ())), preferred_element_type=F32)


def _dot_hi(a, b):
    return jnp.dot(a, b, preferred_element_type=F32, precision=_HI)


def _sigmoid(x):
    return 1.0 / (1.0 + jnp.exp(-x))


def _silu(x):
    return x * _sigmoid(x)


def _log_sigmoid(x):
    return jnp.minimum(x, 0.0) - jnp.log1p(jnp.exp(-jnp.abs(x)))


def _softplus(x):
    return jnp.maximum(x, 0.0) + jnp.log1p(jnp.exp(-jnp.abs(x)))


def _lower_tri(n):
    r = lax.broadcasted_iota(I32, (n, n), 0)
    c = lax.broadcasted_iota(I32, (n, n), 1)
    return r >= c


def _ada_kernel(c_ref, w_ref, b_ref, o_ref):
    cond = _silu(c_ref[...])
    o_ref[0] = _dot(cond.astype(BF16), w_ref[0].astype(BF16)) + b_ref[0]


def _ada_mods(c, ada_w, ada_b):
    depth, d, n6 = ada_w.shape
    bsz = c.shape[0]
    rows = -(-bsz // SUBLANES) * SUBLANES
    c_pad = jnp.zeros((rows, d), F32).at[:bsz].set(c)
    tn = 1536 if n6 % 1536 == 0 else n6
    out = pl.pallas_call(
        _ada_kernel,
        out_shape=jax.ShapeDtypeStruct((depth, rows, n6), F32),
        grid=(depth, n6 // tn),
        in_specs=[
            pl.BlockSpec((rows, d), lambda l, j: (0, 0)),
            pl.BlockSpec((1, d, tn), lambda l, j: (l, 0, j)),
            pl.BlockSpec((1, 1, tn), lambda l, j: (l, 0, j)),
        ],
        out_specs=pl.BlockSpec((1, rows, tn), lambda l, j: (l, 0, j)),
        compiler_params=_cparams(("parallel", "parallel")),
        name="ada_mods",
    )(c_pad, ada_w, ada_b.reshape(depth, 1, n6))
    return out[:, :bsz].reshape(depth, bsz, 6, d)


def _linear_mod_kernel(x_ref, mod_ref, w_ref, o_ref, h_ref, *, shift_row, scale_row):
    @pl.when(pl.program_id(1) == 0)
    def _():
        m = mod_ref[0]
        h = x_ref[...] * (1.0 + m[scale_row:scale_row + 1, :]) + m[shift_row:shift_row + 1, :]
        h_ref[...] = h.astype(BF16)

    o_ref[...] = _dot(h_ref[...], w_ref[...]).astype(o_ref.dtype)


def _linear_mod(x, mod, w, *, seq, shift_row, scale_row, out_dtype, tm, tn, name):
    t, d = x.shape
    n = w.shape[1]
    tm = min(tm, seq)
    tn = min(tn, n)
    assert t % tm == 0 and seq % tm == 0 and n % tn == 0
    return pl.pallas_call(
        functools.partial(_linear_mod_kernel, shift_row=shift_row, scale_row=scale_row),
        out_shape=jax.ShapeDtypeStruct((t, n), out_dtype),
        grid=(t // tm, n // tn),
        in_specs=[
            pl.BlockSpec((tm, d), lambda i, j: (i, 0)),
            pl.BlockSpec((1, 6, d), lambda i, j: ((i * tm) // seq, 0, 0)),
            pl.BlockSpec((d, tn), lambda i, j: (0, j)),
        ],
        out_specs=pl.BlockSpec((tm, tn), lambda i, j: (i, j)),
        scratch_shapes=[pltpu.VMEM((tm, d), BF16)],
        compiler_params=_cparams(("parallel", "arbitrary")),
        name=name,
    )(x, mod, w)


def _deepnorm_ln(x, y, gate, g, b, alpha):
    v = alpha * x + (1.0 + gate) * y
    mu = jnp.mean(v, axis=-1, keepdims=True)
    dv = v - mu
    var = jnp.mean(dv * dv, axis=-1, keepdims=True)
    return dv * lax.rsqrt(var + LN_EPS) * g + b


def _outproj_ln_kernel(y_ref, w_ref, x_ref, mod_ref, g_ref, b_ref, o_ref, acc_ref, *, gate_row, nk, alpha):
    k = pl.program_id(1)
    part = _dot(y_ref[...], w_ref[...])

    def finish(acc):
        gate = mod_ref[0][gate_row:gate_row + 1, :]
        o_ref[...] = _deepnorm_ln(x_ref[...], acc, gate, g_ref[...], b_ref[...], alpha)

    if nk == 1:
        finish(part)
    else:
        @pl.when(k == 0)
        def _():
            acc_ref[...] = part

        @pl.when(jnp.logical_and(k > 0, k < nk - 1))
        def _():
            acc_ref[...] += part

        @pl.when(k == nk - 1)
        def _():
            finish(acc_ref[...] + part)


def _outproj_ln(y, w, x, mod, ln_g, ln_b, *, seq, gate_row, alpha, tm, tk, name):
    t, kdim = y.shape
    d = w.shape[1]
    tm = min(tm, seq)
    tk = min(tk, kdim)
    assert t % tm == 0 and seq % tm == 0 and kdim % tk == 0
    nk = kdim // tk
    return pl.pallas_call(
        functools.partial(_outproj_ln_kernel, gate_row=gate_row, nk=nk, alpha=alpha),
        out_shape=jax.ShapeDtypeStruct((t, d), F32),
        grid=(t // tm, nk),
        in_specs=[
            pl.BlockSpec((tm, tk), lambda i, k: (i, k)),
            pl.BlockSpec((tk, d), lambda i, k: (k, 0)),
            pl.BlockSpec((tm, d), lambda i, k: (i, 0)),
            pl.BlockSpec((1, 6, d), lambda i, k: ((i * tm) // seq, 0, 0)),
            pl.BlockSpec((1, d), lambda i, k: (0, 0)),
            pl.BlockSpec((1, d), lambda i, k: (0, 0)),
        ],
        out_specs=pl.BlockSpec((tm, d), lambda i, k: (i, 0)),
        scratch_shapes=[pltpu.VMEM((tm, d), F32)],
        compiler_params=_cparams(("parallel", "arbitrary")),
        name=name,
    )(y, w, x, mod, ln_g.reshape(1, d), ln_b.reshape(1, d))


def _mlstm_kernel(q_ref, k_ref, v_ref, o_ref, aux_ref, bias_ref, nw_ref, y_ref,
                  c_ref, n_ref, m_ref, *, heads, dqk, dv, chunk):
    @pl.when(pl.program_id(1) == 0)
    def _():
        c_ref[...] = jnp.zeros_like(c_ref)
        n_ref[...] = jnp.zeros_like(n_ref)
        m_ref[...] = jnp.zeros_like(m_ref)

    scale = dqk ** -0.5
    g = MLSTM_GATE_CAP * jnp.tanh((aux_ref[...] + bias_ref[...]) / MLSTM_GATE_CAP)
    lane = lax.broadcasted_iota(I32, g.shape, 1)
    logsig = _log_sigmoid(g)
    is_f = jnp.logical_and(lane >= heads, lane < 2 * heads)
    tri = _lower_tri(chunk)
    bcum = _dot_hi(tri.astype(F32), jnp.where(is_f, logsig, 0.0))
    g_t = g.T
    bcum_t = bcum.T

    for h in range(heads):
        ig_col = g[:, h:h + 1]
        ig_row = g_t[h:h + 1, :]
        b_col = bcum[:, heads + h:heads + h + 1]
        b_row = bcum_t[heads + h:heads + h + 1, :]
        m_prev = m_ref[h][0:1, 0:1]

        log_d = jnp.where(tri, b_col - b_row + ig_row, -jnp.inf)
        m_inter = b_col + m_prev
        m_t = jnp.maximum(jnp.max(log_d, axis=1, keepdims=True), m_inter)
        dmat = jnp.exp(log_d - m_t)

        q = q_ref[:, h * dqk:(h + 1) * dqk]
        k = k_ref[:, h * dqk:(h + 1) * dqk]
        v = v_ref[:, h * dv:(h + 1) * dv]
        scores = _dot_nt(q, k) * scale * dmat
        inter = jnp.exp(m_inter - m_t)
        c_st = c_ref[h]
        n_st = n_ref[h][0:1, :]
        num = _dot(scores.astype(BF16), v) + inter * (_dot_nt(q, c_st.astype(BF16)) * scale)
        qn = jnp.sum(q.astype(F32) * n_st, axis=1, keepdims=True) * scale
        den = jnp.sum(scores, axis=1, keepdims=True) + inter * qn
        hout = num * (1.0 / jnp.maximum(jnp.abs(den), jnp.exp(-m_t)))

        b_last = b_col[chunk - 1:chunk, :]
        log_w = b_last - b_col + ig_col
        m_new = jnp.maximum(b_last + m_prev, jnp.max(log_w, axis=0, keepdims=True))
        w = jnp.exp(log_w - m_new)
        decay = jnp.exp(b_last + m_prev - m_new)
        vw = (v.astype(F32) * w).astype(BF16)
        c_ref[h] = decay * c_st + _dot_tn(vw, k)
        n_new = decay * n_st + jnp.sum(k.astype(F32) * w, axis=0, keepdims=True)
        n_ref[h] = jnp.broadcast_to(n_new, n_ref.shape[1:])
        m_ref[h] = jnp.broadcast_to(m_new, m_ref.shape[1:])

        ms = jnp.mean(hout * hout, axis=1, keepdims=True)
        hn = hout * lax.rsqrt(ms + RMS_EPS) * nw_ref[:, h * dv:(h + 1) * dv]
        og = o_ref[:, h * dv:(h + 1) * dv].astype(F32)
        y_ref[:, h * dv:(h + 1) * dv] = (hn * _sigmoid(og)).astype(y_ref.dtype)


def _mlstm_recurrence(main, aux, bias, norm_w, *, bsz, seq, heads, dqk, dv):
    t = main.shape[0]
    nqk, nv = heads * dqk, heads * dv
    chunk = min(MLSTM_CHUNK, seq)
    nc = seq // chunk
    assert nv == 2 * nqk and seq % chunk == 0
    row = lambda b, c: b * nc + c
    return pl.pallas_call(
        functools.partial(_mlstm_kernel, heads=heads, dqk=dqk, dv=dv, chunk=chunk),
        out_shape=jax.ShapeDtypeStruct((t, nv), BF16),
        grid=(bsz, nc),
        in_specs=[
            pl.BlockSpec((chunk, nqk), lambda b, c: (row(b, c), 0)),
            pl.BlockSpec((chunk, nqk), lambda b, c: (row(b, c), 1)),
            pl.BlockSpec((chunk, nv), lambda b, c: (row(b, c), 1)),
            pl.BlockSpec((chunk, nv), lambda b, c: (row(b, c), 2)),
            pl.BlockSpec((chunk, LANES), lambda b, c: (row(b, c), 0)),
            pl.BlockSpec((1, LANES), lambda b, c: (0, 0)),
            pl.BlockSpec((1, nv), lambda b, c: (0, 0)),
        ],
        out_specs=pl.BlockSpec((chunk, nv), lambda b, c: (row(b, c), 0)),
        scratch_shapes=[
            pltpu.VMEM((heads, dv, dqk), F32),
            pltpu.VMEM((heads, SUBLANES, dqk), F32),
            pltpu.VMEM((heads, SUBLANES, LANES), F32),
        ],
        compiler_params=_cparams(("parallel", "arbitrary")),
        name="mlstm_recurrence",
    )(main, main, main, main, aux, bias, norm_w.reshape(1, nv))


def _mlstm_mixer(x, mod, w_in, b_gates, norm_w, w_out, ln_g, ln_b, *, bsz, seq, alpha):
    d = x.shape[1]
    heads = b_gates.shape[0] // 2
    nv = w_out.shape[0]
    nqk = (w_in.shape[1] - 2 * nv - 2 * heads) // 2
    n_main = 2 * nqk + 2 * nv
    w_main = w_in[:, :n_main].astype(BF16)
    w_aux = jnp.zeros((d, LANES), BF16).at[:, :2 * heads].set(w_in[:, n_main:].astype(BF16))
    bias = jnp.zeros((1, LANES), F32).at[0, :2 * heads].set(b_gates)
    main = _linear_mod(x, mod, w_main, seq=seq, shift_row=0, scale_row=1, out_dtype=BF16,
                       tm=1024, tn=1024, name="mlstm_in_main")
    aux = _linear_mod(x, mod, w_aux, seq=seq, shift_row=0, scale_row=1, out_dtype=F32,
                      tm=1024, tn=LANES, name="mlstm_in_gates")
    y = _mlstm_recurrence(main, aux, bias, norm_w, bsz=bsz, seq=seq, heads=heads,
                          dqk=nqk // heads, dv=nv // heads)
    return _outproj_ln(y, w_out.astype(BF16), x, mod, ln_g, ln_b, seq=seq, gate_row=2, alpha=alpha,
                       tm=512, tk=2048, name="mlstm_out_ln")


def _hgrn_kernel(main_ref, lbp_ref, nw_ref, y_ref, st_ref, q_ref, gc_ref, g_ref, i_ref, o_ref,
                 *, layer, chunk, sub, nchunks, hps):
    dk = HGRN_HEAD_DIM

    @pl.when(pl.program_id(2) == 0)
    def _():
        st_ref[...] = jnp.zeros_like(st_ref)

    p = lbp_ref[...]
    e = jnp.exp(p - jnp.max(p, axis=0, keepdims=True))
    sm = e / jnp.sum(e, axis=0, keepdims=True)
    lb_all = jnp.zeros((1, hps * dk), F32)
    for r in range(1, layer + 1):
        lb_all = lb_all + sm[r:r + 1, :]
    tri = _lower_tri(chunk).astype(F32)
    nsub = chunk // sub
    half = sub // 2
    half_row = lax.broadcasted_iota(I32, (half, dk), 0)
    ones_sq = jnp.ones((dk, dk), BF16)
    nw = nw_ref[...]

    def prologue(r0, hh):
        lb = lb_all[:, hh * dk:(hh + 1) * dk]
        log_lb = jnp.log(lb)
        log_1m = jnp.log1p(-lb)
        c0 = hh * 4 * dk
        blk = main_ref[pl.ds(r0, chunk), c0:c0 + 3 * dk]
        qraw = blk[:, 0:dk].astype(F32)
        fraw = blk[:, dk:2 * dk].astype(F32)
        iv_b = blk[:, 2 * dk:3 * dk]
        q = _silu(qraw)
        lsig = _log_sigmoid(fraw)
        b2 = log_1m + lsig
        logf = jnp.maximum(log_lb, b2) + jnp.log1p(jnp.exp(-jnp.abs(log_lb - b2)))
        gcum = _dot_hi(tri, logf)
        gk = gcum - (log_1m + lsig - fraw)
        q_ref[hh] = q
        gc_ref[hh] = gcum
        g_ref[hh] = gk
        i_ref[hh] = iv_b.astype(F32)
        st = st_ref[hh]
        o_ref[hh] = _dot_nt((q * jnp.exp(gcum)).astype(BF16), st.astype(BF16))
        g_last = gcum[chunk - 1:chunk]
        st_ref[hh] = st * jnp.exp(g_last) + _dot_tn(iv_b, jnp.exp(g_last - gk).astype(BF16))

    def sub_block(r0, hh, bi):
        lo = bi * sub
        g_i = gc_ref[hh, lo:lo + sub, :]
        q_i = q_ref[hh, lo:lo + sub, :]
        acc = o_ref[hh, lo:lo + sub, :]
        if bi > 0:
            ref_row = gc_ref[hh, lo - 1:lo, :]
            qt = (q_i * jnp.exp(g_i - ref_row)).astype(BF16)
            kt = jnp.exp(ref_row - g_ref[hh, 0:lo, :]).astype(BF16)
            c0 = hh * 4 * dk
            iv_b = main_ref[pl.ds(r0, lo), c0 + 2 * dk:c0 + 3 * dk]
            acc = acc + _dot(_dot_nt(qt, kt).astype(BF16), iv_b)
        g_top, g_bot = g_i[:half], g_i[half:]
        q_top, q_bot = q_i[:half], q_i[half:]
        pws = []
        for s in range(sub):
            gs = g_ref[hh, lo + s:lo + s + 1, :]
            if s < half:
                pws.append(q_top * jnp.exp(jnp.where(half_row >= s, g_top - gs, -jnp.inf)))
                pws.append(q_bot * jnp.exp(g_bot - gs))
            else:
                pws.append(q_bot * jnp.exp(jnp.where(half_row >= s - half, g_bot - gs, -jnp.inf)))
        rs = _dot(jnp.concatenate(pws, axis=0).astype(BF16), ones_sq)
        top, bot = acc[:half], acc[half:]
        p = 0
        for s in range(sub):
            i_s = i_ref[hh, lo + s:lo + s + 1, :]
            if s < half:
                top = top + rs[p * half:(p + 1) * half] * i_s
                p += 1
            bot = bot + rs[p * half:(p + 1) * half] * i_s
            p += 1
        o_ref[hh, lo:lo + half, :] = top
        o_ref[hh, lo + half:lo + sub, :] = bot

    def epilogue(r0, hh):
        o = o_ref[hh]
        c0 = hh * 4 * dk
        graw = main_ref[pl.ds(r0, chunk), c0 + 3 * dk:c0 + 4 * dk].astype(F32)
        ms = jnp.mean(o * o, axis=1, keepdims=True)
        y = o * lax.rsqrt(ms + RMS_EPS) * nw * _silu(graw)
        y_ref[pl.ds(r0, chunk), hh * dk:(hh + 1) * dk] = y.astype(y_ref.dtype)

    def body(ci, carry):
        r0 = pl.multiple_of(ci * chunk, chunk)
        for hh in range(hps):
            prologue(r0, hh)
        for bi in range(nsub):
            for hh in range(hps):
                sub_block(r0, hh, bi)
        for hh in range(hps):
            epilogue(r0, hh)
        return carry

    lax.fori_loop(0, nchunks, body, 0)


def _hgrn_recurrence(main, lb_param, norm_w, *, bsz, seq, heads, layer):
    t = main.shape[0]
    dk = HGRN_HEAD_DIM
    depth = lb_param.shape[0]
    rows = min(HGRN_ROWS_PER_STEP, seq)
    chunk = min(HGRN_CHUNK, rows)
    hps = HGRN_HEADS_PER_STEP
    assert seq % rows == 0 and rows % chunk == 0 and chunk % HGRN_SUB == 0 and heads % hps == 0
    nr = seq // rows
    return pl.pallas_call(
        functools.partial(_hgrn_kernel, layer=layer, chunk=chunk, sub=HGRN_SUB, nchunks=rows // chunk, hps=hps),
        out_shape=jax.ShapeDtypeStruct((t, heads * dk), BF16),
        grid=(bsz, heads // hps, nr),
        in_specs=[
            pl.BlockSpec((rows, hps * 4 * dk), lambda b, h, c: (b * nr + c, h)),
            pl.BlockSpec((depth, hps * dk), lambda b, h, c: (0, h)),
            pl.BlockSpec((1, dk), lambda b, h, c: (0, 0)),
        ],
        out_specs=pl.BlockSpec((rows, hps * dk), lambda b, h, c: (b * nr + c, h)),
        scratch_shapes=[pltpu.VMEM((hps, dk, dk), F32)] + [pltpu.VMEM((hps, chunk, dk), F32)] * 5,
        compiler_params=_cparams(("parallel", "parallel", "arbitrary")),
        name="hgrn_recurrence",
    )(main, lb_param, norm_w.reshape(1, dk))


def _hgrn_mixer(x, mod, w_in, lb_param, norm_w, w_out, ln_g, ln_b, *, bsz, seq, alpha, layer):
    d = x.shape[1]
    dk = HGRN_HEAD_DIM
    heads = d // dk
    w_perm = w_in.reshape(d, 4, heads, dk).transpose(0, 2, 1, 3).reshape(d, 4 * d).astype(BF16)
    main = _linear_mod(x, mod, w_perm, seq=seq, shift_row=0, scale_row=1, out_dtype=BF16,
                       tm=1024, tn=1024, name="hgrn_in")
    y = _hgrn_recurrence(main, lb_param, norm_w, bsz=bsz, seq=seq, heads=heads, layer=layer)
    return _outproj_ln(y, w_out.astype(BF16), x, mod, ln_g, ln_b, seq=seq, gate_row=2, alpha=alpha,
                       tm=512, tk=2048, name="hgrn_out_ln")


def _conv_kernel(x_ref, halo_ref, w_ref, b_ref, o_ref, *, taps, steps_per_seq):
    first = (pl.program_id(0) % steps_per_seq) == 0
    x = x_ref[...].astype(F32)
    halo = jnp.where(first, 0.0, halo_ref[...].astype(F32))
    xc = jnp.concatenate([halo, x], axis=0)
    hr, tm = halo.shape[0], x.shape[0]
    acc = jnp.broadcast_to(b_ref[...], x.shape)
    for kk in range(taps):
        shift = taps - 1 - kk
        xs = xc if shift == 0 else pltpu.roll(xc, shift, 0)
        acc = acc + w_ref[kk:kk + 1, :] * xs[hr:hr + tm]
    o_ref[...] = _silu(acc).astype(o_ref.dtype)


def _ssd_conv(main, conv_w, conv_b, *, seq, col0):
    t = main.shape[0]
    taps, cdim = conv_w.shape
    tm = min(512, seq)
    tc = 512
    halo = 2 * SUBLANES
    assert seq % tm == 0 and cdim % tc == 0 and col0 % tc == 0 and taps - 1 <= halo and tm % halo == 0
    cb0 = col0 // tc
    hb = tm // halo
    return pl.pallas_call(
        functools.partial(_conv_kernel, taps=taps, steps_per_seq=seq // tm),
        out_shape=jax.ShapeDtypeStruct((t, cdim), BF16),
        grid=(t // tm, cdim // tc),
        in_specs=[
            pl.BlockSpec((tm, tc), lambda i, j: (i, cb0 + j)),
            pl.BlockSpec((halo, tc), lambda i, j: (jnp.maximum(i * hb - 1, 0), cb0 + j)),
            pl.BlockSpec((taps, tc), lambda i, j: (0, j)),
            pl.BlockSpec((1, tc), lambda i, j: (0, j)),
        ],
        out_specs=pl.BlockSpec((tm, tc), lambda i, j: (i, j)),
        compiler_params=_cparams(("parallel", "parallel")),
        name="ssd_conv",
    )(main, main, conv_w, conv_b.reshape(1, cdim))


def _ssd_kernel(xs_ref, b_ref, c_ref, z_ref, aux_ref, dtb_ref, alog_ref, dsk_ref, nw_ref, y_ref,
                state_ref, acol_ref, dt_t_ref, cum_t_ref, *, chunk, hpg, hd, gps):
    cidx = pl.program_id(1)
    g = pl.program_id(2)

    @pl.when(jnp.logical_and(cidx == 0, g == 0))
    def _():
        state_ref[...] = jnp.zeros_like(state_ref)

    tri = _lower_tri(chunk)

    @pl.when(g == 0)
    def _():
        dt = _softplus(aux_ref[...] + dtb_ref[...])
        la = dt * (-jnp.exp(alog_ref[...]))
        cum = _dot_hi(tri.astype(F32), la)
        acol_ref[...] = jnp.log(dt) - cum
        dt_t_ref[...] = dt.T
        cum_t_ref[...] = cum.T

    acol_nat = acol_ref[...]
    lane = lax.broadcasted_iota(I32, acol_nat.shape, 1)
    s_le_t = jnp.logical_not(tri) | (lax.broadcasted_iota(I32, (chunk, chunk), 0)
                                     == lax.broadcasted_iota(I32, (chunk, chunk), 1))
    gw = hpg * hd
    nstate = b_ref.shape[1] // gps

    for gg in range(gps):
        gi = g * gps + gg
        g8 = pl.multiple_of(gi * hpg, hpg)
        cum_t = cum_t_ref[pl.ds(g8, hpg), :]
        dt_t = dt_t_ref[pl.ds(g8, hpg), :]
        xs = xs_ref[:, gg * gw:(gg + 1) * gw].astype(F32)
        x_t = xs.T
        bm = b_ref[:, gg * nstate:(gg + 1) * nstate]
        cm = c_ref[:, gg * nstate:(gg + 1) * nstate]
        cb_t = _dot_nt(bm, cm)
        st = state_ref[gi]
        inter_t = _dot_nt(st.astype(BF16), cm)
        last = cum_t[:, chunk - 1:chunk]
        w_t = jnp.exp(last - cum_t) * dt_t
        ecum_t = jnp.exp(cum_t)

        y_parts = []
        xw_parts = []
        st_parts = []
        for j in range(hpg):
            col = g8 + j
            a_col = jnp.sum(jnp.where(lane == col, acol_nat, 0.0), axis=1, keepdims=True)
            cum_row = cum_t[j:j + 1, :]
            m_t = (cb_t * jnp.exp(jnp.where(s_le_t, cum_row + a_col, -jnp.inf))).astype(BF16)
            x_j = x_t[j * hd:(j + 1) * hd, :]
            y_j = _dot(x_j.astype(BF16), m_t) + inter_t[j * hd:(j + 1) * hd, :] * ecum_t[j:j + 1, :]
            y_parts.append(y_j)
            xw_parts.append((x_j * w_t[j:j + 1, :]).astype(BF16))
            st_parts.append(st[j * hd:(j + 1) * hd, :] * jnp.exp(last[j:j + 1, :]))
        y_t = jnp.concatenate(y_parts, axis=0)
        xw_t = jnp.concatenate(xw_parts, axis=0)
        state_ref[gi] = jnp.concatenate(st_parts, axis=0) + _dot(xw_t, bm)

        y = y_t.T + dsk_ref[:, gg * gw:(gg + 1) * gw] * xs
        y = y * _silu(z_ref[:, gg * gw:(gg + 1) * gw].astype(F32))
        ms = jnp.mean(y * y, axis=1, keepdims=True)
        y_ref[:, gg * gw:(gg + 1) * gw] = (y * lax.rsqrt(ms + RMS_EPS) * nw_ref[:, gg * gw:(gg + 1) * gw]
                                           ).astype(y_ref.dtype)


def _ssd_recurrence(main, conv, aux, dt_bias, a_log, d_skip, norm_w, *, bsz, seq, d_inner, nstate):
    t = main.shape[0]
    groups = SSM_GROUPS
    gw = d_inner // groups
    hpg = gw // SSM_HEADDIM
    nheads = groups * hpg
    chunk = min(SSD_CHUNK, seq)
    assert seq % chunk == 0 and nstate == LANES and chunk == LANES and nheads <= LANES
    nc = seq // chunk
    row = lambda b, c, g: b * nc + c
    pad = lambda v: jnp.zeros((1, LANES), F32).at[0, :nheads].set(v)
    dsk = jnp.repeat(d_skip, SSM_HEADDIM).reshape(1, d_inner)
    gps = SSD_GROUPS_PER_STEP
    assert groups % gps == 0 and d_inner % (gps * nstate) == 0
    b0 = d_inner // (gps * nstate)
    c0 = b0 + groups // gps
    return pl.pallas_call(
        functools.partial(_ssd_kernel, chunk=chunk, hpg=hpg, hd=SSM_HEADDIM, gps=gps),
        out_shape=jax.ShapeDtypeStruct((t, d_inner), BF16),
        grid=(bsz, nc, groups // gps),
        in_specs=[
            pl.BlockSpec((chunk, gps * gw), lambda b, c, g: (row(b, c, g), g)),
            pl.BlockSpec((chunk, gps * nstate), lambda b, c, g: (row(b, c, g), b0 + g)),
            pl.BlockSpec((chunk, gps * nstate), lambda b, c, g: (row(b, c, g), c0 + g)),
            pl.BlockSpec((chunk, gps * gw), lambda b, c, g: (row(b, c, g), g)),
            pl.BlockSpec((chunk, LANES), lambda b, c, g: (row(b, c, g), 0)),
            pl.BlockSpec((1, LANES), lambda b, c, g: (0, 0)),
            pl.BlockSpec((1, LANES), lambda b, c, g: (0, 0)),
            pl.BlockSpec((1, gps * gw), lambda b, c, g: (0, g)),
            pl.BlockSpec((1, gps * gw), lambda b, c, g: (0, g)),
        ],
        out_specs=pl.BlockSpec((chunk, gps * gw), lambda b, c, g: (row(b, c, g), g)),
        scratch_shapes=[
            pltpu.VMEM((groups, gw, nstate), F32),
            pltpu.VMEM((chunk, LANES), F32),
            pltpu.VMEM((LANES, chunk), F32),
            pltpu.VMEM((LANES, chunk), F32),
        ],
        compiler_params=_cparams(("parallel", "arbitrary", "arbitrary")),
        name="ssd_recurrence",
    )(conv, conv, conv, main, aux, pad(dt_bias), pad(a_log), dsk, norm_w.reshape(1, d_inner))


def _ssd_mixer(x, mod, w_in, conv_w, conv_b, dt_bias, a_log, d_skip, norm_w, w_out, ln_g, ln_b,
               *, bsz, seq, alpha):
    d = x.shape[1]
    d_inner = w_out.shape[0]
    cdim = conv_w.shape[-1]
    nheads = dt_bias.shape[0]
    nstate = (cdim - d_inner) // (2 * SSM_GROUPS)
    n_main = d_inner + cdim
    w_main = w_in[:, :n_main].astype(BF16)
    w_aux = jnp.zeros((d, LANES), BF16).at[:, :nheads].set(w_in[:, n_main:].astype(BF16))
    main = _linear_mod(x, mod, w_main, seq=seq, shift_row=0, scale_row=1, out_dtype=BF16,
                       tm=1024, tn=1024, name="ssd_in_main")
    aux = _linear_mod(x, mod, w_aux, seq=seq, shift_row=0, scale_row=1, out_dtype=F32,
                      tm=1024, tn=LANES, name="ssd_in_dt")
    conv = _ssd_conv(main, conv_w.reshape(conv_w.shape[0], cdim), conv_b, seq=seq, col0=d_inner)
    y = _ssd_recurrence(main, conv, aux, dt_bias, a_log, d_skip, norm_w, bsz=bsz, seq=seq,
                        d_inner=d_inner, nstate=nstate)
    return _outproj_ln(y, w_out.astype(BF16), x, mod, ln_g, ln_b, seq=seq, gate_row=2, alpha=alpha,
                       tm=256, tk=w_out.shape[0], name="ssd_out_ln")


def _pack_pairs(lo, hi):
    lo_b = lax.bitcast_convert_type(lo.astype(BF16).astype(F32), U32) >> 16
    hi_b = lax.bitcast_convert_type(hi.astype(BF16).astype(F32), U32) & jnp.uint32(HI_HALF_MASK)
    return lo_b | hi_b


def _unpack_pairs(w):
    lo = lax.bitcast_convert_type(w << 16, F32)
    hi = lax.bitcast_convert_type(w & jnp.uint32(HI_HALF_MASK), F32)
    return lo, hi


def _store_packed_rows(dst_ref, v):
    rows, d = v.shape
    half = d // 2
    pr = half // LANES
    w = _pack_pairs(v[:, :half], v[:, half:])
    for c in range(pr):
        dst_ref[pl.ds(c, rows, stride=pr), :] = w[:, c * LANES:(c + 1) * LANES]


def _load_packed_rows(src_ref, rows, pr, base=0):
    return [_unpack_pairs(src_ref[pl.ds(base + c, rows, stride=pr), :]) for c in range(pr)]


def _router_kernel(x_ref, mod_ref, wr_ref, br_ref, h_ref, ti_ref, tg_ref, *, n_real):
    i = pl.program_id(0)

    @pl.when(i < n_real)
    def _():
        m = mod_ref[0]
        h = x_ref[...] * (1.0 + m[4:5, :]) + m[3:4, :]
        _store_packed_rows(h_ref, h)
        logits = _dot_hi(h, wr_ref[...]) + br_ref[...]
        lane = lax.broadcasted_iota(I32, logits.shape, 1)
        lane_f = lane.astype(F32)
        vals = []
        ti = jnp.zeros(logits.shape, I32)
        for kk in range(TOP_K):
            mx = jnp.max(logits, axis=1, keepdims=True)
            idx = jnp.min(jnp.where(logits == mx, lane_f, float(LANES)), axis=1, keepdims=True).astype(I32)
            vals.append(mx)
            ti = jnp.where(lane == kk, idx, ti)
            logits = jnp.where(lane == idx, -jnp.inf, logits)
        es = [jnp.exp(v - vals[0]) for v in vals]
        tot = es[0]
        for e in es[1:]:
            tot = tot + e
        tg = jnp.zeros(ti.shape, F32)
        for kk in range(TOP_K):
            tg = jnp.where(lane == kk, es[kk] / tot, tg)
        ti_ref[...] = ti
        tg_ref[...] = tg

    @pl.when(i >= n_real)
    def _():
        h_ref[...] = jnp.zeros_like(h_ref)
        ti_ref[...] = jnp.zeros_like(ti_ref)
        tg_ref[...] = jnp.zeros_like(tg_ref)


def _router(x, mod, w_router, b_router, *, seq, tm):
    t, d = x.shape
    ne = w_router.shape[1]
    tm = min(tm, seq)
    n_real = t // tm
    wr = jnp.zeros((d, LANES), F32).at[:, :ne].set(w_router)
    br = jnp.full((1, LANES), -1e30, F32).at[0, :ne].set(b_router)
    clamp = lambda i: jnp.minimum(i, n_real - 1)
    pr = d // (2 * LANES)
    return pl.pallas_call(
        functools.partial(_router_kernel, n_real=n_real),
        out_shape=(
            jax.ShapeDtypeStruct(((t + tm) * pr, LANES), U32),
            jax.ShapeDtypeStruct((t + tm, LANES), I32),
            jax.ShapeDtypeStruct((t + tm, LANES), F32),
        ),
        grid=(n_real + 1,),
        in_specs=[
            pl.BlockSpec((tm, d), lambda i: (clamp(i), 0)),
            pl.BlockSpec((1, 6, d), lambda i: ((clamp(i) * tm) // seq, 0, 0)),
            pl.BlockSpec((d, LANES), lambda i: (0, 0)),
            pl.BlockSpec((1, LANES), lambda i: (0, 0)),
        ],
        out_specs=(
            pl.BlockSpec((tm * pr, LANES), lambda i: (i, 0)),
            pl.BlockSpec((tm, LANES), lambda i: (i, 0)),
            pl.BlockSpec((tm, LANES), lambda i: (i, 0)),
        ),
        compiler_params=_cparams(("parallel",)),
        name="moe_router",
    )(x, mod, wr, br)


def _w1_prep_kernel(w_ref, p_ref, o_ref):
    o_ref[0] = _dot(w_ref[0].astype(BF16), p_ref[...]).astype(BF16)


def _prep_w1(w1):
    ne, d, f2 = w1.shape
    f = f2 // 2
    j = jnp.arange(f2, dtype=I32)
    src_col = jnp.where(j < f, 2 * j, 2 * (j - f) + 1)
    sel = (jnp.arange(f2, dtype=I32)[:, None] == src_col[None, :]).astype(BF16)
    tk = min(512, d)
    return pl.pallas_call(
        _w1_prep_kernel,
        out_shape=jax.ShapeDtypeStruct((ne, d, f2), BF16),
        grid=(ne, d // tk),
        in_specs=[
            pl.BlockSpec((1, tk, f2), lambda e, k: (e, k, 0)),
            pl.BlockSpec((f2, f2), lambda e, k: (0, 0)),
        ],
        out_specs=pl.BlockSpec((1, tk, f2), lambda e, k: (e, k, 0)),
        compiler_params=_cparams(("parallel", "parallel")),
        name="moe_w1_prep",
    )(w1, sel)


def _gather_rows_start(src_hbm, idx_ref, dst_ref, sem, *, count, pr):
    def body(r2, carry):
        for p in range(2):
            r = r2 * 2 + p
            src = src_hbm.at[pl.ds(pl.multiple_of(idx_ref[0, r], pr), pr)]
            dst = dst_ref.at[pl.ds(pl.multiple_of(r * pr, pr), pr)]
            pltpu.make_async_copy(src, dst, sem).start(priority=p)
        return carry

    lax.fori_loop(0, count // 2, body, 0, unroll=4)


def _gather_rows_wait(src_hbm, dst_ref, sem):
    pltpu.make_async_copy(src_hbm.at[pl.ds(0, dst_ref.shape[0])], dst_ref, sem).wait()


def _expert_kernel(be_ref, rows_ref, src_cur_ref, src_nxt_ref, h_hbm, w1_ref, b1_ref, w2_ref, b2_ref, y_ref,
                   xbuf, xs_ref, gsem, *, blk, nb, f):
    i = pl.program_id(0)
    slot = i % 2
    half = xs_ref.shape[1] // 2
    pr = half // LANES

    @pl.when(jnp.logical_and(i == 0, rows_ref[0] > 0))
    def _():
        _gather_rows_start(h_hbm, src_cur_ref, xbuf.at[0], gsem.at[0], count=blk, pr=pr)

    nxt = jnp.minimum(i + 1, nb - 1)

    @pl.when(jnp.logical_and(i + 1 < nb, rows_ref[nxt] > 0))
    def _():
        _gather_rows_start(h_hbm, src_nxt_ref, xbuf.at[1 - slot], gsem.at[1 - slot], count=blk, pr=pr)

    @pl.when(rows_ref[i] > 0)
    def _():
        _gather_rows_wait(h_hbm, xbuf.at[slot], gsem.at[slot])
        for c, (lo, hi) in enumerate(_load_packed_rows(xbuf.at[slot], blk, pr)):
            xs_ref[:, c * LANES:(c + 1) * LANES] = lo.astype(BF16)
            xs_ref[:, half + c * LANES:half + (c + 1) * LANES] = hi.astype(BF16)
        hh = _dot(xs_ref[...], w1_ref[0]) + b1_ref[0]
        glu = jnp.minimum(hh[:, :f], SWIGLU_LIMIT)
        lin = jnp.clip(hh[:, f:], -SWIGLU_LIMIT, SWIGLU_LIMIT)
        act = glu * _sigmoid(SWIGLU_ALPHA * glu) * (lin + 1.0)
        _store_packed_rows(y_ref, _dot(act.astype(BF16), w2_ref[0]) + b2_ref[0])

    @pl.when(rows_ref[i] == 0)
    def _():
        y_ref[...] = jnp.zeros_like(y_ref)


def _experts(h_rows, plan, w1p, b1p, w2b, b2r, *, e0, blk):
    block_e, block_rows, src_rows = plan
    nb = block_e.shape[0]
    f2 = w1p.shape[2]
    f = f2 // 2
    d = w2b.shape[2]
    pr = d // (2 * LANES)
    wmap = lambda i, be, br: (e0 + be[i], 0, 0)
    smem_blk = lambda fn: pl.BlockSpec((None, 1, blk), fn, memory_space=pltpu.SMEM)
    grid_spec = pltpu.PrefetchScalarGridSpec(
        num_scalar_prefetch=2,
        grid=(nb,),
        in_specs=[
            smem_blk(lambda i, be, br: (i, 0, 0)),
            smem_blk(lambda i, be, br: (jnp.minimum(i + 1, nb - 1), 0, 0)),
            pl.BlockSpec(memory_space=pl.ANY),
            pl.BlockSpec((1, d, f2), wmap),
            pl.BlockSpec((1, 1, f2), wmap),
            pl.BlockSpec((1, f, d), wmap),
            pl.BlockSpec((1, 1, d), wmap),
        ],
        out_specs=pl.BlockSpec((blk * pr, LANES), lambda i, be, br: (i, 0)),
        scratch_shapes=[
            pltpu.VMEM((2, blk * pr, LANES), U32),
            pltpu.VMEM((blk, d), BF16),
            pltpu.SemaphoreType.DMA((2,)),
        ],
    )
    src3 = src_rows.reshape(nb, 1, blk)
    return pl.pallas_call(
        functools.partial(_expert_kernel, blk=blk, nb=nb, f=f),
        out_shape=jax.ShapeDtypeStruct((nb * blk * pr, LANES), U32),
        grid_spec=grid_spec,
        compiler_params=_cparams(("arbitrary",)),
        name="moe_experts",
    )(block_e, block_rows, src3, src3, h_rows, w1p, b1p, w2b, b2r)


def _combine_ln_kernel(idx_cur_ref, idx_nxt_ref, y_hbm, tg_ref, x_ref, mod_ref, g_ref, b_ref, o_ref,
                       ybuf, sem, *, alpha, nt):
    i = pl.program_id(0)
    slot = i % 2
    tm, d = x_ref.shape
    pr = d // (2 * LANES)
    count = TOP_K * tm

    @pl.when(i == 0)
    def _():
        _gather_rows_start(y_hbm, idx_cur_ref, ybuf.at[0], sem.at[0], count=count, pr=pr)

    @pl.when(i + 1 < nt)
    def _():
        _gather_rows_start(y_hbm, idx_nxt_ref, ybuf.at[1 - slot], sem.at[1 - slot], count=count, pr=pr)

    _gather_rows_wait(y_hbm, ybuf.at[slot], sem.at[slot])
    tg = tg_ref[...]
    los, his = [None] * pr, [None] * pr
    for kk in range(TOP_K):
        gk = tg[:, kk:kk + 1]
        for c, (lo, hi) in enumerate(_load_packed_rows(ybuf.at[slot], tm, pr, base=kk * tm * pr)):
            los[c] = gk * lo if kk == 0 else los[c] + gk * lo
            his[c] = gk * hi if kk == 0 else his[c] + gk * hi
    acc = jnp.concatenate(los + his, axis=1)
    gate = mod_ref[0][5:6, :]
    o_ref[...] = _deepnorm_ln(x_ref[...], acc, gate, g_ref[...], b_ref[...], alpha)


def _combine_ln(y_rows, slot_rows, tg, x, mod, ln_g, ln_b, *, seq, alpha, tm):
    t, d = x.shape
    nt = t // tm
    pr = d // (2 * LANES)
    idx3 = slot_rows.reshape(nt, 1, TOP_K * tm)
    smem_blk = lambda fn: pl.BlockSpec((None, 1, TOP_K * tm), fn, memory_space=pltpu.SMEM)
    return pl.pallas_call(
        functools.partial(_combine_ln_kernel, alpha=alpha, nt=nt),
        out_shape=jax.ShapeDtypeStruct((t, d), F32),
        grid=(nt,),
        in_specs=[
            smem_blk(lambda i: (i, 0, 0)),
            smem_blk(lambda i: (jnp.minimum(i + 1, nt - 1), 0, 0)),
            pl.BlockSpec(memory_space=pl.ANY),
            pl.BlockSpec((tm, LANES), lambda i: (i, 0)),
            pl.BlockSpec((tm, d), lambda i: (i, 0)),
            pl.BlockSpec((1, 6, d), lambda i: ((i * tm) // seq, 0, 0)),
            pl.BlockSpec((1, d), lambda i: (0, 0)),
            pl.BlockSpec((1, d), lambda i: (0, 0)),
        ],
        out_specs=pl.BlockSpec((tm, d), lambda i: (i, 0)),
        scratch_shapes=[
            pltpu.VMEM((2, TOP_K * tm * pr, LANES), U32),
            pltpu.SemaphoreType.DMA((2,)),
        ],
        compiler_params=_cparams(("arbitrary",)),
        name="moe_combine_ln",
    )(idx3, idx3, y_rows, tg, x, mod, ln_g.reshape(1, d), ln_b.reshape(1, d))


def _dispatch_plan(top_i, *, t, ne, blk, row_sublanes, tm):
    n = t * TOP_K
    n_blocks = n // blk + ne
    n_pad = n_blocks * blk
    e_flat = top_i[:t, :TOP_K].reshape(n)
    e_sorted, order = lax.sort((e_flat, jnp.arange(n, dtype=I32)), num_keys=1, is_stable=True)
    counts = jnp.sum((e_flat[:, None] == jnp.arange(ne, dtype=I32)[None, :]).astype(I32), axis=0)
    starts = jnp.cumsum(counts) - counts
    padded = (counts + blk - 1) // blk * blk
    pad_ends = jnp.cumsum(padded)
    pad_starts = pad_ends - padded
    blk_start = jnp.arange(n_blocks, dtype=I32) * blk
    block_e = jnp.minimum(jnp.sum((pad_ends[None, :] <= blk_start[:, None]).astype(I32), axis=1), ne - 1)
    block_valid = (blk_start < pad_ends[-1]).astype(I32)
    e_slot = jnp.repeat(block_e, blk)
    v_slot = jnp.repeat(block_valid, blk)
    slot = jnp.arange(n_pad, dtype=I32)
    r = slot - pad_starts[e_slot]
    valid = jnp.logical_and(r < counts[e_slot], v_slot == 1)
    p = jnp.clip(starts[e_slot] + r, 0, n - 1)
    src_row = jnp.where(valid, order[p] // TOP_K, t)
    block_rows = jnp.clip(pad_starts[block_e] + counts[block_e] - blk_start, 0, blk) * block_valid
    slot_sorted = pad_starts[e_sorted] + jnp.arange(n, dtype=I32) - starts[e_sorted]
    _, slot_of = lax.sort((order, slot_sorted.astype(I32)), num_keys=1)
    slot_rows = slot_of.reshape(t // tm, tm, TOP_K).transpose(0, 2, 1).reshape(t // tm, TOP_K * tm)
    plan = (block_e, block_rows.astype(I32), (src_row * row_sublanes).astype(I32))
    return plan, (slot_rows * row_sublanes).astype(I32)


def _moe_ffn(x, mod, w_router, b_router, w1p, b1p, w2b, b2r, ln_g, ln_b, *, seq, alpha, e0):
    t, d = x.shape
    ne = w_router.shape[1]
    tm_r = min(512, seq)
    tm_c = min(256, seq)
    h_rows, top_i, top_g = _router(x, mod, w_router, b_router, seq=seq, tm=tm_r)
    plan, slot_rows = _dispatch_plan(top_i, t=t, ne=ne, blk=MOE_BLOCK, row_sublanes=d // (2 * LANES), tm=tm_c)
    y_rows = _experts(h_rows, plan, w1p, b1p, w2b, b2r, e0=e0, blk=MOE_BLOCK)
    return _combine_ln(y_rows, slot_rows, top_g, x, mod, ln_g, ln_b, seq=seq, alpha=alpha, tm=tm_c)


def kernel(x, c, ada_w, ada_b, ln_g, ln_b, mlstm_w_in, mlstm_b_gates, mlstm_norm_w, mlstm_w_out, hgrn_w_in, hgrn_lb_param, hgrn_norm_w, hgrn_w_out, ssm_w_in, ssm_conv_w, ssm_conv_b, ssm_dt_bias, ssm_a_log, ssm_d, ssm_norm_w, ssm_w_out, moe_w_router, moe_b_router, moe_w1, moe_b1, moe_w2, moe_b2):
    bsz, seq, d = x.shape
    depth = ada_w.shape[0]
    alpha = (2 * depth) ** 0.25
    mods = _ada_mods(c, ada_w, ada_b)
    ne, f2 = moe_w1.shape[1], moe_w1.shape[3]
    w1p = _prep_w1(moe_w1.reshape(depth * ne, d, f2))
    b1p = jnp.concatenate([moe_b1[..., 0::2], moe_b1[..., 1::2]], axis=-1).reshape(depth * ne, 1, f2)
    w2b = moe_w2.astype(BF16).reshape(depth * ne, f2 // 2, d)
    b2r = moe_b2.reshape(depth * ne, 1, d)
    xf = x.reshape(bsz * seq, d)
    for i in range(depth):
        mod = mods[i]
        kind, j = i % 3, i // 3
        if kind == 0:
            xf = _mlstm_mixer(xf, mod, mlstm_w_in[j], mlstm_b_gates[j], mlstm_norm_w[j], mlstm_w_out[j],
                              ln_g[i, 0], ln_b[i, 0], bsz=bsz, seq=seq, alpha=alpha)
        elif kind == 1:
            xf = _hgrn_mixer(xf, mod, hgrn_w_in[j], hgrn_lb_param, hgrn_norm_w[j], hgrn_w_out[j],
                             ln_g[i, 0], ln_b[i, 0], bsz=bsz, seq=seq, alpha=alpha, layer=i)
        else:
            xf = _ssd_mixer(xf, mod, ssm_w_in[j], ssm_conv_w[j], ssm_conv_b[j], ssm_dt_bias[j], ssm_a_log[j],
                            ssm_d[j], ssm_norm_w[j], ssm_w_out[j], ln_g[i, 0], ln_b[i, 0],
                            bsz=bsz, seq=seq, alpha=alpha)
        xf = _moe_ffn(xf, mod, moe_w_router[i], moe_b_router[i], w1p, b1p, w2b, b2r,
                      ln_g[i, 1], ln_b[i, 1], seq=seq, alpha=alpha, e0=i * ne)
    return xf.reshape(bsz, seq, d)
```
